```python
import jax, jax.numpy as jnp
from jax import lax
import numpy as np

D_MODEL = 4096
BATCH = 4
SEQ = 2048
DEPTH = 4
DEC_BATCH = 32
DEC_SEQ = 1
PAST_LEN = 8192
PAGE_SIZE = 128

HEAD_DIM = 128
BAND = 128
NORM_EPS = 1e-6
A_HEADS = 8
A_KV_HEADS = 2
A_GROUP = A_HEADS // A_KV_HEADS
A_WINDOW = BAND
B_GROUPS = ((128, 1), (512, 4), (2048, 16))
B_N_GROUPS = 3
B_HEADS_PER_GROUP = 8
C_HEADS = 6
C_QK_DIM = 128
C_V_DIM = 256
C_CHUNK = 128

A_WIDTH = A_HEADS * HEAD_DIM
A_KV_WIDTH = A_KV_HEADS * HEAD_DIM
B_QKV_WIDTH = B_N_GROUPS * B_HEADS_PER_GROUP * HEAD_DIM
B_WIDTH = B_HEADS_PER_GROUP * HEAD_DIM
C_QK_WIDTH = C_HEADS * C_QK_DIM
C_WIDTH = C_HEADS * C_V_DIM
IN_SPLITS = (A_WIDTH, A_KV_WIDTH, A_KV_WIDTH, A_WIDTH,
             B_QKV_WIDTH, B_QKV_WIDTH, B_QKV_WIDTH, B_WIDTH,
             C_QK_WIDTH, C_QK_WIDTH, C_WIDTH, C_WIDTH,
             D_MODEL, D_MODEL, D_MODEL)
IN_WIDTH = sum(IN_SPLITS)

kernel_name = 'hybrid_swa_dilated_retention_step'


def rmsnorm(x, w):
    xf = x.astype(jnp.float32)
    y = xf * lax.rsqrt(jnp.mean(xf * xf, axis=-1, keepdims=True) + NORM_EPS)
    return (y * w.astype(jnp.float32)).astype(x.dtype)


def alibi_slopes(n):
    return jnp.exp2(-8.0 * jnp.arange(1, n + 1, dtype=jnp.float32) / n)


def softmax_parts(logits, sink):
    m = jnp.max(logits, axis=-1)
    if sink is not None:
        m = jnp.maximum(m, sink)
    e = jnp.exp(logits - m[..., None])
    den = jnp.sum(e, axis=-1)
    if sink is not None:
        den = den + jnp.exp(sink - m)
    return e, den, m + jnp.log(den)


def banded_attention(q, k, v, slopes, step, sink=None):
    b, n, kvh, g, dh = q.shape
    nb = -(-n // BAND)
    pad = nb * BAND - n
    q = jnp.pad(q, ((0, 0), (0, pad), (0, 0), (0, 0), (0, 0))).reshape(b, nb, BAND, kvh, g, dh)
    k = jnp.pad(k, ((0, 0), (0, pad), (0, 0), (0, 0))).reshape(b, nb, BAND, kvh, dh)
    v = jnp.pad(v, ((0, 0), (0, pad), (0, 0), (0, 0))).reshape(b, nb, BAND, kvh, dh)

    def with_previous_block(a):
        prev = jnp.pad(a[:, :-1], ((0, 0), (1, 0), (0, 0), (0, 0), (0, 0)))
        return jnp.concatenate([prev, a], axis=2)

    k2, v2 = with_previous_block(k), with_previous_block(v)
    s = jnp.einsum('bnqhgd,bnkhd->bnhgqk', q, k2, preferred_element_type=jnp.float32) * (dh ** -0.5)
    qi = jnp.arange(BAND)[:, None] + BAND
    ki = jnp.arange(2 * BAND)[None, :]
    dist = qi - ki
    key_idx = jnp.arange(nb)[:, None, None] * BAND + ki[None] - BAND
    valid = (dist >= 0) & (dist <= BAND) & (key_idx >= 0)
    bias = -slopes.astype(jnp.float32)[:, :, None, None] * (step * dist).astype(jnp.float32)
    logits = jnp.where(valid[None, :, None, None], s + bias, -jnp.inf)
    sk = None if sink is None else sink.astype(jnp.float32)[:, :, None]
    e, den, lse = softmax_parts(logits, sk)
    o = jnp.einsum('bnhgqk,bnkhd->bnhgqd', e, v2.astype(jnp.float32)) / den[..., None]
    o = o.transpose(0, 1, 4, 2, 3, 5).reshape(b, nb * BAND, kvh, g, dh)[:, :n]
    lse = lse.transpose(0, 1, 4, 2, 3).reshape(b, nb * BAND, kvh, g)[:, :n]
    return o, lse


def strided_attention(q, kv_buf, kv_new, n_keys, slopes, step, sink=None):
    b, t, kvh, g, dh = q.shape
    n_buf = kv_buf.shape[1]
    j = jnp.arange(n_keys)
    pos_k = PAST_LEN + jnp.arange(t)[:, None] - step * j[None, :]
    idx = pos_k - (PAST_LEN - n_buf)
    valid = (pos_k >= 0) & (idx >= 0)
    in_buf = (idx < n_buf)[None, :, :, None, None, None]
    rows = jnp.where(in_buf, kv_buf[:, jnp.clip(idx, 0, n_buf - 1)],
                     kv_new[:, jnp.clip(idx - n_buf, 0, t - 1)])
    kg, vg = rows[:, :, :, 0], rows[:, :, :, 1]
    s = jnp.einsum('bthgd,btnhd->bthgn', q, kg, preferred_element_type=jnp.float32) * (dh ** -0.5)
    bias = -slopes.astype(jnp.float32)[:, :, None] * (step * j).astype(jnp.float32)
    logits = jnp.where(valid[None, :, None, None, :], s + bias, -jnp.inf)
    sk = None if sink is None else sink.astype(jnp.float32)
    e, den, lse = softmax_parts(logits, sk)
    o = jnp.einsum('bthgn,btnhd->bthgd', e, vg.astype(jnp.float32)) / den[..., None]
    return o, lse


def to_strided(a, d):
    b, s = a.shape[:2]
    a = a.reshape(b, s // d, d, *a.shape[2:])
    return jnp.swapaxes(a, 1, 2).reshape(b * d, s // d, *a.shape[3:])


def from_strided(a, b, d):
    n = a.shape[1]
    a = a.reshape(b, d, n, *a.shape[2:])
    return jnp.swapaxes(a, 1, 2).reshape(b, n * d, *a.shape[3:])


def retention_chunk(s0, q, k, v, log_gamma):
    c = q.shape[1]
    i = jnp.arange(c, dtype=jnp.float32)
    diff = i[:, None] - i[None, :]
    decay = jnp.where(diff >= 0, jnp.exp(log_gamma[:, None, None] * jnp.maximum(diff, 0.0)), 0.0)
    attn = jnp.einsum('bihd,bjhd->bhij', q, k) * decay[None]
    intra = jnp.einsum('bhij,bjhe->bihe', attn, v)
    q_dec = jnp.exp(log_gamma[None, :] * (i[:, None] + 1.0))
    cross = jnp.einsum('bihd,bhde->bihe', q * q_dec[None, :, :, None], s0)
    k_dec = jnp.exp(log_gamma[None, :] * (c - 1.0 - i[:, None]))
    s1 = jnp.exp(log_gamma * c)[None, :, None, None] * s0 + jnp.einsum('bjhd,bjhe->bhde', k * k_dec[None, :, :, None], v)
    return intra + cross, s1


def head_norm(o, w):
    mu = jnp.mean(o, axis=-1, keepdims=True)
    var = jnp.mean(jnp.square(o - mu), axis=-1, keepdims=True)
    y = (o - mu) * lax.rsqrt(var + NORM_EPS)
    return y.reshape(*o.shape[:-2], -1) * w.astype(jnp.float32)


def trunk_layer(x, norm_w, w_in, w_ba, w_bb, w_bc, w_out, sinks, ret_norm_w, past):
    f32 = jnp.float32
    bsz, t, _ = x.shape
    h = rmsnorm(x, norm_w)
    points = [int(p) for p in np.cumsum(IN_SPLITS)[:-1]]
    (qa, ka, va, za, qb, kb, vb, zb, qc, kc, vc, zc, ga, gb, gc) = jnp.split(
        jnp.einsum('btd,dn->btn', h, w_in), points, axis=-1)

    qa = qa.reshape(bsz, t, A_KV_HEADS, A_GROUP, HEAD_DIM)
    kv_a = jnp.stack([ka.reshape(bsz, t, A_KV_HEADS, HEAD_DIM), va.reshape(bsz, t, A_KV_HEADS, HEAD_DIM)], axis=2)
    slopes_a = alibi_slopes(A_HEADS).reshape(A_KV_HEADS, A_GROUP)
    sink_a = sinks.reshape(A_KV_HEADS, A_GROUP)
    if past is None:
        oa, _ = banded_attention(qa, kv_a[:, :, 0], kv_a[:, :, 1], slopes_a, 1, sink_a)
        new_a = kv_a[:, t - min(A_WINDOW, t):]
    else:
        oa, _ = strided_attention(qa, past[0], kv_a, A_WINDOW + 1, slopes_a, 1, sink_a)
        new_a = kv_a
    oa = oa.reshape(bsz, t, A_WIDTH).astype(x.dtype)

    qb = qb.reshape(bsz, t, B_N_GROUPS, B_HEADS_PER_GROUP, HEAD_DIM)
    kb = kb.reshape(bsz, t, B_N_GROUPS, B_HEADS_PER_GROUP, HEAD_DIM)
    vb = vb.reshape(bsz, t, B_N_GROUPS, B_HEADS_PER_GROUP, HEAD_DIM)
    slopes_b = alibi_slopes(B_N_GROUPS * B_HEADS_PER_GROUP).reshape(B_N_GROUPS, B_HEADS_PER_GROUP)
    outs, lses, new_b = [], [], []
    for g, (win, dil) in enumerate(B_GROUPS):
        qg, kg, vg = qb[:, :, g], kb[:, :, g], vb[:, :, g]
        kv_g = jnp.stack([kg, vg], axis=2)
        sl = slopes_b[g][:, None]
        if past is None:
            o, lse = banded_attention(to_strided(qg, dil)[:, :, :, None], to_strided(kg, dil),
                                      to_strided(vg, dil), sl, dil)
            o = from_strided(o[:, :, :, 0], bsz, dil)
            lse = from_strided(lse[..., 0], bsz, dil)
            new_b.append(kv_g[:, t - min(win, t):])
        else:
            o, lse = strided_attention(qg[:, :, :, None], past[1 + g], kv_g, win // dil + 1, sl, dil)
            o, lse = o[:, :, :, 0], lse[..., 0]
            new_b.append(kv_g)
        outs.append(o)
        lses.append(lse)
    alpha = jax.nn.softmax(jnp.stack(lses), axis=0)
    ob = jnp.sum(alpha[..., None] * jnp.stack(outs), axis=0).reshape(bsz, t, B_WIDTH).astype(x.dtype)

    log_gamma = jnp.log1p(-jnp.exp2(-5.0 - jnp.arange(C_HEADS, dtype=f32)))
    qc = qc.reshape(bsz, t, C_HEADS, C_QK_DIM).astype(f32)
    kc = kc.reshape(bsz, t, C_HEADS, C_QK_DIM).astype(f32) * (C_QK_DIM ** -0.5)
    vc = vc.reshape(bsz, t, C_HEADS, C_V_DIM).astype(f32)
    if past is None:
        nch = t // C_CHUNK

        def chunks(a):
            return jnp.swapaxes(a.reshape(bsz, nch, C_CHUNK, *a.shape[2:]), 0, 1)

        def body(state, xs):
            o, s_new = retention_chunk(state, xs[0], xs[1], xs[2], log_gamma)
            return s_new, o

        s0 = jnp.zeros((bsz, C_HEADS, C_QK_DIM, C_V_DIM), f32)
        new_c, oc = lax.scan(body, s0, (chunks(qc), chunks(kc), chunks(vc)))
        oc = jnp.swapaxes(oc, 0, 1).reshape(bsz, t, C_HEADS, C_V_DIM)
    else:
        oc, new_c = retention_chunk(past[4].astype(f32), qc, kc, vc, log_gamma)
    oc = head_norm(oc, ret_norm_w).astype(x.dtype)

    ua = jnp.einsum('btw,wd->btd', oa * jax.nn.silu(za), w_ba)
    ub = jnp.einsum('btw,wd->btd', ob * jax.nn.silu(zb), w_bb)
    uc = jnp.einsum('btw,wd->btd', oc * jax.nn.silu(zc), w_bc)
    mixed = jax.nn.sigmoid(ga) * ua + jax.nn.sigmoid(gb) * ub + jax.nn.sigmoid(gc) * uc
    y = x + jnp.einsum('btd,de->bte', mixed, w_out)
    return y, (new_a, new_b[0], new_b[1], new_b[2], new_c.astype(x.dtype))


def setup_inputs(seed: int = 0) -> dict:
    key = jax.random.key(seed)
    ks = jax.random.split(key, 16)

    def nrm(k, shape, scale):
        return scale * jax.random.normal(k, shape, jnp.float32)

    la = min(A_WINDOW, PAST_LEN)
    lb = [min(w, PAST_LEN) for w, _ in B_GROUPS]
    kvb = (2, B_HEADS_PER_GROUP, HEAD_DIM)
    return {
        'x_prompt': nrm(ks[0], (BATCH, SEQ, D_MODEL), 1.0),
        'x_sample': nrm(ks[1], (DEC_BATCH, DEC_SEQ, D_MODEL), 1.0),
        'cache_a': nrm(ks[2], (DEPTH, DEC_BATCH, la, 2, A_KV_HEADS, HEAD_DIM), 1.0),
        'cache_b1': nrm(ks[3], (DEPTH, DEC_BATCH, lb[0]) + kvb, 1.0),
        'cache_b2': nrm(ks[4], (DEPTH, DEC_BATCH, lb[1]) + kvb, 1.0),
        'cache_b3': nrm(ks[5], (DEPTH, DEC_BATCH, lb[2]) + kvb, 1.0),
        'state_c': nrm(ks[6], (DEPTH, DEC_BATCH, C_HEADS, C_QK_DIM, C_V_DIM), 1.0),
        'norm_w': 1.0 + nrm(ks[7], (DEPTH, D_MODEL), 0.02),
        'w_in': nrm(ks[8], (DEPTH, D_MODEL, IN_WIDTH), D_MODEL ** -0.5),
        'w_branch_a': nrm(ks[9], (DEPTH, A_WIDTH, D_MODEL), A_WIDTH ** -0.5),
        'w_branch_b': nrm(ks[10], (DEPTH, B_WIDTH, D_MODEL), B_WIDTH ** -0.5),
        'w_branch_c': nrm(ks[11], (DEPTH, C_WIDTH, D_MODEL), C_WIDTH ** -0.5),
        'w_out': nrm(ks[12], (DEPTH, D_MODEL, D_MODEL), D_MODEL ** -0.5),
        'attn_sinks': nrm(ks[13], (DEPTH, A_HEADS), 1.0),
        'ret_norm_w': 1.0 + nrm(ks[14], (DEPTH, C_WIDTH), 0.02),
        'final_norm_w': 1.0 + nrm(ks[15], (D_MODEL,), 0.02),
    }


def reference(x_prompt, x_sample, cache_a, cache_b1, cache_b2, cache_b3, state_c, norm_w, w_in,
              w_branch_a, w_branch_b, w_branch_c, w_out, attn_sinks, ret_norm_w, final_norm_w):
    xp, xs = x_prompt, x_sample
    p_states, s_states = [], []
    for l in range(DEPTH):
        weights = (norm_w[l], w_in[l], w_branch_a[l], w_branch_b[l], w_branch_c[l], w_out[l],
                   attn_sinks[l], ret_norm_w[l])
        xp, sp = trunk_layer(xp, *weights, None)
        xs, ss = trunk_layer(xs, *weights, (cache_a[l], cache_b1[l], cache_b2[l], cache_b3[l], state_c[l]))
        p_states.append(sp)
        s_states.append(ss)
    y_prompt = rmsnorm(xp, final_norm_w)
    y_sample = rmsnorm(xs, final_norm_w)
    a_p = jnp.stack([s[0] for s in p_states])
    b1_p = jnp.stack([s[1] for s in p_states])
    b2_p = jnp.stack([s[2] for s in p_states])
    b3_p = jnp.stack([s[3] for s in p_states])
    c_p = jnp.stack([s[4] for s in p_states])
    a_s = jnp.stack([s[0] for s in s_states])
    b1_s = jnp.stack([s[1] for s in s_states])
    b2_s = jnp.stack([s[2] for s in s_states])
    b3_s = jnp.stack([s[3] for s in s_states])
    c_s = jnp.stack([s[4] for s in s_states])
    return (y_prompt, y_sample, a_p, b1_p, b2_p, b3_p, c_p, a_s, b1_s, b2_s, b3_s, c_s)
```

```python
import functools

import jax
import jax.numpy as jnp
from jax import lax
from jax.experimental import pallas as pl
from jax.experimental.pallas import tpu as pltpu

F32 = jnp.float32
BF16 = jnp.bfloat16

HEAD_DIM = 128
BAND = 128
NORM_EPS = 1e-6
A_HEADS = 8
A_KV_HEADS = 2
A_GROUP = A_HEADS // A_KV_HEADS
B_GROUPS = ((128, 1), (512, 4), (2048, 16))
B_HEADS = 8
C_HEADS = 6
C_QK_DIM = 128
C_V_DIM = 256
C_CHUNK = 128

A_WIDTH = A_HEADS * HEAD_DIM
A_KV_WIDTH = A_KV_HEADS * HEAD_DIM
B_WIDTH = B_HEADS * HEAD_DIM
B_QKV_WIDTH = len(B_GROUPS) * B_WIDTH
C_QK_WIDTH = C_HEADS * C_QK_DIM
C_WIDTH = C_HEADS * C_V_DIM

OFF_QA = 0
OFF_KA = OFF_QA + A_WIDTH
OFF_VA = OFF_KA + A_KV_WIDTH
OFF_ZA = OFF_VA + A_KV_WIDTH
OFF_QB = OFF_ZA + A_WIDTH
OFF_KB = OFF_QB + B_QKV_WIDTH
OFF_VB = OFF_KB + B_QKV_WIDTH
OFF_ZB = OFF_VB + B_QKV_WIDTH
OFF_QC = OFF_ZB + B_WIDTH
OFF_KC = OFF_QC + C_QK_WIDTH
OFF_VC = OFF_KC + C_QK_WIDTH
OFF_ZC = OFF_VC + C_WIDTH
OFF_G = OFF_ZC + C_WIDTH

HALF = 512
NEG_BIG = -1e30
SCALE = HEAD_DIM ** -0.5
VMEM_LIMIT = 56 * 1024 * 1024
SAMPLE_CHUNK = 4


def _params(*sem):
    return pltpu.CompilerParams(dimension_semantics=sem, vmem_limit_bytes=VMEM_LIMIT)


def _smem():
    return pl.BlockSpec(memory_space=pltpu.SMEM)


def _rmsnorm_kernel(x_ref, w_ref, o_ref):
    x = x_ref[...].astype(F32)
    ms = jnp.mean(x * x, axis=-1, keepdims=True)
    o_ref[...] = (x * lax.rsqrt(ms + NORM_EPS) * w_ref[...].astype(F32)).astype(o_ref.dtype)


def _rmsnorm(x, w_row, out_dtype):
    m, d = x.shape
    tr = min(m, 256)
    return pl.pallas_call(
        _rmsnorm_kernel,
        grid=(m // tr,),
        in_specs=[pl.BlockSpec((tr, d), lambda i: (i, 0)),
                  pl.BlockSpec((1, d), lambda i: (0, 0))],
        out_specs=pl.BlockSpec((tr, d), lambda i: (i, 0)),
        out_shape=jax.ShapeDtypeStruct((m, d), out_dtype),
        compiler_params=_params("arbitrary"),
        name="rmsnorm",
    )(x, w_row)


def _matmul_kernel(x_ref, w_ref, *rest, has_residual):
    if has_residual:
        r_ref, o_ref, wb_ref = rest
    else:
        o_ref, wb_ref = rest

    @pl.when(pl.program_id(1) == 0)
    def _():
        wb_ref[...] = w_ref[...].astype(BF16)

    acc = jnp.dot(x_ref[...], wb_ref[...], preferred_element_type=F32)
    if has_residual:
        acc = acc + r_ref[...]
    o_ref[...] = acc.astype(o_ref.dtype)


def _matmul(x, w_stack, layer, out_dtype, residual=None):
    m, k = x.shape
    n = w_stack.shape[-1]
    tm = min(m, 1024)
    tn = HALF
    in_specs = [pl.BlockSpec((tm, k), lambda j, i: (i, 0)),
                pl.BlockSpec((None, k, tn), lambda j, i: (layer, 0, j))]
    args = [x, w_stack]
    if residual is not None:
        in_specs.append(pl.BlockSpec((tm, tn), lambda j, i: (i, j)))
        args.append(residual)
    return pl.pallas_call(
        functools.partial(_matmul_kernel, has_residual=residual is not None),
        grid=(n // tn, m // tm),
        in_specs=in_specs,
        out_specs=pl.BlockSpec((tm, tn), lambda j, i: (i, j)),
        out_shape=jax.ShapeDtypeStruct((m, n), out_dtype),
        scratch_shapes=[pltpu.VMEM((k, tn), BF16)],
        compiler_params=_params("arbitrary", "arbitrary"),
        name="matmul",
    )(*args)


def _branch_kernel(ya_ref, yb_ref, yc_ref, wa_ref, wb_ref, wc_ref, ga_ref, gb_ref, gc_ref,
                   o_ref, wa_s, wb_s, wc_s):
    @pl.when(pl.program_id(1) == 0)
    def _():
        wa_s[...] = wa_ref[...].astype(BF16)
        wb_s[...] = wb_ref[...].astype(BF16)
        wc_s[...] = wc_ref[...].astype(BF16)

    def gated(y_ref, w_s, g_ref):
        u = jnp.dot(y_ref[...].astype(BF16), w_s[...], preferred_element_type=F32)
        return jax.nn.sigmoid(g_ref[...].astype(F32)) * u

    mixed = gated(ya_ref, wa_s, ga_ref) + gated(yb_ref, wb_s, gb_ref) + gated(yc_ref, wc_s, gc_ref)
    o_ref[...] = mixed.astype(o_ref.dtype)


def _branch_merge(ya, yb, yc, w_a, w_b, w_c, proj, layer, d_model):
    m = ya.shape[0]
    tm = min(m, 512)
    tn = HALF
    g0 = OFF_G // tn
    nd = d_model // tn

    def y_spec(width):
        return pl.BlockSpec((tm, width), lambda j, i: (i, 0))

    def w_spec(width):
        return pl.BlockSpec((None, width, tn), lambda j, i: (layer, 0, j))

    def g_spec(idx):
        return pl.BlockSpec((tm, tn), lambda j, i: (i, g0 + idx * nd + j))

    return pl.pallas_call(
        _branch_kernel,
        grid=(d_model // tn, m // tm),
        in_specs=[y_spec(A_WIDTH), y_spec(B_WIDTH), y_spec(C_WIDTH),
                  w_spec(A_WIDTH), w_spec(B_WIDTH), w_spec(C_WIDTH),
                  g_spec(0), g_spec(1), g_spec(2)],
        out_specs=pl.BlockSpec((tm, tn), lambda j, i: (i, j)),
        out_shape=jax.ShapeDtypeStruct((m, d_model), BF16),
        scratch_shapes=[pltpu.VMEM((A_WIDTH, tn), BF16), pltpu.VMEM((B_WIDTH, tn), BF16),
                        pltpu.VMEM((C_WIDTH, tn), BF16)],
        compiler_params=_params("arbitrary", "arbitrary"),
        name="branch_merge",
    )(ya, yb, yc, w_a, w_b, w_c, proj, proj, proj)


def _band_masks():
    row = lax.broadcasted_iota(jnp.int32, (BAND, BAND), 0)
    col = lax.broadcasted_iota(jnp.int32, (BAND, BAND), 1)
    dist_cur = (row - col).astype(F32)
    return row, col, dist_cur, dist_cur + float(BAND)


def _qk(q, k):
    return lax.dot_general(q, k, (((1,), (1,)), ((), ())), preferred_element_type=F32) * SCALE


def _band_head(q, k_cur, v_cur, k_prev, v_prev, slope_step, has_prev, sink):
    row, col, dist_cur, dist_prev = _band_masks()
    logit_c = jnp.where(col <= row, _qk(q, k_cur) - slope_step * dist_cur, NEG_BIG)
    m = jnp.max(logit_c, axis=-1, keepdims=True)
    if k_prev is not None:
        logit_p = jnp.where(col >= row, _qk(q, k_prev) - slope_step * dist_prev, NEG_BIG)
        logit_p = logit_p + jnp.where(has_prev, 0.0, NEG_BIG)
        m = jnp.maximum(m, jnp.max(logit_p, axis=-1, keepdims=True))
    if sink is not None:
        m = jnp.maximum(m, sink)
    e_c = jnp.exp(logit_c - m)
    den = jnp.sum(e_c, axis=-1, keepdims=True)
    acc = jnp.dot(e_c.astype(BF16), v_cur, preferred_element_type=F32)
    if k_prev is not None:
        e_p = jnp.exp(logit_p - m)
        den = den + jnp.sum(e_p, axis=-1, keepdims=True)
        acc = acc + jnp.dot(e_p.astype(BF16), v_prev, preferred_element_type=F32)
    if sink is not None:
        den = den + jnp.exp(sink - m)
    return acc, m, den


def _part(refs, h):
    per = HALF // HEAD_DIM
    return refs[h // per][:, (h % per) * HEAD_DIM:(h % per + 1) * HEAD_DIM]


def _attn_a_kernel(slopes_ref, sinks_ref, q0, q1, kc_ref, kp_ref, vc_ref, vp_ref, z0, z1, o_ref):
    has_prev = pl.program_id(1) > 0
    for h in range(A_HEADS):
        kv = h // A_GROUP
        sl = slice(kv * HEAD_DIM, (kv + 1) * HEAD_DIM)
        acc, _, den = _band_head(_part((q0, q1), h), kc_ref[:, sl], vc_ref[:, sl], kp_ref[:, sl],
                                 vp_ref[:, sl], slopes_ref[h], has_prev, sinks_ref[h])
        z = _part((z0, z1), h).astype(F32)
        o_ref[:, h * HEAD_DIM:(h + 1) * HEAD_DIM] = ((acc / den) * (z * jax.nn.sigmoid(z))).astype(o_ref.dtype)


def _attn_a(proj, slopes, sinks, batch, seq):
    nb = seq // BAND

    def cur(off, width):
        return pl.BlockSpec((BAND, width), lambda b, n: (b * nb + n, off // width))

    def prev(off, width):
        return pl.BlockSpec((BAND, width), lambda b, n: (b * nb + jnp.maximum(n - 1, 0), off // width))

    return pl.pallas_call(
        _attn_a_kernel,
        grid=(batch, nb),
        in_specs=[_smem(), _smem(),
                  cur(OFF_QA, HALF), cur(OFF_QA + HALF, HALF),
                  cur(OFF_KA, A_KV_WIDTH), prev(OFF_KA, A_KV_WIDTH),
                  cur(OFF_VA, A_KV_WIDTH), prev(OFF_VA, A_KV_WIDTH),
                  cur(OFF_ZA, HALF), cur(OFF_ZA + HALF, HALF)],
        out_specs=pl.BlockSpec((BAND, A_WIDTH), lambda b, n: (b * nb + n, 0)),
        out_shape=jax.ShapeDtypeStruct((batch * seq, A_WIDTH), BF16),
        compiler_params=_params("arbitrary", "arbitrary"),
        name="attn_a",
    )(slopes, sinks, *([proj] * 8))


def _attn_b_kernel(*refs, dil, has_prev_block, has_carry, is_last):
    slopes_ref = refs[0]
    q = refs[1:3]
    kc = refs[3:5]
    vc = refs[5:7]
    pos = 7
    if has_prev_block:
        kp, vp = refs[7:9], refs[9:11]
        pos = 11
    if has_carry:
        o_in, lse_in = refs[pos], refs[pos + 1]
        pos += 2
    if is_last:
        z = refs[pos:pos + 2]
        y_ref = refs[pos + 2]
    else:
        o_out, lse_out = refs[pos], refs[pos + 1]
    has_prev = pl.program_id(2) > 0
    for h in range(B_HEADS):
        sl = slice(h * HEAD_DIM, (h + 1) * HEAD_DIM)
        acc, m, den = _band_head(
            _part(q, h), _part(kc, h), _part(vc, h),
            _part(kp, h) if has_prev_block else None,
            _part(vp, h) if has_prev_block else None,
            slopes_ref[h] * float(dil), has_prev, None)
        o = acc / den
        lse = m + jnp.log(den)
        if has_carry:
            lse_c = lse_in[:, sl]
            lse_new = jnp.maximum(lse_c, lse) + jnp.log(1.0 + jnp.exp(-jnp.abs(lse_c - lse)))
            o = o_in[:, sl] * jnp.exp(lse_c - lse_new) + o * jnp.exp(lse - lse_new)
            lse = lse_new
        if is_last:
            zh = _part(z, h).astype(F32)
            y_ref[:, sl] = (o * (zh * jax.nn.sigmoid(zh))).astype(y_ref.dtype)
        else:
            o_out[:, sl] = o
            lse_out[:, sl] = jnp.broadcast_to(lse, (BAND, HEAD_DIM))


def _attn_b_group(proj, slopes_g, g, batch, seq, width, carry, is_last):
    win, dil = B_GROUPS[g]
    n_strided = seq // dil
    nb = n_strided // BAND
    has_prev_block = nb > 1
    proj_v = proj.reshape(batch * n_strided, dil * width)
    wpb = width // HALF

    def cur(off):
        return pl.BlockSpec((BAND, HALF), lambda b, r, n: (b * nb + n, r * wpb + off // HALF))

    def prev(off):
        return pl.BlockSpec((BAND, HALF),
                            lambda b, r, n: (b * nb + jnp.maximum(n - 1, 0), r * wpb + off // HALF))

    nat = pl.BlockSpec((BAND, B_WIDTH), lambda b, r, n: (b * nb + n, r))
    oq, ok, ov = (o + g * B_WIDTH for o in (OFF_QB, OFF_KB, OFF_VB))
    in_specs = [_smem(), cur(oq), cur(oq + HALF), cur(ok), cur(ok + HALF), cur(ov), cur(ov + HALF)]
    args = [slopes_g] + [proj_v] * 6
    if has_prev_block:
        in_specs += [prev(ok), prev(ok + HALF), prev(ov), prev(ov + HALF)]
        args += [proj_v] * 4
    if carry is not None:
        in_specs += [nat, nat]
        args += [c.reshape(batch * n_strided, dil * B_WIDTH) for c in carry]
    m_rows = batch * seq
    if is_last:
        in_specs += [cur(OFF_ZB), cur(OFF_ZB + HALF)]
        args += [proj_v] * 2
        out_specs = nat
        out_shape = jax.ShapeDtypeStruct((batch * n_strided, dil * B_WIDTH), BF16)
    else:
        out_specs = [nat, nat]
        out_shape = [jax.ShapeDtypeStruct((batch * n_strided, dil * B_WIDTH), F32)] * 2
    out = pl.pallas_call(
        functools.partial(_attn_b_kernel, dil=dil, has_prev_block=has_prev_block,
                          has_carry=carry is not None, is_last=is_last),
        grid=(batch, dil, nb),
        in_specs=in_specs,
        out_specs=out_specs,
        out_shape=out_shape,
        compiler_params=_params("arbitrary", "arbitrary", "arbitrary"),
        name=f"attn_b{g + 1}",
    )(*args)
    if is_last:
        return out.reshape(m_rows, B_WIDTH)
    return tuple(o.reshape(m_rows, B_WIDTH) for o in out)


def _attn_b(proj, slopes_b, batch, seq, width):
    carry = _attn_b_group(proj, slopes_b[2], 2, batch, seq, width, None, False)
    carry = _attn_b_group(proj, slopes_b[1], 1, batch, seq, width, carry, False)
    return _attn_b_group(proj, slopes_b[0], 0, batch, seq, width, carry, True)


def _head_norm_gate(o, w_row, z):
    mu = jnp.mean(o, axis=-1, keepdims=True)
    var = jnp.mean(jnp.square(o - mu), axis=-1, keepdims=True)
    y = (o - mu) * lax.rsqrt(var + NORM_EPS) * w_row
    return y * (z * jax.nn.sigmoid(z))


def _retention_kernel(lg_ref, q_ref, k_ref, v_ref, z_ref, w_ref, y_ref, s_ref):
    c = pl.program_id(2)
    lg = lg_ref[pl.program_id(1)]

    @pl.when(c == 0)
    def _():
        s_ref[...] = jnp.zeros_like(s_ref)

    row = lax.broadcasted_iota(jnp.int32, (C_CHUNK, C_CHUNK), 0)
    col = lax.broadcasted_iota(jnp.int32, (C_CHUNK, C_CHUNK), 1)
    diff = (row - col).astype(F32)
    decay = jnp.where(diff >= 0, jnp.exp(lg * jnp.maximum(diff, 0.0)), 0.0)
    pos = lax.broadcasted_iota(jnp.int32, (C_CHUNK, 1), 0).astype(F32)
    q_dec = jnp.exp(lg * (pos + 1.0))
    k_dec = jnp.exp(lg * (float(C_CHUNK) - 1.0 - pos))

    q = q_ref[...].astype(F32)
    k = k_ref[...].astype(F32) * (C_QK_DIM ** -0.5)
    v = v_ref[...].astype(BF16)
    s0 = s_ref[...]
    attn = lax.dot_general(q.astype(BF16), k.astype(BF16), (((1,), (1,)), ((), ())),
                           preferred_element_type=F32) * decay
    intra = jnp.dot(attn.astype(BF16), v, preferred_element_type=F32)
    cross = jnp.dot((q * q_dec).astype(BF16), s0.astype(BF16), preferred_element_type=F32)
    kd_t = jnp.transpose(k * k_dec).astype(BF16)
    s_ref[...] = jnp.exp(lg * float(C_CHUNK)) * s0 + jnp.dot(kd_t, v, preferred_element_type=F32)
    y_ref[...] = _head_norm_gate(intra + cross, w_ref[...].astype(F32),
                                 z_ref[...].astype(F32)).astype(y_ref.dtype)


def _retention(proj, log_gamma, ret_w_row, batch, seq):
    nch = seq // C_CHUNK

    def tok(off, width):
        return pl.BlockSpec((C_CHUNK, width), lambda b, h, c: (b * nch + c, off // width + h))

    return pl.pallas_call(
        _retention_kernel,
        grid=(batch, C_HEADS, nch),
        in_specs=[_smem(), tok(OFF_QC, C_QK_DIM), tok(OFF_KC, C_QK_DIM), tok(OFF_VC, C_V_DIM),
                  tok(OFF_ZC, C_V_DIM), pl.BlockSpec((1, C_V_DIM), lambda b, h, c: (0, h))],
        out_specs=[pl.BlockSpec((C_CHUNK, C_V_DIM), lambda b, h, c: (b * nch + c, h)),
                   pl.BlockSpec((None, None, C_QK_DIM, C_V_DIM), lambda b, h, c: (b, h, 0, 0))],
        out_shape=[jax.ShapeDtypeStruct((batch * seq, C_WIDTH), BF16),
                   jax.ShapeDtypeStruct((batch, C_HEADS, C_QK_DIM, C_V_DIM), F32)],
        compiler_params=_params("arbitrary", "arbitrary", "arbitrary"),
        name="retention",
    )(log_gamma, proj, proj, proj, proj, ret_w_row)


def _sample_head(q, kbuf, vbuf, knew, vnew, slope_step, sink):
    steps_back = float(BAND) - lax.broadcasted_iota(jnp.int32, (1, BAND, 1), 1).astype(F32)
    logit_b = jnp.sum(kbuf * q, axis=-1, keepdims=True) * SCALE - slope_step * steps_back
    logit_n = jnp.sum(knew * q, axis=-1, keepdims=True) * SCALE
    m = jnp.maximum(jnp.max(logit_b, axis=1, keepdims=True), logit_n)
    if sink is not None:
        m = jnp.maximum(m, sink)
    e_b = jnp.exp(logit_b - m)
    e_n = jnp.exp(logit_n - m)
    den = jnp.sum(e_b, axis=1, keepdims=True) + e_n
    if sink is not None:
        den = den + jnp.exp(sink - m)
    acc = jnp.sum(e_b * vbuf, axis=1, keepdims=True) + e_n * vnew
    return acc, m, den


def _column(row_vec):
    n = row_vec.shape[-1]
    eye = lax.broadcasted_iota(jnp.int32, (n, n), 0) == lax.broadcasted_iota(jnp.int32, (n, n), 1)
    return jnp.sum(jnp.where(eye, row_vec, 0.0), axis=-1, keepdims=True)


def _sample_kernel(slopes_a, sinks_a, slopes_b, lg_ref, p_ref, ca_ref, cb1_ref, cb2_ref, cb3_ref,
                   st_ref, w_ref, y_ref, st_out):
    def sec(off, width=HEAD_DIM):
        return p_ref[:, :, off:off + width].astype(F32)

    def silu(z):
        return z * jax.nn.sigmoid(z)

    for h in range(A_HEADS):
        kv = h // A_GROUP
        ko, vo = kv * HEAD_DIM, (A_KV_HEADS + kv) * HEAD_DIM
        acc, _, den = _sample_head(
            sec(OFF_QA + h * HEAD_DIM), ca_ref[:, :, ko:ko + HEAD_DIM], ca_ref[:, :, vo:vo + HEAD_DIM],
            sec(OFF_KA + kv * HEAD_DIM), sec(OFF_VA + kv * HEAD_DIM), slopes_a[h], sinks_a[h])
        y_ref[:, :, h * HEAD_DIM:(h + 1) * HEAD_DIM] = (acc / den) * silu(sec(OFF_ZA + h * HEAD_DIM))

    caches = (cb1_ref, cb2_ref, cb3_ref)
    for h in range(B_HEADS):
        parts = []
        for g, (_, dil) in enumerate(B_GROUPS):
            ko, vo = h * HEAD_DIM, (B_HEADS + h) * HEAD_DIM
            col = g * B_WIDTH + h * HEAD_DIM
            parts.append(_sample_head(
                sec(OFF_QB + col), caches[g][:, :, ko:ko + HEAD_DIM], caches[g][:, :, vo:vo + HEAD_DIM],
                sec(OFF_KB + col), sec(OFF_VB + col), slopes_b[g * B_HEADS + h] * float(dil), None))
        m_all = jnp.maximum(jnp.maximum(parts[0][1], parts[1][1]), parts[2][1])
        num = sum(acc * jnp.exp(m - m_all) for acc, m, _ in parts)
        den = sum(d * jnp.exp(m - m_all) for _, m, d in parts)
        y_ref[:, :, A_WIDTH + h * HEAD_DIM:A_WIDTH + (h + 1) * HEAD_DIM] = (
            (num / den) * silu(sec(OFF_ZB + h * HEAD_DIM)))

    for h in range(C_HEADS):
        gamma = jnp.exp(lg_ref[h])
        vo = h * C_V_DIM
        for i in range(SAMPLE_CHUNK):
            q = p_ref[i, :, OFF_QC + h * C_QK_DIM:OFF_QC + (h + 1) * C_QK_DIM].astype(F32)
            k = p_ref[i, :, OFF_KC + h * C_QK_DIM:OFF_KC + (h + 1) * C_QK_DIM].astype(F32) * (C_QK_DIM ** -0.5)
            v = p_ref[i, :, OFF_VC + vo:OFF_VC + vo + C_V_DIM].astype(F32)
            z = p_ref[i, :, OFF_ZC + vo:OFF_ZC + vo + C_V_DIM].astype(F32)
            s0 = st_ref[i, h]
            intra = jnp.sum(q * k, axis=-1, keepdims=True) * v
            cross = jnp.sum(_column(q * gamma) * s0, axis=0, keepdims=True)
            st_out[i, h] = gamma * s0 + _column(k) * v
            y_ref[i, :, A_WIDTH + B_WIDTH + vo:A_WIDTH + B_WIDTH + vo + C_V_DIM] = _head_norm_gate(
                intra + cross, w_ref[:, vo:vo + C_V_DIM].astype(F32), z)


def _sample_mixers(proj_s, caches, state, layer, slopes_a, sinks, slopes_b, log_gamma, ret_w_row):
    n, width = proj_s.shape
    nc = SAMPLE_CHUNK
    p4 = proj_s.reshape(n // nc, nc, 1, width)
    kv_a = 2 * A_KV_WIDTH
    kv_b = 2 * B_WIDTH
    views = [caches[0].reshape(caches[0].shape[0], n, BAND, kv_a)]
    for g, (win, dil) in enumerate(B_GROUPS):
        c = caches[1 + g]
        views.append(c.reshape(c.shape[0], n, BAND, dil * kv_b))

    def cache_spec(w):
        return pl.BlockSpec((None, nc, BAND, w), lambda i: (layer, i, 0, 0))

    y_width = A_WIDTH + B_WIDTH + C_WIDTH
    y, st = pl.pallas_call(
        _sample_kernel,
        grid=(n // nc,),
        in_specs=[_smem(), _smem(), _smem(), _smem(),
                  pl.BlockSpec((None, nc, 1, width), lambda i: (i, 0, 0, 0)),
                  cache_spec(kv_a), cache_spec(kv_b), cache_spec(kv_b), cache_spec(kv_b),
                  pl.BlockSpec((None, nc, C_HEADS, C_QK_DIM, C_V_DIM), lambda i: (layer, i, 0, 0, 0)),
                  pl.BlockSpec((1, C_WIDTH), lambda i: (0, 0))],
        out_specs=[pl.BlockSpec((None, nc, 1, y_width), lambda i: (i, 0, 0, 0)),
                   pl.BlockSpec((nc, C_HEADS, C_QK_DIM, C_V_DIM), lambda i: (i, 0, 0, 0))],
        out_shape=[jax.ShapeDtypeStruct((n // nc, nc, 1, y_width), F32),
                   jax.ShapeDtypeStruct((n, C_HEADS, C_QK_DIM, C_V_DIM), F32)],
        compiler_params=_params("arbitrary"),
        name="sample_mixers",
    )(slopes_a, sinks, slopes_b, log_gamma, p4, *views, state, ret_w_row)
    return y.reshape(n, y_width), st


def _kv_rows(proj3, off_k, off_v, heads, rows):
    b, t, _ = proj3.shape
    w = heads * HEAD_DIM
    k = proj3[:, t - rows:, off_k:off_k + w]
    v = proj3[:, t - rows:, off_v:off_v + w]
    return jnp.stack([k, v], axis=2).reshape(b, rows, 2, heads, HEAD_DIM).astype(F32)


def _window_rows(proj, batch, seq):
    p3 = proj.reshape(batch, seq, proj.shape[-1])
    out = [_kv_rows(p3, OFF_KA, OFF_VA, A_KV_HEADS, min(BAND, seq))]
    for g, (win, _) in enumerate(B_GROUPS):
        out.append(_kv_rows(p3, OFF_KB + g * B_WIDTH, OFF_VB + g * B_WIDTH, B_HEADS, min(win, seq)))
    return out


def kernel(x_prompt, x_sample, cache_a, cache_b1, cache_b2, cache_b3, state_c, norm_w, w_in,
           w_branch_a, w_branch_b, w_branch_c, w_out, attn_sinks, ret_norm_w, final_norm_w):
    batch, seq, d_model = x_prompt.shape
    n_s, t_s, _ = x_sample.shape
    depth = w_in.shape[0]
    width = w_in.shape[-1]
    assert t_s == 1 and seq % (B_GROUPS[-1][1] * BAND) == 0 and n_s % SAMPLE_CHUNK == 0
    assert width == OFF_G + 3 * d_model and d_model % HALF == 0
    assert cache_a.shape[2] == BAND
    assert all(c.shape[2] == win for c, (win, _) in zip((cache_b1, cache_b2, cache_b3), B_GROUPS))

    slopes_a = jnp.exp2(-8.0 * jnp.arange(1, A_HEADS + 1, dtype=F32) / A_HEADS)
    nbh = len(B_GROUPS) * B_HEADS
    slopes_b = jnp.exp2(-8.0 * jnp.arange(1, nbh + 1, dtype=F32) / nbh)
    slopes_b2 = slopes_b.reshape(len(B_GROUPS), B_HEADS)
    log_gamma = jnp.log1p(-jnp.exp2(-5.0 - jnp.arange(C_HEADS, dtype=F32)))
    caches = (cache_a, cache_b1, cache_b2, cache_b3)

    xp = x_prompt.reshape(batch * seq, d_model)
    xs = x_sample.reshape(n_s, d_model)
    p_states, s_states = [], []
    for l in range(depth):
        nw = norm_w[l].reshape(1, d_model)
        rw = ret_norm_w[l].reshape(1, C_WIDTH)
        sinks = attn_sinks[l]

        proj = _matmul(_rmsnorm(xp, nw, BF16), w_in, l, BF16)
        ya = _attn_a(proj, slopes_a, sinks, batch, seq)
        yb = _attn_b(proj, slopes_b2, batch, seq, width)
        yc, new_c = _retention(proj, log_gamma, rw, batch, seq)
        mixed = _branch_merge(ya, yb, yc, w_branch_a, w_branch_b, w_branch_c, proj, l, d_model)
        xp = _matmul(mixed, w_out, l, F32, residual=xp)
        p_states.append(_window_rows(proj, batch, seq) + [new_c])

        proj_s = _matmul(_rmsnorm(xs, nw, BF16), w_in, l, F32)
        y_s, new_c_s = _sample_mixers(proj_s, caches, state_c, l, slopes_a, sinks, slopes_b,
                                      log_gamma, rw)
        mixed_s = _branch_merge(y_s[:, :A_WIDTH], y_s[:, A_WIDTH:A_WIDTH + B_WIDTH],
                                y_s[:, A_WIDTH + B_WIDTH:], w_branch_a, w_branch_b, w_branch_c,
                                proj_s, l, d_model)
        xs = _matmul(mixed_s, w_out, l, F32, residual=xs)
        s_states.append(_window_rows(proj_s, n_s, 1) + [new_c_s])

    fw = final_norm_w.reshape(1, d_model)
    y_prompt = _rmsnorm(xp, fw, F32).reshape(batch, seq, d_model)
    y_sample = _rmsnorm(xs, fw, F32).reshape(n_s, 1, d_model)
    stacked = [jnp.stack([s[i] for s in states]) for states in (p_states, s_states) for i in range(5)]
    return (y_prompt, y_sample, *stacked)
```

```python
import functools

import jax
import jax.numpy as jnp
from jax import lax
from jax.experimental import pallas as pl
from jax.experimental.pallas import tpu as pltpu

F32 = jnp.float32
BF16 = jnp.bfloat16

HEAD_DIM = 128
BAND = 128
NORM_EPS = 1e-6
A_HEADS = 8
A_KV_HEADS = 2
A_GROUP = A_HEADS // A_KV_HEADS
B_GROUPS = ((128, 1), (512, 4), (2048, 16))
B_HEADS = 8
C_HEADS = 6
C_QK_DIM = 128
C_V_DIM = 256
C_CHUNK = 128

A_WIDTH = A_HEADS * HEAD_DIM
A_KV_WIDTH = A_KV_HEADS * HEAD_DIM
B_WIDTH = B_HEADS * HEAD_DIM
B_QKV_WIDTH = len(B_GROUPS) * B_WIDTH
C_QK_WIDTH = C_HEADS * C_QK_DIM
C_WIDTH = C_HEADS * C_V_DIM
Y_WIDTH = A_WIDTH + B_WIDTH + C_WIDTH

OFF_QA = 0
OFF_KA = OFF_QA + A_WIDTH
OFF_VA = OFF_KA + A_KV_WIDTH
OFF_ZA = OFF_VA + A_KV_WIDTH
OFF_QB = OFF_ZA + A_WIDTH
OFF_KB = OFF_QB + B_QKV_WIDTH
OFF_VB = OFF_KB + B_QKV_WIDTH
OFF_ZB = OFF_VB + B_QKV_WIDTH
OFF_QC = OFF_ZB + B_WIDTH
OFF_KC = OFF_QC + C_QK_WIDTH
OFF_VC = OFF_KC + C_QK_WIDTH
OFF_ZC = OFF_VC + C_WIDTH
OFF_G = OFF_ZC + C_WIDTH

TN = 512
NEG_BIG = -1e30
SCALE = HEAD_DIM ** -0.5
VMEM_LIMIT = 56 * 1024 * 1024
SAMPLE_CHUNK = 2
B_INFLIGHT = 8
A_INFLIGHT = 4
C_INFLIGHT = 8

_STRIDED_RUNS = tuple((off + B_WIDTH, off + B_QKV_WIDTH) for off in (OFF_QB, OFF_KB, OFF_VB))


def _main_runs(width):
    edges = [0] + [e for run in _STRIDED_RUNS for e in run] + [width]
    return tuple((edges[i], edges[i + 1]) for i in range(0, len(edges), 2))


def _compact(runs, col):
    base = 0
    for lo, hi in runs:
        if lo <= col < hi:
            return base + col - lo
        base += hi - lo
    raise ValueError(col)


def _block_map(runs):
    starts, gaps, pos, prev_hi = [], [], 0, 0
    for lo, hi in runs:
        starts.append(pos // TN)
        gaps.append((lo - prev_hi) // TN)
        pos += hi - lo
        prev_hi = hi

    def col_block(j):
        out = j
        for s, g in zip(starts, gaps):
            if g:
                out = out + jnp.where(j >= s, g, 0)
        return out

    return col_block, pos // TN


def _params(*sem):
    return pltpu.CompilerParams(dimension_semantics=sem, vmem_limit_bytes=VMEM_LIMIT)


def _smem():
    return pl.BlockSpec(memory_space=pltpu.SMEM)


def _silu(z):
    return z * jax.nn.sigmoid(z)


def _rmsnorm_kernel(x_ref, w_ref, o_ref):
    x = x_ref[...].astype(F32)
    ms = jnp.mean(x * x, axis=-1, keepdims=True)
    o_ref[...] = (x * lax.rsqrt(ms + NORM_EPS) * w_ref[...].astype(F32)).astype(o_ref.dtype)


def _rmsnorm(x, w_row, out_dtype):
    m, d = x.shape
    tr = min(m, 256)
    return pl.pallas_call(
        _rmsnorm_kernel,
        grid=(m // tr,),
        in_specs=[pl.BlockSpec((tr, d), lambda i: (i, 0)),
                  pl.BlockSpec((1, d), lambda i: (0, 0))],
        out_specs=pl.BlockSpec((tr, d), lambda i: (i, 0)),
        out_shape=jax.ShapeDtypeStruct((m, d), out_dtype),
        compiler_params=_params("arbitrary"),
        name="rmsnorm",
    )(x, w_row)


def _matmul_kernel(x_ref, xs_ref, w_ref, *rest, has_residual):
    if has_residual:
        r_ref, rs_ref, o_ref, os_ref, wb_ref = rest
    else:
        o_ref, os_ref, wb_ref = rest

    @pl.when(pl.program_id(1) == 0)
    def _():
        wb_ref[...] = w_ref[...].astype(BF16)
        acc_s = jnp.dot(xs_ref[...], wb_ref[...], preferred_element_type=F32)
        if has_residual:
            acc_s = acc_s + rs_ref[...]
        os_ref[...] = acc_s

    acc = jnp.dot(x_ref[...], wb_ref[...], preferred_element_type=F32)
    if has_residual:
        acc = acc + r_ref[...]
    o_ref[...] = acc.astype(o_ref.dtype)


def _matmul(x, xs, w_stack, layer, out_dtype, runs=None, residual=None, name="matmul"):
    m, k = x.shape
    ms = xs.shape[0]
    tm = min(m, 1024)
    col_block, nblocks = _block_map(runs if runs is not None else ((0, w_stack.shape[-1]),))
    in_specs = [pl.BlockSpec((tm, k), lambda j, i: (i, 0)),
                pl.BlockSpec((ms, k), lambda j, i: (0, 0)),
                pl.BlockSpec((None, k, TN), lambda j, i: (layer, 0, col_block(j)))]
    args = [x, xs, w_stack]
    if residual is not None:
        in_specs += [pl.BlockSpec((tm, TN), lambda j, i: (i, j)),
                     pl.BlockSpec((ms, TN), lambda j, i: (0, j))]
        args += list(residual)
    return pl.pallas_call(
        functools.partial(_matmul_kernel, has_residual=residual is not None),
        grid=(nblocks, m // tm),
        in_specs=in_specs,
        out_specs=[pl.BlockSpec((tm, TN), lambda j, i: (i, j)),
                   pl.BlockSpec((ms, TN), lambda j, i: (0, j))],
        out_shape=[jax.ShapeDtypeStruct((m, nblocks * TN), out_dtype),
                   jax.ShapeDtypeStruct((ms, nblocks * TN), F32)],
        scratch_shapes=[pltpu.VMEM((k, TN), BF16)],
        compiler_params=_params("arbitrary", "arbitrary"),
        name=name,
    )(*args)


def _branch_kernel(ya_ref, yb_ref, yc_ref, ys_ref, wa_ref, wb_ref, wc_ref, ga_ref, gb_ref, gc_ref,
                   gsa_ref, gsb_ref, gsc_ref, o_ref, os_ref, wa_s, wb_s, wc_s):
    def gated(y, w_s, g_ref):
        u = jnp.dot(y, w_s[...], preferred_element_type=F32)
        return jax.nn.sigmoid(g_ref[...].astype(F32)) * u

    @pl.when(pl.program_id(1) == 0)
    def _():
        wa_s[...] = wa_ref[...].astype(BF16)
        wb_s[...] = wb_ref[...].astype(BF16)
        wc_s[...] = wc_ref[...].astype(BF16)
        ys = ys_ref[...].astype(BF16)
        os_ref[...] = (gated(ys[:, :A_WIDTH], wa_s, gsa_ref)
                       + gated(ys[:, A_WIDTH:A_WIDTH + B_WIDTH], wb_s, gsb_ref)
                       + gated(ys[:, A_WIDTH + B_WIDTH:], wc_s, gsc_ref)).astype(os_ref.dtype)

    mixed = (gated(ya_ref[...], wa_s, ga_ref) + gated(yb_ref[...], wb_s, gb_ref)
             + gated(yc_ref[...], wc_s, gc_ref))
    o_ref[...] = mixed.astype(o_ref.dtype)


def _branch_merge(ya, yb, yc, y_s, w_a, w_b, w_c, main, main_s, g_col, layer, d_model):
    m = ya.shape[0]
    ms = y_s.shape[0]
    tm = min(m, 512)
    g0 = g_col // TN
    nd = d_model // TN

    def y_spec(width):
        return pl.BlockSpec((tm, width), lambda j, i: (i, 0))

    def w_spec(width):
        return pl.BlockSpec((None, width, TN), lambda j, i: (layer, 0, j))

    def g_spec(rows, idx, whole):
        if whole:
            return pl.BlockSpec((rows, TN), lambda j, i: (0, g0 + idx * nd + j))
        return pl.BlockSpec((rows, TN), lambda j, i: (i, g0 + idx * nd + j))

    return pl.pallas_call(
        _branch_kernel,
        grid=(d_model // TN, m // tm),
        in_specs=[y_spec(A_WIDTH), y_spec(B_WIDTH), y_spec(C_WIDTH),
                  pl.BlockSpec((ms, Y_WIDTH), lambda j, i: (0, 0)),
                  w_spec(A_WIDTH), w_spec(B_WIDTH), w_spec(C_WIDTH),
                  g_spec(tm, 0, False), g_spec(tm, 1, False), g_spec(tm, 2, False),
                  g_spec(ms, 0, True), g_spec(ms, 1, True), g_spec(ms, 2, True)],
        out_specs=[pl.BlockSpec((tm, TN), lambda j, i: (i, j)),
                   pl.BlockSpec((ms, TN), lambda j, i: (0, j))],
        out_shape=[jax.ShapeDtypeStruct((m, d_model), BF16),
                   jax.ShapeDtypeStruct((ms, d_model), BF16)],
        scratch_shapes=[pltpu.VMEM((A_WIDTH, TN), BF16), pltpu.VMEM((B_WIDTH, TN), BF16),
                        pltpu.VMEM((C_WIDTH, TN), BF16)],
        compiler_params=_params("arbitrary", "arbitrary"),
        name="branch_merge",
    )(ya, yb, yc, y_s, w_a, w_b, w_c, main, main, main, main_s, main_s, main_s)


def _band_bias(slope_step, rows_per_tile=1):
    shape = (rows_per_tile * BAND, BAND)
    row = lax.broadcasted_iota(jnp.int32, shape, 0) % BAND
    col = lax.broadcasted_iota(jnp.int32, shape, 1)
    dist = (row - col).astype(F32)
    cur = jnp.where(col <= row, -slope_step * dist, NEG_BIG)
    prev = jnp.where(col >= row, -slope_step * (dist + float(BAND)), NEG_BIG)
    return cur, prev


def _block_rows(n):
    start = n * BAND if isinstance(n, int) else pl.multiple_of(n * BAND, BAND)
    return pl.ds(start, BAND)


def _qk(q, k):
    return lax.dot_general(q, k, (((1,), (1,)), ((), ())), preferred_element_type=F32) * SCALE


def _band_attention(blocks, sink=None):
    logits = []
    for q, k_cur, _, bias_cur, k_prev, _, bias_prev in blocks:
        logit_p = None if k_prev is None else _qk(q, k_prev) + bias_prev
        logits.append((_qk(q, k_cur) + bias_cur, logit_p))
    probs = []
    for logit_c, logit_p in logits:
        m = jnp.max(logit_c, axis=-1, keepdims=True)
        if logit_p is not None:
            m = jnp.maximum(m, jnp.max(logit_p, axis=-1, keepdims=True))
        if sink is not None:
            m = jnp.maximum(m, sink)
        e_c = jnp.exp(logit_c - m)
        den = jnp.sum(e_c, axis=-1, keepdims=True)
        e_p = None
        if logit_p is not None:
            e_p = jnp.exp(logit_p - m)
            den = den + jnp.sum(e_p, axis=-1, keepdims=True)
            e_p = e_p.astype(BF16)
        if sink is not None:
            den = den + jnp.exp(sink - m)
        probs.append((e_c.astype(BF16), e_p, m, den))
    outs = []
    for (_, _, v_cur, _, _, v_prev, _), (e_c, e_p, m, den) in zip(blocks, probs):
        acc = jnp.dot(e_c, v_cur, preferred_element_type=F32)
        if e_p is not None:
            acc = acc + jnp.dot(e_p, v_prev, preferred_element_type=F32)
        outs.append((acc, m, den))
    return outs


def _attn_a_kernel(slopes_ref, sinks_ref, q_ref, k_ref, v_ref, z_ref, y_ref, bias_s, sink_s):
    kv = pl.program_id(1)
    rows = A_GROUP * BAND
    head = lax.broadcasted_iota(jnp.int32, (rows, 1), 0) // BAND
    slope_col = jnp.zeros((rows, 1), F32)
    sink_col = jnp.zeros((rows, 1), F32)
    for g in range(A_GROUP):
        slope_col = jnp.where(head == g, slopes_ref[kv * A_GROUP + g], slope_col)
        sink_col = jnp.where(head == g, sinks_ref[kv * A_GROUP + g], sink_col)
    bias_s[0], bias_s[1] = _band_bias(slope_col, A_GROUP)
    sink_s[...] = sink_col
    nb = q_ref.shape[0] // BAND

    def blocks(first, first_has_prev):
        rows_ = [_block_rows(first + j) for j in range(A_INFLIGHT)]
        kv_ = [(k_ref[rw, :], v_ref[rw, :]) for rw in rows_]
        if first_has_prev:
            before = _block_rows(first - 1)
            kv_before = (k_ref[before, :], v_ref[before, :])
        work = []
        for j, rw in enumerate(rows_):
            q = jnp.concatenate([q_ref[rw, g * HEAD_DIM:(g + 1) * HEAD_DIM] for g in range(A_GROUP)], axis=0)
            if j == 0 and not first_has_prev:
                work.append((q, *kv_[j], bias_s[0], None, None, None))
            else:
                work.append((q, *kv_[j], bias_s[0], *(kv_[j - 1] if j else kv_before), bias_s[1]))
        zs = [[z_ref[rw, g * HEAD_DIM:(g + 1) * HEAD_DIM].astype(F32) for g in range(A_GROUP)] for rw in rows_]
        outs = []
        for (acc, _, den), z in zip(_band_attention(work, sink=sink_s[...]), zs):
            o = acc / den
            outs.append([(o[g * BAND:(g + 1) * BAND] * _silu(z[g])).astype(y_ref.dtype) for g in range(A_GROUP)])
        for rw, ys in zip(rows_, outs):
            for g in range(A_GROUP):
                y_ref[rw, g * HEAD_DIM:(g + 1) * HEAD_DIM] = ys[g]

    blocks(0, False)

    def body(t, carry):
        blocks(t * A_INFLIGHT, True)
        return carry

    lax.fori_loop(1, nb // A_INFLIGHT, body, 0)


def _attn_a(main, slopes, sinks, main_runs, batch, seq):
    gw = A_GROUP * HEAD_DIM

    def spec(off, width):
        c = _compact(main_runs, off) // width
        return pl.BlockSpec((seq, width), lambda b, kv: (b, c + kv))

    return pl.pallas_call(
        _attn_a_kernel,
        grid=(batch, A_KV_HEADS),
        in_specs=[_smem(), _smem(), spec(OFF_QA, gw), spec(OFF_KA, HEAD_DIM), spec(OFF_VA, HEAD_DIM),
                  spec(OFF_ZA, gw)],
        out_specs=pl.BlockSpec((seq, gw), lambda b, kv: (b, kv)),
        out_shape=jax.ShapeDtypeStruct((batch * seq, A_WIDTH), BF16),
        scratch_shapes=[pltpu.VMEM((2, A_GROUP * BAND, BAND), F32), pltpu.VMEM((A_GROUP * BAND, 1), F32)],
        compiler_params=_params("arbitrary", "arbitrary"),
        name="attn_a",
    )(slopes, sinks, main, main, main, main)


def _merge(o_a, lse_a, o_b, lse_b):
    lse = jnp.maximum(lse_a, lse_b) + jnp.log(1.0 + jnp.exp(-jnp.abs(lse_a - lse_b)))
    return o_a * jnp.exp(lse_a - lse) + o_b * jnp.exp(lse_b - lse), lse


def _attn_b_kernel(slopes_ref, q1, k1, v1, q2, k2, v2, q3, k3, v3, z_ref, y_ref, o_s, l_s, bias_s):
    h = pl.program_id(1)
    seq = y_ref.shape[0]
    for g, (_, dil) in enumerate(B_GROUPS):
        bias_s[2 * g], bias_s[2 * g + 1] = _band_bias(slopes_ref[g * B_HEADS + h] * float(dil))

    def finish(acc, m, den):
        lse = jnp.broadcast_to(m + jnp.log(den), (BAND, HEAD_DIM))
        return acc / den, lse

    dil3 = B_GROUPS[2][1]
    assert seq // dil3 == BAND

    def g3_body(t, carry):
        rows = [pl.ds(t * B_INFLIGHT + j, BAND, stride=dil3) for j in range(B_INFLIGHT)]
        qkv = [tuple(a[rw, :].astype(BF16) for a in (q3, k3, v3)) for rw in rows]
        work = [(q, k, v, bias_s[4], None, None, None) for q, k, v in qkv]
        outs = [finish(*part) for part in _band_attention(work)]
        for rw, (o, lse) in zip(rows, outs):
            o_s[rw, :] = o
            l_s[rw, :] = lse
        return carry

    lax.fori_loop(0, dil3 // B_INFLIGHT, g3_body, 0)

    dil2 = B_GROUPS[1][1]
    nb2 = seq // dil2 // BAND

    per_body = max(1, B_INFLIGHT // nb2)

    def g2_body(t, carry):
        rows = [pl.ds(t * per_body + i + n * BAND * dil2, BAND, stride=dil2)
                for i in range(per_body) for n in range(nb2)]
        qkv = [tuple(a[rw, :].astype(BF16) for a in (q2, k2, v2)) for rw in rows]
        old = [(o_s[rw, :], l_s[rw, :]) for rw in rows]
        work = [(q, k, v, bias_s[2], None, None, None) if j % nb2 == 0 else
                (q, k, v, bias_s[2], qkv[j - 1][1], qkv[j - 1][2], bias_s[3]) for j, (q, k, v) in enumerate(qkv)]
        outs = [_merge(*old[j], *finish(*part)) for j, part in enumerate(_band_attention(work))]
        for rw, (o, lse) in zip(rows, outs):
            o_s[rw, :] = o
            l_s[rw, :] = lse
        return carry

    lax.fori_loop(0, dil2 // per_body, g2_body, 0)

    def g1_blocks(first, first_has_prev):
        rows = [_block_rows(first + j) for j in range(B_INFLIGHT)]
        kv = [(k1[rw, :], v1[rw, :]) for rw in rows]
        if first_has_prev:
            before = _block_rows(first - 1)
            kv_before = (k1[before, :], v1[before, :])
        qs = [q1[rw, :] for rw in rows]
        old = [(o_s[rw, :], l_s[rw, :]) for rw in rows]
        zs = [z_ref[rw, :].astype(F32) for rw in rows]
        work = [(qs[j], *kv[j], bias_s[0], None, None, None) if j == 0 and not first_has_prev else
                (qs[j], *kv[j], bias_s[0], *(kv[j - 1] if j else kv_before), bias_s[1]) for j in range(B_INFLIGHT)]
        outs = []
        for j, part in enumerate(_band_attention(work)):
            o, _ = _merge(*old[j], *finish(*part))
            outs.append((o * _silu(zs[j])).astype(y_ref.dtype))
        for rw, y in zip(rows, outs):
            y_ref[rw, :] = y

    g1_blocks(0, False)

    def g1_body(t, carry):
        g1_blocks(t * B_INFLIGHT, True)
        return carry

    lax.fori_loop(1, seq // BAND // B_INFLIGHT, g1_body, 0)


def _attn_b(main, strided, slopes_b, main_runs, batch, seq):
    def spec(runs, off):
        c = _compact(runs, off) // HEAD_DIM
        return pl.BlockSpec((seq, HEAD_DIM), lambda b, h: (b, c + h))

    in_specs = [_smem()]
    args = [slopes_b]
    for g in range(len(B_GROUPS)):
        runs, arr = (main_runs, main) if g == 0 else (_STRIDED_RUNS, strided)
        in_specs += [spec(runs, off + g * B_WIDTH) for off in (OFF_QB, OFF_KB, OFF_VB)]
        args += [arr] * 3
    in_specs.append(spec(main_runs, OFF_ZB))
    args.append(main)
    return pl.pallas_call(
        _attn_b_kernel,
        grid=(batch, B_HEADS),
        in_specs=in_specs,
        out_specs=pl.BlockSpec((seq, HEAD_DIM), lambda b, h: (b, h)),
        out_shape=jax.ShapeDtypeStruct((batch * seq, B_WIDTH), BF16),
        scratch_shapes=[pltpu.VMEM((seq, HEAD_DIM), F32), pltpu.VMEM((seq, HEAD_DIM), F32),
                        pltpu.VMEM((2 * len(B_GROUPS), BAND, BAND), F32)],
        compiler_params=_params("arbitrary", "arbitrary"),
        name="attn_b",
    )(*args)


def _head_norm_gate(o, w_row, z):
    mu = jnp.mean(o, axis=-1, keepdims=True)
    var = jnp.mean(jnp.square(o - mu), axis=-1, keepdims=True)
    y = (o - mu) * lax.rsqrt(var + NORM_EPS) * w_row
    return y * _silu(z)


def _retention_kernel(lg_ref, q_ref, k_ref, v_ref, z_ref, w_ref, y_ref, s_ref, decay_s):
    lg = lg_ref[pl.program_id(1)]
    row = lax.broadcasted_iota(jnp.int32, (C_CHUNK, C_CHUNK), 0)
    col = lax.broadcasted_iota(jnp.int32, (C_CHUNK, C_CHUNK), 1)
    diff = (row - col).astype(F32)
    decay_s[...] = jnp.where(diff >= 0, jnp.exp(lg * jnp.maximum(diff, 0.0)), 0.0)
    pos = lax.broadcasted_iota(jnp.int32, (C_CHUNK, 1), 0).astype(F32)
    q_dec = jnp.exp(lg * (pos + 1.0))
    k_dec = jnp.exp(lg * (float(C_CHUNK) - 1.0 - pos))
    chunk_decay = jnp.exp(lg * float(C_CHUNK))
    w_row = w_ref[...].astype(F32)

    def body(t, state):
        rows = [pl.ds(pl.multiple_of((t * C_INFLIGHT + j) * C_CHUNK, C_CHUNK), C_CHUNK) for j in range(C_INFLIGHT)]
        qs = [q_ref[rw, :].astype(F32) for rw in rows]
        ks = [k_ref[rw, :].astype(F32) * (C_QK_DIM ** -0.5) for rw in rows]
        vs = [v_ref[rw, :] for rw in rows]
        zs = [z_ref[rw, :].astype(F32) for rw in rows]
        attn = [lax.dot_general(q.astype(BF16), k.astype(BF16), (((1,), (1,)), ((), ())),
                                preferred_element_type=F32) for q, k in zip(qs, ks)]
        update = [jnp.dot(jnp.transpose(k * k_dec).astype(BF16), v, preferred_element_type=F32)
                  for k, v in zip(ks, vs)]
        intra = [jnp.dot((a * decay_s[...]).astype(BF16), v, preferred_element_type=F32)
                 for a, v in zip(attn, vs)]
        states = [state]
        for u in update:
            states.append(chunk_decay * states[-1] + u)
        cross = [jnp.dot((q * q_dec).astype(BF16), s0.astype(BF16), preferred_element_type=F32)
                 for q, s0 in zip(qs, states)]
        outs = [_head_norm_gate(i + c, w_row, z).astype(y_ref.dtype) for i, c, z in zip(intra, cross, zs)]
        for rw, y in zip(rows, outs):
            y_ref[rw, :] = y
        return states[-1]

    s_ref[...] = lax.fori_loop(0, q_ref.shape[0] // C_CHUNK // C_INFLIGHT, body,
                               jnp.zeros((C_QK_DIM, C_V_DIM), F32))


def _retention(main, log_gamma, ret_w_row, main_runs, batch, seq):
    def spec(off, width):
        c = _compact(main_runs, off) // width
        return pl.BlockSpec((seq, width), lambda b, h: (b, c + h))

    return pl.pallas_call(
        _retention_kernel,
        grid=(batch, C_HEADS),
        in_specs=[_smem(), spec(OFF_QC, C_QK_DIM), spec(OFF_KC, C_QK_DIM), spec(OFF_VC, C_V_DIM),
                  spec(OFF_ZC, C_V_DIM), pl.BlockSpec((1, C_V_DIM), lambda b, h: (0, h))],
        out_specs=[pl.BlockSpec((seq, C_V_DIM), lambda b, h: (b, h)),
                   pl.BlockSpec((None, None, C_QK_DIM, C_V_DIM), lambda b, h: (b, h, 0, 0))],
        out_shape=[jax.ShapeDtypeStruct((batch * seq, C_WIDTH), BF16),
                   jax.ShapeDtypeStruct((batch, C_HEADS, C_QK_DIM, C_V_DIM), F32)],
        scratch_shapes=[pltpu.VMEM((C_CHUNK, C_CHUNK), F32)],
        compiler_params=_params("arbitrary", "arbitrary"),
        name="retention",
    )(log_gamma, main, main, main, main, ret_w_row)


def _sample_attend(q, k_buf, v_buf, k_new, v_new, slope, sink=None):
    steps_back = float(BAND) - lax.broadcasted_iota(jnp.int32, (1, BAND, 1, 1), 1).astype(F32)
    logit_b = jnp.sum(k_buf * q, axis=-1, keepdims=True) * SCALE - slope * steps_back
    logit_n = jnp.sum(k_new * q, axis=-1, keepdims=True) * SCALE
    m = jnp.maximum(jnp.max(logit_b, axis=1, keepdims=True), logit_n)
    if sink is not None:
        m = jnp.maximum(m, sink)
    e_b = jnp.exp(logit_b - m)
    e_n = jnp.exp(logit_n - m)
    den = jnp.sum(e_b, axis=1, keepdims=True) + e_n
    if sink is not None:
        den = den + jnp.exp(sink - m)
    acc = jnp.sum(e_b * v_buf, axis=1, keepdims=True) + e_n * v_new
    return acc, m, den


def _column(row_vec):
    n = row_vec.shape[-1]
    eye = lax.broadcasted_iota(jnp.int32, (n, n), 0) == lax.broadcasted_iota(jnp.int32, (n, n), 1)
    return jnp.sum(jnp.where(eye, row_vec, 0.0), axis=-1, keepdims=True)


def _sample_kernel(lg_ref, sa_ref, sb_ref, sink_ref, hs_ref, p_ref, ca_ref, cb1_ref, cb2_ref, cb3_ref,
                   st_ref, w_ref, yab_ref, yc_ref, st_out):
    def heads(off, count):
        u = off // HEAD_DIM
        return hs_ref[:, u:u + count, :][:, None]

    for kv in range(A_KV_HEADS):
        hq = slice(kv * A_GROUP, (kv + 1) * A_GROUP)
        acc, _, den = _sample_attend(
            heads(OFF_QA + kv * A_GROUP * HEAD_DIM, A_GROUP),
            ca_ref[:, :, 0, kv:kv + 1, :], ca_ref[:, :, 1, kv:kv + 1, :],
            heads(OFF_KA + kv * HEAD_DIM, 1), heads(OFF_VA + kv * HEAD_DIM, 1),
            sa_ref[hq, 0:1][None, None], sink_ref[hq, 0:1][None, None])
        z = heads(OFF_ZA + kv * A_GROUP * HEAD_DIM, A_GROUP)
        yab_ref[:, hq, :] = ((acc / den) * _silu(z))[:, 0]

    parts = []
    for g, (cache, (_, dil)) in enumerate(zip((cb1_ref, cb2_ref, cb3_ref), B_GROUPS)):
        off = g * B_WIDTH
        parts.append(_sample_attend(
            heads(OFF_QB + off, B_HEADS), cache[:, :, 0], cache[:, :, 1],
            heads(OFF_KB + off, B_HEADS), heads(OFF_VB + off, B_HEADS),
            sb_ref[g * B_HEADS:(g + 1) * B_HEADS, 0:1][None, None] * float(dil)))
    m_all = jnp.maximum(jnp.maximum(parts[0][1], parts[1][1]), parts[2][1])
    num = sum(acc * jnp.exp(m - m_all) for acc, m, _ in parts)
    den = sum(d * jnp.exp(m - m_all) for _, m, d in parts)
    yab_ref[:, A_HEADS:A_HEADS + B_HEADS, :] = ((num / den) * _silu(heads(OFF_ZB, B_HEADS)))[:, 0]

    for h in range(C_HEADS):
        gamma = jnp.exp(lg_ref[h])
        vo = h * C_V_DIM
        for i in range(SAMPLE_CHUNK):
            q = p_ref[i, :, OFF_QC + h * C_QK_DIM:OFF_QC + (h + 1) * C_QK_DIM]
            k = p_ref[i, :, OFF_KC + h * C_QK_DIM:OFF_KC + (h + 1) * C_QK_DIM] * (C_QK_DIM ** -0.5)
            v = p_ref[i, :, OFF_VC + vo:OFF_VC + vo + C_V_DIM]
            z = p_ref[i, :, OFF_ZC + vo:OFF_ZC + vo + C_V_DIM]
            s0 = st_ref[i, h]
            intra = jnp.sum(q * k, axis=-1, keepdims=True) * v
            cross = jnp.sum(_column(q * gamma) * s0, axis=0, keepdims=True)
            st_out[i, h] = gamma * s0 + _column(k) * v
            yc_ref[i, :, vo:vo + C_V_DIM] = _head_norm_gate(intra + cross, w_ref[:, vo:vo + C_V_DIM].astype(F32), z)


def _sample_mixers(flat, caches, state, layer, log_gamma, slopes_a, slopes_b, sinks, ret_w_row):
    n, width = flat.shape
    nc = SAMPLE_CHUNK
    units = width // HEAD_DIM
    hs = flat.reshape(n, units, HEAD_DIM)
    p4 = flat.reshape(n // nc, nc, 1, width)
    depth = caches[0].shape[0]
    views = [caches[0]]
    for c, (win, dil) in zip(caches[1:], B_GROUPS):
        views.append(c.reshape(depth, n, BAND, dil, 2, B_HEADS, HEAD_DIM))

    def lanes(vec):
        return jnp.broadcast_to(vec.astype(F32)[:, None], (vec.shape[0], HEAD_DIM))

    def cache_b_spec():
        return pl.BlockSpec((None, nc, BAND, None, 2, B_HEADS, HEAD_DIM), lambda i: (layer, i, 0, 0, 0, 0, 0))

    yab, yc, st = pl.pallas_call(
        _sample_kernel,
        grid=(n // nc,),
        in_specs=[_smem(),
                  pl.BlockSpec((A_HEADS, HEAD_DIM), lambda i: (0, 0)),
                  pl.BlockSpec((len(B_GROUPS) * B_HEADS, HEAD_DIM), lambda i: (0, 0)),
                  pl.BlockSpec((A_HEADS, HEAD_DIM), lambda i: (0, 0)),
                  pl.BlockSpec((nc, units, HEAD_DIM), lambda i: (i, 0, 0)),
                  pl.BlockSpec((None, nc, 1, width), lambda i: (i, 0, 0, 0)),
                  pl.BlockSpec((None, nc, BAND, 2, A_KV_HEADS, HEAD_DIM), lambda i: (layer, i, 0, 0, 0, 0)),
                  cache_b_spec(), cache_b_spec(), cache_b_spec(),
                  pl.BlockSpec((None, nc, C_HEADS, C_QK_DIM, C_V_DIM), lambda i: (layer, i, 0, 0, 0)),
                  pl.BlockSpec((1, C_WIDTH), lambda i: (0, 0))],
        out_specs=[pl.BlockSpec((nc, A_HEADS + B_HEADS, HEAD_DIM), lambda i: (i, 0, 0)),
                   pl.BlockSpec((None, nc, 1, C_WIDTH), lambda i: (i, 0, 0, 0)),
                   pl.BlockSpec((nc, C_HEADS, C_QK_DIM, C_V_DIM), lambda i: (i, 0, 0, 0))],
        out_shape=[jax.ShapeDtypeStruct((n, A_HEADS + B_HEADS, HEAD_DIM), F32),
                   jax.ShapeDtypeStruct((n // nc, nc, 1, C_WIDTH), F32),
                   jax.ShapeDtypeStruct((n, C_HEADS, C_QK_DIM, C_V_DIM), F32)],
        compiler_params=_params("arbitrary"),
        name="sample_mixers",
    )(log_gamma, lanes(slopes_a), lanes(slopes_b), lanes(sinks), hs, p4, *views, state, ret_w_row)
    y = jnp.concatenate([yab.reshape(n, A_WIDTH + B_WIDTH), yc.reshape(n, C_WIDTH)], axis=-1)
    return y, st


def _kv_rows(k_arr, k_col, v_arr, v_col, batch, heads, rows):
    w = heads * HEAD_DIM
    t = k_arr.shape[0] // batch
    k = k_arr.reshape(batch, t, -1)[:, t - rows:, k_col:k_col + w]
    v = v_arr.reshape(batch, t, -1)[:, t - rows:, v_col:v_col + w]
    return jnp.stack([k, v], axis=2).reshape(batch, rows, 2, heads, HEAD_DIM).astype(F32)


def _window_rows(main, strided, main_runs, batch):
    t = main.shape[0] // batch
    out = [_kv_rows(main, _compact(main_runs, OFF_KA), main, _compact(main_runs, OFF_VA), batch,
                    A_KV_HEADS, min(BAND, t))]
    for g, (win, _) in enumerate(B_GROUPS):
        runs, arr = (main_runs, main) if g == 0 else (_STRIDED_RUNS, strided)
        out.append(_kv_rows(arr, _compact(runs, OFF_KB + g * B_WIDTH), arr, _compact(runs, OFF_VB + g * B_WIDTH),
                            batch, B_HEADS, min(win, t)))
    return out


def _reference_order(main, strided, main_runs, upto):
    pieces = []
    for (lo, hi), src in sorted([(r, main) for r in main_runs] + [(r, strided) for r in _STRIDED_RUNS]):
        if lo >= upto:
            break
        runs = main_runs if src is main else _STRIDED_RUNS
        c = _compact(runs, lo)
        pieces.append(src[:, c:c + min(hi, upto) - lo])
    return jnp.concatenate(pieces, axis=-1)


def kernel(x_prompt, x_sample, cache_a, cache_b1, cache_b2, cache_b3, state_c, norm_w, w_in,
           w_branch_a, w_branch_b, w_branch_c, w_out, attn_sinks, ret_norm_w, final_norm_w):
    batch, seq, d_model = x_prompt.shape
    n_s, t_s, _ = x_sample.shape
    depth = w_in.shape[0]
    width = w_in.shape[-1]
    assert t_s == 1 and seq == B_GROUPS[-1][1] * BAND and n_s % SAMPLE_CHUNK == 0
    assert width == OFF_G + 3 * d_model and d_model % TN == 0
    assert cache_a.shape[2] == BAND
    assert all(c.shape[2] == win for c, (win, _) in zip((cache_b1, cache_b2, cache_b3), B_GROUPS))

    slopes_a = jnp.exp2(-8.0 * jnp.arange(1, A_HEADS + 1, dtype=F32) / A_HEADS)
    nbh = len(B_GROUPS) * B_HEADS
    slopes_b = jnp.exp2(-8.0 * jnp.arange(1, nbh + 1, dtype=F32) / nbh)
    log_gamma = jnp.log1p(-jnp.exp2(-5.0 - jnp.arange(C_HEADS, dtype=F32)))
    caches = (cache_a, cache_b1, cache_b2, cache_b3)
    main_runs = _main_runs(width)
    g_col = _compact(main_runs, OFF_G)

    xp = x_prompt.reshape(batch * seq, d_model)
    xs = x_sample.reshape(n_s, d_model)
    p_states, s_states = [], []
    for l in range(depth):
        nw = norm_w[l].reshape(1, d_model)
        rw = ret_norm_w[l].reshape(1, C_WIDTH)
        sinks = attn_sinks[l]

        hp = _rmsnorm(xp, nw, BF16)
        hsm = _rmsnorm(xs, nw, BF16)
        main, main_s = _matmul(hp, hsm, w_in, l, BF16, runs=main_runs, name="proj_main")
        strided, strided_s = _matmul(hp, hsm, w_in, l, F32, runs=_STRIDED_RUNS, name="proj_strided")

        ya = _attn_a(main, slopes_a, sinks, main_runs, batch, seq)
        yb = _attn_b(main, strided, slopes_b, main_runs, batch, seq)
        yc, new_c = _retention(main, log_gamma, rw, main_runs, batch, seq)
        y_s, new_c_s = _sample_mixers(_reference_order(main_s, strided_s, main_runs, OFF_G), caches, state_c, l,
                                      log_gamma, slopes_a, slopes_b, sinks, rw)

        mixed, mixed_s = _branch_merge(ya, yb, yc, y_s, w_branch_a, w_branch_b, w_branch_c,
                                       main, main_s, g_col, l, d_model)
        xp, xs = _matmul(mixed, mixed_s, w_out, l, F32, residual=(xp, xs), name="proj_out")
        p_states.append(_window_rows(main, strided, main_runs, batch) + [new_c])
        s_states.append(_window_rows(main_s, strided_s, main_runs, n_s) + [new_c_s])

    fw = final_norm_w.reshape(1, d_model)
    y_prompt = _rmsnorm(xp, fw, F32).reshape(batch, seq, d_model)
    y_sample = _rmsnorm(xs, fw, F32).reshape(n_s, 1, d_model)
    stacked = [jnp.stack([s[i] for s in states]) for states in (p_states, s_states) for i in range(5)]
    return (y_prompt, y_sample, *stacked)
```

```python
import functools

import jax
import jax.numpy as jnp
from jax import lax
from jax.experimental import pallas as pl
from jax.experimental.pallas import tpu as pltpu

F32 = jnp.float32
BF16 = jnp.bfloat16

HEAD_DIM = 128
BAND = 128
NORM_EPS = 1e-6
A_HEADS = 8
A_KV_HEADS = 2
A_GROUP = A_HEADS // A_KV_HEADS
B_GROUPS = ((128, 1), (512, 4), (2048, 16))
B_HEADS = 8
C_HEADS = 6
C_QK_DIM = 128
C_V_DIM = 256
C_CHUNK = 128

A_WIDTH = A_HEADS * HEAD_DIM
A_KV_WIDTH = A_KV_HEADS * HEAD_DIM
B_WIDTH = B_HEADS * HEAD_DIM
B_QKV_WIDTH = len(B_GROUPS) * B_WIDTH
C_QK_WIDTH = C_HEADS * C_QK_DIM
C_WIDTH = C_HEADS * C_V_DIM
Y_WIDTH = A_WIDTH + B_WIDTH + C_WIDTH

OFF_QA = 0
OFF_KA = OFF_QA + A_WIDTH
OFF_VA = OFF_KA + A_KV_WIDTH
OFF_ZA = OFF_VA + A_KV_WIDTH
OFF_QB = OFF_ZA + A_WIDTH
OFF_KB = OFF_QB + B_QKV_WIDTH
OFF_VB = OFF_KB + B_QKV_WIDTH
OFF_ZB = OFF_VB + B_QKV_WIDTH
OFF_QC = OFF_ZB + B_WIDTH
OFF_KC = OFF_QC + C_QK_WIDTH
OFF_VC = OFF_KC + C_QK_WIDTH
OFF_ZC = OFF_VC + C_WIDTH
OFF_G = OFF_ZC + C_WIDTH

TN = 512
PART = 256
PARTS = 4
WIDE = PARTS * PART
NEG_BIG = -1e30
SCALE = HEAD_DIM ** -0.5
VMEM_LIMIT = 56 * 1024 * 1024
SAMPLE_CHUNK = 2
B_INFLIGHT = 8
A_INFLIGHT = 4
C_INFLIGHT = 8

_STRIDED_RUNS = tuple((off + B_WIDTH, off + B_QKV_WIDTH) for off in (OFF_QB, OFF_KB, OFF_VB))


def _main_runs(width):
    edges = [0] + [e for run in _STRIDED_RUNS for e in run] + [width]
    return tuple((edges[i], edges[i + 1]) for i in range(0, len(edges), 2))


def _compact(runs, col):
    base = 0
    for lo, hi in runs:
        if lo <= col < hi:
            return base + col - lo
        base += hi - lo
    raise ValueError(col)


def _block_map(runs):
    starts, gaps, pos, prev_hi = [], [], 0, 0
    for lo, hi in runs:
        starts.append(pos // PART)
        gaps.append((lo - prev_hi) // PART)
        pos += hi - lo
        prev_hi = hi

    def col_block(j):
        out = j
        for s, g in zip(starts, gaps):
            if g:
                out = out + jnp.where(j >= s, g, 0)
        return out

    return col_block, pos // PART


def _params(*sem):
    return pltpu.CompilerParams(dimension_semantics=sem, vmem_limit_bytes=VMEM_LIMIT)


def _smem():
    return pl.BlockSpec(memory_space=pltpu.SMEM)


def _silu(z):
    return z * jax.nn.sigmoid(z)


def _rmsnorm_kernel(x_ref, w_ref, o_ref):
    x = x_ref[...].astype(F32)
    ms = jnp.mean(x * x, axis=-1, keepdims=True)
    o_ref[...] = (x * lax.rsqrt(ms + NORM_EPS) * w_ref[...].astype(F32)).astype(o_ref.dtype)


def _rmsnorm(x, w_row, out_dtype):
    m, d = x.shape
    tr = min(m, 256)
    return pl.pallas_call(
        _rmsnorm_kernel,
        grid=(m // tr,),
        in_specs=[pl.BlockSpec((tr, d), lambda i: (i, 0)),
                  pl.BlockSpec((1, d), lambda i: (0, 0))],
        out_specs=pl.BlockSpec((tr, d), lambda i: (i, 0)),
        out_shape=jax.ShapeDtypeStruct((m, d), out_dtype),
        compiler_params=_params("arbitrary"),
        name="rmsnorm",
    )(x, w_row)


def _matmul_kernel(x_ref, xs_ref, w_ref, *rest, has_residual, row_steps):
    if has_residual:
        r_ref, rs_ref, o_ref, os_ref, wb_ref = rest
    else:
        o_ref, os_ref, wb_ref = rest
    j = pl.program_id(0)
    i = pl.program_id(1)
    steps_per_part = row_steps // PARTS
    kq = w_ref.shape[0] // steps_per_part

    def stage():
        rows = pl.ds(pl.multiple_of((i % steps_per_part) * kq, kq), kq)
        wb_ref[j % 2, i // steps_per_part, rows, :] = w_ref[rows, :].astype(BF16)

    def multiply(lhs, res_ref, out_ref):
        for part in range(PARTS):
            cols = slice(part * PART, (part + 1) * PART)
            acc = jnp.dot(lhs, wb_ref[(j + 1) % 2, part], preferred_element_type=F32)
            if has_residual:
                acc = acc + res_ref[:, cols]
            out_ref[:, cols] = acc.astype(out_ref.dtype)

    @pl.when(j == 0)
    def _():
        stage()

    @pl.when(j > 0)
    def _():
        @pl.when(i == 0)
        def _():
            multiply(xs_ref[...], rs_ref if has_residual else None, os_ref)

        multiply(x_ref[...], r_ref if has_residual else None, o_ref)
        stage()


def _matmul(x, xs, w_stack, layer, out_dtype, runs=None, residual=None, name="matmul"):
    m, k = x.shape
    ms = xs.shape[0]
    tm = min(m // PARTS, 1024 if jnp.dtype(out_dtype).itemsize == 2 else 512)
    row_steps = m // tm
    col_block, nparts = _block_map(runs if runs is not None else ((0, w_stack.shape[-1]),))
    nblocks = nparts // PARTS
    steps_per_part = row_steps // PARTS
    assert nparts % PARTS == 0 and row_steps % PARTS == 0 and k % steps_per_part == 0

    def row_map(j, i):
        return jnp.where(j == 0, 0, i)

    def out_map(j, i):
        return (row_map(j, i), jnp.maximum(j - 1, 0))

    def w_map(j, i):
        return (layer, 0, col_block(PARTS * jnp.minimum(j, nblocks - 1) + i // steps_per_part))

    in_specs = [pl.BlockSpec((tm, k), lambda j, i: (row_map(j, i), 0)),
                pl.BlockSpec((ms, k), lambda j, i: (0, 0)),
                pl.BlockSpec((None, k, PART), w_map)]
    args = [x, xs, w_stack]
    if residual is not None:
        in_specs += [pl.BlockSpec((tm, WIDE), out_map),
                     pl.BlockSpec((ms, WIDE), lambda j, i: (0, jnp.maximum(j - 1, 0)))]
        args += list(residual)
    return pl.pallas_call(
        functools.partial(_matmul_kernel, has_residual=residual is not None, row_steps=row_steps),
        grid=(nblocks + 1, row_steps),
        in_specs=in_specs,
        out_specs=[pl.BlockSpec((tm, WIDE), out_map),
                   pl.BlockSpec((ms, WIDE), lambda j, i: (0, jnp.maximum(j - 1, 0)))],
        out_shape=[jax.ShapeDtypeStruct((m, nblocks * WIDE), out_dtype),
                   jax.ShapeDtypeStruct((ms, nblocks * WIDE), F32)],
        scratch_shapes=[pltpu.VMEM((2, PARTS, k, PART), BF16)],
        compiler_params=_params("arbitrary", "arbitrary"),
        name=name,
    )(*args)


def _branch_kernel(ya_ref, yb_ref, yc_ref, ys_ref, wa_ref, wb_ref, wc_ref, ga_ref, gb_ref, gc_ref,
                   gsa_ref, gsb_ref, gsc_ref, o_ref, os_ref, wa_s, wb_s, wc_s, *, row_steps):
    j = pl.program_id(0)
    i = pl.program_id(1)
    steps_per_part = row_steps // PARTS
    staged = ((wa_ref, wa_s), (wb_ref, wb_s), (wc_ref, wc_s))

    def stage():
        for w_ref, w_s in staged:
            kq = w_ref.shape[0] // steps_per_part
            rows = pl.ds(pl.multiple_of((i % steps_per_part) * kq, kq), kq)
            w_s[j % 2, i // steps_per_part, rows, :] = w_ref[rows, :].astype(BF16)

    def merge(ys, g_refs, out_ref):
        for part in range(PARTS):
            cols = slice(part * PART, (part + 1) * PART)
            mixed = None
            for y, (_, w_s), g_ref in zip(ys, staged, g_refs):
                u = jnp.dot(y, w_s[(j + 1) % 2, part], preferred_element_type=F32)
                term = jax.nn.sigmoid(g_ref[:, cols].astype(F32)) * u
                mixed = term if mixed is None else mixed + term
            out_ref[:, cols] = mixed.astype(out_ref.dtype)

    @pl.when(j == 0)
    def _():
        stage()

    @pl.when(j > 0)
    def _():
        @pl.when(i == 0)
        def _():
            ys = ys_ref[...].astype(BF16)
            merge((ys[:, :A_WIDTH], ys[:, A_WIDTH:A_WIDTH + B_WIDTH], ys[:, A_WIDTH + B_WIDTH:]),
                  (gsa_ref, gsb_ref, gsc_ref), os_ref)

        merge((ya_ref[...], yb_ref[...], yc_ref[...]), (ga_ref, gb_ref, gc_ref), o_ref)
        stage()


def _branch_merge(ya, yb, yc, y_s, w_a, w_b, w_c, main, main_s, g_col, layer, d_model):
    m = ya.shape[0]
    ms = y_s.shape[0]
    tm = min(m // PARTS, 512)
    row_steps = m // tm
    steps_per_part = row_steps // PARTS
    nblocks = d_model // WIDE
    assert row_steps % PARTS == 0 and g_col % WIDE == 0 and d_model % WIDE == 0
    assert all(w % steps_per_part == 0 for w in (A_WIDTH, B_WIDTH, C_WIDTH))

    def row_map(j, i):
        return jnp.where(j == 0, 0, i)

    def col_map(j):
        return jnp.maximum(j - 1, 0)

    def y_spec(width):
        return pl.BlockSpec((tm, width), lambda j, i: (row_map(j, i), 0))

    def w_spec(width):
        return pl.BlockSpec((None, width, PART),
                            lambda j, i: (layer, 0, PARTS * jnp.minimum(j, nblocks - 1) + i // steps_per_part))

    def g_spec(rows, idx, whole):
        first = (g_col + idx * d_model) // WIDE
        if whole:
            return pl.BlockSpec((rows, WIDE), lambda j, i: (0, first + col_map(j)))
        return pl.BlockSpec((rows, WIDE), lambda j, i: (row_map(j, i), first + col_map(j)))

    return pl.pallas_call(
        functools.partial(_branch_kernel, row_steps=row_steps),
        grid=(nblocks + 1, row_steps),
        in_specs=[y_spec(A_WIDTH), y_spec(B_WIDTH), y_spec(C_WIDTH),
                  pl.BlockSpec((ms, Y_WIDTH), lambda j, i: (0, 0)),
                  w_spec(A_WIDTH), w_spec(B_WIDTH), w_spec(C_WIDTH),
                  g_spec(tm, 0, False), g_spec(tm, 1, False), g_spec(tm, 2, False),
                  g_spec(ms, 0, True), g_spec(ms, 1, True), g_spec(ms, 2, True)],
        out_specs=[pl.BlockSpec((tm, WIDE), lambda j, i: (row_map(j, i), col_map(j))),
                   pl.BlockSpec((ms, WIDE), lambda j, i: (0, col_map(j)))],
        out_shape=[jax.ShapeDtypeStruct((m, d_model), BF16),
                   jax.ShapeDtypeStruct((ms, d_model), BF16)],
        scratch_shapes=[pltpu.VMEM((2, PARTS, w, PART), BF16) for w in (A_WIDTH, B_WIDTH, C_WIDTH)],
        compiler_params=_params("arbitrary", "arbitrary"),
        name="branch_merge",
    )(ya, yb, yc, y_s, w_a, w_b, w_c, main, main, main, main_s, main_s, main_s)


def _band_bias(slope_step, rows_per_tile=1):
    shape = (rows_per_tile * BAND, BAND)
    row = lax.broadcasted_iota(jnp.int32, shape, 0) % BAND
    col = lax.broadcasted_iota(jnp.int32, shape, 1)
    dist = (row - col).astype(F32)
    cur = jnp.where(col <= row, -slope_step * dist, NEG_BIG)
    prev = jnp.where(col >= row, -slope_step * (dist + float(BAND)), NEG_BIG)
    return cur, prev


def _block_rows(n):
    start = n * BAND if isinstance(n, int) else pl.multiple_of(n * BAND, BAND)
    return pl.ds(start, BAND)


def _qk(q, k):
    return lax.dot_general(q, k, (((1,), (1,)), ((), ())), preferred_element_type=F32) * SCALE


def _band_attention(blocks, sink=None):
    logits = []
    for q, k_cur, _, bias_cur, k_prev, _, bias_prev in blocks:
        logit_p = None if k_prev is None else _qk(q, k_prev) + bias_prev
        logits.append((_qk(q, k_cur) + bias_cur, logit_p))
    probs = []
    for logit_c, logit_p in logits:
        m = jnp.max(logit_c, axis=-1, keepdims=True)
        if logit_p is not None:
            m = jnp.maximum(m, jnp.max(logit_p, axis=-1, keepdims=True))
        if sink is not None:
            m = jnp.maximum(m, sink)
        e_c = jnp.exp(logit_c - m)
        den = jnp.sum(e_c, axis=-1, keepdims=True)
        e_p = None
        if logit_p is not None:
            e_p = jnp.exp(logit_p - m)
            den = den + jnp.sum(e_p, axis=-1, keepdims=True)
            e_p = e_p.astype(BF16)
        if sink is not None:
            den = den + jnp.exp(sink - m)
        probs.append((e_c.astype(BF16), e_p, m, den))
    outs = []
    for (_, _, v_cur, _, _, v_prev, _), (e_c, e_p, m, den) in zip(blocks, probs):
        acc = jnp.dot(e_c, v_cur, preferred_element_type=F32)
        if e_p is not None:
            acc = acc + jnp.dot(e_p, v_prev, preferred_element_type=F32)
        outs.append((acc, m, den))
    return outs


def _attn_a_kernel(slopes_ref, sinks_ref, q_ref, k_ref, v_ref, z_ref, y_ref, bias_s, sink_s):
    kv = pl.program_id(1)
    rows = A_GROUP * BAND
    head = lax.broadcasted_iota(jnp.int32, (rows, 1), 0) // BAND
    slope_col = jnp.zeros((rows, 1), F32)
    sink_col = jnp.zeros((rows, 1), F32)
    for g in range(A_GROUP):
        slope_col = jnp.where(head == g, slopes_ref[kv * A_GROUP + g], slope_col)
        sink_col = jnp.where(head == g, sinks_ref[kv * A_GROUP + g], sink_col)
    bias_s[0], bias_s[1] = _band_bias(slope_col, A_GROUP)
    sink_s[...] = sink_col
    nb = q_ref.shape[0] // BAND

    def blocks(first, first_has_prev):
        rows_ = [_block_rows(first + j) for j in range(A_INFLIGHT)]
        kv_ = [(k_ref[rw, :], v_ref[rw, :]) for rw in rows_]
        if first_has_prev:
            before = _block_rows(first - 1)
            kv_before = (k_ref[before, :], v_ref[before, :])
        work = []
        for j, rw in enumerate(rows_):
            q = jnp.concatenate([q_ref[rw, g * HEAD_DIM:(g + 1) * HEAD_DIM] for g in range(A_GROUP)], axis=0)
            if j == 0 and not first_has_prev:
                work.append((q, *kv_[j], bias_s[0], None, None, None))
            else:
                work.append((q, *kv_[j], bias_s[0], *(kv_[j - 1] if j else kv_before), bias_s[1]))
        zs = [[z_ref[rw, g * HEAD_DIM:(g + 1) * HEAD_DIM].astype(F32) for g in range(A_GROUP)] for rw in rows_]
        outs = []
        for (acc, _, den), z in zip(_band_attention(work, sink=sink_s[...]), zs):
            o = acc / den
            outs.append([(o[g * BAND:(g + 1) * BAND] * _silu(z[g])).astype(y_ref.dtype) for g in range(A_GROUP)])
        for rw, ys in zip(rows_, outs):
            for g in range(A_GROUP):
                y_ref[rw, g * HEAD_DIM:(g + 1) * HEAD_DIM] = ys[g]

    blocks(0, False)

    def body(t, carry):
        blocks(t * A_INFLIGHT, True)
        return carry

    lax.fori_loop(1, nb // A_INFLIGHT, body, 0)


def _attn_a(main, slopes, sinks, main_runs, batch, seq):
    gw = A_GROUP * HEAD_DIM

    def spec(off, width):
        c = _compact(main_runs, off) // width
        return pl.BlockSpec((seq, width), lambda b, kv: (b, c + kv))

    return pl.pallas_call(
        _attn_a_kernel,
        grid=(batch, A_KV_HEADS),
        in_specs=[_smem(), _smem(), spec(OFF_QA, gw), spec(OFF_KA, HEAD_DIM), spec(OFF_VA, HEAD_DIM),
                  spec(OFF_ZA, gw)],
        out_specs=pl.BlockSpec((seq, gw), lambda b, kv: (b, kv)),
        out_shape=jax.ShapeDtypeStruct((batch * seq, A_WIDTH), BF16),
        scratch_shapes=[pltpu.VMEM((2, A_GROUP * BAND, BAND), F32), pltpu.VMEM((A_GROUP * BAND, 1), F32)],
        compiler_params=_params("arbitrary", "arbitrary"),
        name="attn_a",
    )(slopes, sinks, main, main, main, main)


def _merge(o_a, lse_a, o_b, lse_b):
    lse = jnp.maximum(lse_a, lse_b) + jnp.log(1.0 + jnp.exp(-jnp.abs(lse_a - lse_b)))
    return o_a * jnp.exp(lse_a - lse) + o_b * jnp.exp(lse_b - lse), lse


def _attn_b_kernel(slopes_ref, q1, k1, v1, q2, k2, v2, q3, k3, v3, z_ref, y_ref, o_s, l_s, bias_s):
    h = pl.program_id(1)
    seq = y_ref.shape[0]
    for g, (_, dil) in enumerate(B_GROUPS):
        bias_s[2 * g], bias_s[2 * g + 1] = _band_bias(slopes_ref[g * B_HEADS + h] * float(dil))

    def finish(acc, m, den):
        lse = jnp.broadcast_to(m + jnp.log(den), (BAND, HEAD_DIM))
        return acc / den, lse

    dil3 = B_GROUPS[2][1]
    assert seq // dil3 == BAND

    def g3_body(t, carry):
        rows = [pl.ds(t * B_INFLIGHT + j, BAND, stride=dil3) for j in range(B_INFLIGHT)]
        qkv = [tuple(a[rw, :].astype(BF16) for a in (q3, k3, v3)) for rw in rows]
        work = [(q, k, v, bias_s[4], None, None, None) for q, k, v in qkv]
        outs = [finish(*part) for part in _band_attention(work)]
        for rw, (o, lse) in zip(rows, outs):
            o_s[rw, :] = o
            l_s[rw, :] = lse
        return carry

    lax.fori_loop(0, dil3 // B_INFLIGHT, g3_body, 0)

    dil2 = B_GROUPS[1][1]
    nb2 = seq // dil2 // BAND

    per_body = max(1, B_INFLIGHT // nb2)

    def g2_body(t, carry):
        rows = [pl.ds(t * per_body + i + n * BAND * dil2, BAND, stride=dil2)
                for i in range(per_body) for n in range(nb2)]
        qkv = [tuple(a[rw, :].astype(BF16) for a in (q2, k2, v2)) for rw in rows]
        old = [(o_s[rw, :], l_s[rw, :]) for rw in rows]
        work = [(q, k, v, bias_s[2], None, None, None) if j % nb2 == 0 else
                (q, k, v, bias_s[2], qkv[j - 1][1], qkv[j - 1][2], bias_s[3]) for j, (q, k, v) in enumerate(qkv)]
        outs = [_merge(*old[j], *finish(*part)) for j, part in enumerate(_band_attention(work))]
        for rw, (o, lse) in zip(rows, outs):
            o_s[rw, :] = o
            l_s[rw, :] = lse
        return carry

    lax.fori_loop(0, dil2 // per_body, g2_body, 0)

    def g1_blocks(first, first_has_prev):
        rows = [_block_rows(first + j) for j in range(B_INFLIGHT)]
        kv = [(k1[rw, :], v1[rw, :]) for rw in rows]
        if first_has_prev:
            before = _block_rows(first - 1)
            kv_before = (k1[before, :], v1[before, :])
        qs = [q1[rw, :] for rw in rows]
        old = [(o_s[rw, :], l_s[rw, :]) for rw in rows]
        zs = [z_ref[rw, :].astype(F32) for rw in rows]
        work = [(qs[j], *kv[j], bias_s[0], None, None, None) if j == 0 and not first_has_prev else
                (qs[j], *kv[j], bias_s[0], *(kv[j - 1] if j else kv_before), bias_s[1]) for j in range(B_INFLIGHT)]
        outs = []
        for j, part in enumerate(_band_attention(work)):
            o, _ = _merge(*old[j], *finish(*part))
            outs.append((o * _silu(zs[j])).astype(y_ref.dtype))
        for rw, y in zip(rows, outs):
            y_ref[rw, :] = y

    g1_blocks(0, False)

    def g1_body(t, carry):
        g1_blocks(t * B_INFLIGHT, True)
        return carry

    lax.fori_loop(1, seq // BAND // B_INFLIGHT, g1_body, 0)


def _attn_b(main, strided, slopes_b, main_runs, batch, seq):
    def spec(runs, off):
        c = _compact(runs, off) // HEAD_DIM
        return pl.BlockSpec((seq, HEAD_DIM), lambda b, h: (b, c + h))

    in_specs = [_smem()]
    args = [slopes_b]
    for g in range(len(B_GROUPS)):
        runs, arr = (main_runs, main) if g == 0 else (_STRIDED_RUNS, strided)
        in_specs += [spec(runs, off + g * B_WIDTH) for off in (OFF_QB, OFF_KB, OFF_VB)]
        args += [arr] * 3
    in_specs.append(spec(main_runs, OFF_ZB))
    args.append(main)
    return pl.pallas_call(
        _attn_b_kernel,
        grid=(batch, B_HEADS),
        in_specs=in_specs,
        out_specs=pl.BlockSpec((seq, HEAD_DIM), lambda b, h: (b, h)),
        out_shape=jax.ShapeDtypeStruct((batch * seq, B_WIDTH), BF16),
        scratch_shapes=[pltpu.VMEM((seq, HEAD_DIM), F32), pltpu.VMEM((seq, HEAD_DIM), F32),
                        pltpu.VMEM((2 * len(B_GROUPS), BAND, BAND), F32)],
        compiler_params=_params("arbitrary", "arbitrary"),
        name="attn_b",
    )(*args)


def _head_norm_gate(o, w_row, z):
    mu = jnp.mean(o, axis=-1, keepdims=True)
    var = jnp.mean(jnp.square(o - mu), axis=-1, keepdims=True)
    y = (o - mu) * lax.rsqrt(var + NORM_EPS) * w_row
    return y * _silu(z)


def _retention_kernel(lg_ref, q_ref, k_ref, v_ref, z_ref, w_ref, y_ref, s_ref, decay_s):
    lg = lg_ref[pl.program_id(1)]
    row = lax.broadcasted_iota(jnp.int32, (C_CHUNK, C_CHUNK), 0)
    col = lax.broadcasted_iota(jnp.int32, (C_CHUNK, C_CHUNK), 1)
    diff = (row - col).astype(F32)
    decay_s[...] = jnp.where(diff >= 0, jnp.exp(lg * jnp.maximum(diff, 0.0)), 0.0)
    pos = lax.broadcasted_iota(jnp.int32, (C_CHUNK, 1), 0).astype(F32)
    q_dec = jnp.exp(lg * (pos + 1.0))
    k_dec = jnp.exp(lg * (float(C_CHUNK) - 1.0 - pos))
    chunk_decay = jnp.exp(lg * float(C_CHUNK))
    w_row = w_ref[...].astype(F32)

    def body(t, state):
        rows = [pl.ds(pl.multiple_of((t * C_INFLIGHT + j) * C_CHUNK, C_CHUNK), C_CHUNK) for j in range(C_INFLIGHT)]
        qs = [q_ref[rw, :].astype(F32) for rw in rows]
        ks = [k_ref[rw, :].astype(F32) * (C_QK_DIM ** -0.5) for rw in rows]
        vs = [v_ref[rw, :] for rw in rows]
        zs = [z_ref[rw, :].astype(F32) for rw in rows]
        attn = [lax.dot_general(q.astype(BF16), k.astype(BF16), (((1,), (1,)), ((), ())),
                                preferred_element_type=F32) for q, k in zip(qs, ks)]
        update = [jnp.dot(jnp.transpose(k * k_dec).astype(BF16), v, preferred_element_type=F32)
                  for k, v in zip(ks, vs)]
        intra = [jnp.dot((a * decay_s[...]).astype(BF16), v, preferred_element_type=F32)
                 for a, v in zip(attn, vs)]
        states = [state]
        for u in update:
            states.append(chunk_decay * states[-1] + u)
        cross = [jnp.dot((q * q_dec).astype(BF16), s0.astype(BF16), preferred_element_type=F32)
                 for q, s0 in zip(qs, states)]
        outs = [_head_norm_gate(i + c, w_row, z).astype(y_ref.dtype) for i, c, z in zip(intra, cross, zs)]
        for rw, y in zip(rows, outs):
            y_ref[rw, :] = y
        return states[-1]

    s_ref[...] = lax.fori_loop(0, q_ref.shape[0] // C_CHUNK // C_INFLIGHT, body,
                               jnp.zeros((C_QK_DIM, C_V_DIM), F32))


def _retention(main, log_gamma, ret_w_row, main_runs, batch, seq):
    def spec(off, width):
        c = _compact(main_runs, off) // width
        return pl.BlockSpec((seq, width), lambda b, h: (b, c + h))

    return pl.pallas_call(
        _retention_kernel,
        grid=(batch, C_HEADS),
        in_specs=[_smem(), spec(OFF_QC, C_QK_DIM), spec(OFF_KC, C_QK_DIM), spec(OFF_VC, C_V_DIM),
                  spec(OFF_ZC, C_V_DIM), pl.BlockSpec((1, C_V_DIM), lambda b, h: (0, h))],
        out_specs=[pl.BlockSpec((seq, C_V_DIM), lambda b, h: (b, h)),
                   pl.BlockSpec((None, None, C_QK_DIM, C_V_DIM), lambda b, h: (b, h, 0, 0))],
        out_shape=[jax.ShapeDtypeStruct((batch * seq, C_WIDTH), BF16),
                   jax.ShapeDtypeStruct((batch, C_HEADS, C_QK_DIM, C_V_DIM), F32)],
        scratch_shapes=[pltpu.VMEM((C_CHUNK, C_CHUNK), F32)],
        compiler_params=_params("arbitrary", "arbitrary"),
        name="retention",
    )(log_gamma, main, main, main, main, ret_w_row)


def _sample_attend(q, k_buf, v_buf, k_new, v_new, slope, sink=None):
    steps_back = float(BAND) - lax.broadcasted_iota(jnp.int32, (1, BAND, 1, 1), 1).astype(F32)
    logit_b = jnp.sum(k_buf * q, axis=-1, keepdims=True) * SCALE - slope * steps_back
    logit_n = jnp.sum(k_new * q, axis=-1, keepdims=True) * SCALE
    m = jnp.maximum(jnp.max(logit_b, axis=1, keepdims=True), logit_n)
    if sink is not None:
        m = jnp.maximum(m, sink)
    e_b = jnp.exp(logit_b - m)
    e_n = jnp.exp(logit_n - m)
    den = jnp.sum(e_b, axis=1, keepdims=True) + e_n
    if sink is not None:
        den = den + jnp.exp(sink - m)
    acc = jnp.sum(e_b * v_buf, axis=1, keepdims=True) + e_n * v_new
    return acc, m, den


def _column(row_vec):
    n = row_vec.shape[-1]
    eye = lax.broadcasted_iota(jnp.int32, (n, n), 0) == lax.broadcasted_iota(jnp.int32, (n, n), 1)
    return jnp.sum(jnp.where(eye, row_vec, 0.0), axis=-1, keepdims=True)


def _sample_kernel(lg_ref, sa_ref, sb_ref, sink_ref, hs_ref, p_ref, ca_ref, cb1_ref, cb2_ref, cb3_ref,
                   st_ref, w_ref, yab_ref, yc_ref, st_out):
    def heads(off, count):
        u = off // HEAD_DIM
        return hs_ref[:, u:u + count, :][:, None]

    for kv in range(A_KV_HEADS):
        hq = slice(kv * A_GROUP, (kv + 1) * A_GROUP)
        acc, _, den = _sample_attend(
            heads(OFF_QA + kv * A_GROUP * HEAD_DIM, A_GROUP),
            ca_ref[:, :, 0, kv:kv + 1, :], ca_ref[:, :, 1, kv:kv + 1, :],
            heads(OFF_KA + kv * HEAD_DIM, 1), heads(OFF_VA + kv * HEAD_DIM, 1),
            sa_ref[hq, 0:1][None, None], sink_ref[hq, 0:1][None, None])
        z = heads(OFF_ZA + kv * A_GROUP * HEAD_DIM, A_GROUP)
        yab_ref[:, hq, :] = ((acc / den) * _silu(z))[:, 0]

    parts = []
    for g, (cache, (_, dil)) in enumerate(zip((cb1_ref, cb2_ref, cb3_ref), B_GROUPS)):
        off = g * B_WIDTH
        parts.append(_sample_attend(
            heads(OFF_QB + off, B_HEADS), cache[:, :, 0], cache[:, :, 1],
            heads(OFF_KB + off, B_HEADS), heads(OFF_VB + off, B_HEADS),
            sb_ref[g * B_HEADS:(g + 1) * B_HEADS, 0:1][None, None] * float(dil)))
    m_all = jnp.maximum(jnp.maximum(parts[0][1], parts[1][1]), parts[2][1])
    num = sum(acc * jnp.exp(m - m_all) for acc, m, _ in parts)
    den = sum(d * jnp.exp(m - m_all) for _, m, d in parts)
    yab_ref[:, A_HEADS:A_HEADS + B_HEADS, :] = ((num / den) * _silu(heads(OFF_ZB, B_HEADS)))[:, 0]

    for h in range(C_HEADS):
        gamma = jnp.exp(lg_ref[h])
        vo = h * C_V_DIM
        for i in range(SAMPLE_CHUNK):
            q = p_ref[i, :, OFF_QC + h * C_QK_DIM:OFF_QC + (h + 1) * C_QK_DIM]
            k = p_ref[i, :, OFF_KC + h * C_QK_DIM:OFF_KC + (h + 1) * C_QK_DIM] * (C_QK_DIM ** -0.5)
            v = p_ref[i, :, OFF_VC + vo:OFF_VC + vo + C_V_DIM]
            z = p_ref[i, :, OFF_ZC + vo:OFF_ZC + vo + C_V_DIM]
            s0 = st_ref[i, h]
            intra = jnp.sum(q * k, axis=-1, keepdims=True) * v
            cross = jnp.sum(_column(q * gamma) * s0, axis=0, keepdims=True)
            st_out[i, h] = gamma * s0 + _column(k) * v
            yc_ref[i, :, vo:vo + C_V_DIM] = _head_norm_gate(intra + cross, w_ref[:, vo:vo + C_V_DIM].astype(F32), z)


def _sample_mixers(flat, caches, state, layer, log_gamma, slopes_a, slopes_b, sinks, ret_w_row):
    n, width = flat.shape
    nc = SAMPLE_CHUNK
    units = width // HEAD_DIM
    hs = flat.reshape(n, units, HEAD_DIM)
    p4 = flat.reshape(n // nc, nc, 1, width)
    depth = caches[0].shape[0]
    views = [caches[0]]
    for c, (win, dil) in zip(caches[1:], B_GROUPS):
        views.append(c.reshape(depth, n, BAND, dil, 2, B_HEADS, HEAD_DIM))

    def lanes(vec):
        return jnp.broadcast_to(vec.astype(F32)[:, None], (vec.shape[0], HEAD_DIM))

    def cache_b_spec():
        return pl.BlockSpec((None, nc, BAND, None, 2, B_HEADS, HEAD_DIM), lambda i: (layer, i, 0, 0, 0, 0, 0))

    yab, yc, st = pl.pallas_call(
        _sample_kernel,
        grid=(n // nc,),
        in_specs=[_smem(),
                  pl.BlockSpec((A_HEADS, HEAD_DIM), lambda i: (0, 0)),
                  pl.BlockSpec((len(B_GROUPS) * B_HEADS, HEAD_DIM), lambda i: (0, 0)),
                  pl.BlockSpec((A_HEADS, HEAD_DIM), lambda i: (0, 0)),
                  pl.BlockSpec((nc, units, HEAD_DIM), lambda i: (i, 0, 0)),
                  pl.BlockSpec((None, nc, 1, width), lambda i: (i, 0, 0, 0)),
                  pl.BlockSpec((None, nc, BAND, 2, A_KV_HEADS, HEAD_DIM), lambda i: (layer, i, 0, 0, 0, 0)),
                  cache_b_spec(), cache_b_spec(), cache_b_spec(),
                  pl.BlockSpec((None, nc, C_HEADS, C_QK_DIM, C_V_DIM), lambda i: (layer, i, 0, 0, 0)),
                  pl.BlockSpec((1, C_WIDTH), lambda i: (0, 0))],
        out_specs=[pl.BlockSpec((nc, A_HEADS + B_HEADS, HEAD_DIM), lambda i: (i, 0, 0)),
                   pl.BlockSpec((None, nc, 1, C_WIDTH), lambda i: (i, 0, 0, 0)),
                   pl.BlockSpec((nc, C_HEADS, C_QK_DIM, C_V_DIM), lambda i: (i, 0, 0, 0))],
        out_shape=[jax.ShapeDtypeStruct((n, A_HEADS + B_HEADS, HEAD_DIM), F32),
                   jax.ShapeDtypeStruct((n // nc, nc, 1, C_WIDTH), F32),
                   jax.ShapeDtypeStruct((n, C_HEADS, C_QK_DIM, C_V_DIM), F32)],
        compiler_params=_params("arbitrary"),
        name="sample_mixers",
    )(log_gamma, lanes(slopes_a), lanes(slopes_b), lanes(sinks), hs, p4, *views, state, ret_w_row)
    y = jnp.concatenate([yab.reshape(n, A_WIDTH + B_WIDTH), yc.reshape(n, C_WIDTH)], axis=-1)
    return y, st


def _kv_rows(k_arr, k_col, v_arr, v_col, batch, heads, rows):
    w = heads * HEAD_DIM
    t = k_arr.shape[0] // batch
    k = k_arr.reshape(batch, t, -1)[:, t - rows:, k_col:k_col + w]
    v = v_arr.reshape(batch, t, -1)[:, t - rows:, v_col:v_col + w]
    return jnp.stack([k, v], axis=2).reshape(batch, rows, 2, heads, HEAD_DIM).astype(F32)


def _window_rows(main, strided, main_runs, batch):
    t = main.shape[0] // batch
    out = [_kv_rows(main, _compact(main_runs, OFF_KA), main, _compact(main_runs, OFF_VA), batch,
                    A_KV_HEADS, min(BAND, t))]
    for g, (win, _) in enumerate(B_GROUPS):
        runs, arr = (main_runs, main) if g == 0 else (_STRIDED_RUNS, strided)
        out.append(_kv_rows(arr, _compact(runs, OFF_KB + g * B_WIDTH), arr, _compact(runs, OFF_VB + g * B_WIDTH),
                            batch, B_HEADS, min(win, t)))
    return out


def _kv_rows_kernel(*refs, pieces):
    k_refs, v_refs, o_ref = refs[:pieces], refs[pieces:2 * pieces], refs[-1]
    rows, _, heads, dh = o_ref.shape
    for slot, part_refs in enumerate((k_refs, v_refs)):
        flat = jnp.concatenate([r[...].astype(F32) for r in part_refs], axis=-1)
        o_ref[:, slot] = flat.reshape(rows, heads, dh)


def _kv_rows_into(stack, layer, depth, k_arr, k_col, v_arr, v_col, batch, heads, rows):
    w = heads * HEAD_DIM
    pw = min(w, TN)
    pieces = w // pw
    t = k_arr.shape[0] // batch
    tr = min(rows, 256)
    first, per_seq = (t - rows) // tr, t // tr

    def piece(col):
        return pl.BlockSpec((tr, pw), lambda b, i: (b * per_seq + first + i, col // pw))

    in_specs = ([piece(k_col + p * pw) for p in range(pieces)] + [piece(v_col + p * pw) for p in range(pieces)])
    args = [k_arr] * pieces + [v_arr] * pieces
    aliases = {}
    if stack is not None:
        in_specs.append(pl.BlockSpec(memory_space=pl.ANY))
        args.append(stack)
        aliases = {2 * pieces: 0}
    return pl.pallas_call(
        functools.partial(_kv_rows_kernel, pieces=pieces),
        grid=(batch, rows // tr),
        in_specs=in_specs,
        out_specs=pl.BlockSpec((None, None, tr, 2, heads, HEAD_DIM), lambda b, i: (layer, b, i, 0, 0, 0)),
        out_shape=jax.ShapeDtypeStruct((depth, batch, rows, 2, heads, HEAD_DIM), F32),
        input_output_aliases=aliases,
        compiler_params=_params("arbitrary", "arbitrary"),
        name="kv_rows",
    )(*args)


def _window_rows_into(stacks, layer, depth, main, strided, main_runs, batch):
    t = main.shape[0] // batch
    stacks = stacks or [None] * (1 + len(B_GROUPS))
    out = [_kv_rows_into(stacks[0], layer, depth, main, _compact(main_runs, OFF_KA), main,
                         _compact(main_runs, OFF_VA), batch, A_KV_HEADS, min(BAND, t))]
    for g, (win, _) in enumerate(B_GROUPS):
        runs, arr = (main_runs, main) if g == 0 else (_STRIDED_RUNS, strided)
        out.append(_kv_rows_into(stacks[1 + g], layer, depth, arr, _compact(runs, OFF_KB + g * B_WIDTH), arr,
                                 _compact(runs, OFF_VB + g * B_WIDTH), batch, B_HEADS, min(win, t)))
    return out


def _reference_order(main, strided, main_runs, upto):
    pieces = []
    for (lo, hi), src in sorted([(r, main) for r in main_runs] + [(r, strided) for r in _STRIDED_RUNS]):
        if lo >= upto:
            break
        runs = main_runs if src is main else _STRIDED_RUNS
        c = _compact(runs, lo)
        pieces.append(src[:, c:c + min(hi, upto) - lo])
    return jnp.concatenate(pieces, axis=-1)


def kernel(x_prompt, x_sample, cache_a, cache_b1, cache_b2, cache_b3, state_c, norm_w, w_in,
           w_branch_a, w_branch_b, w_branch_c, w_out, attn_sinks, ret_norm_w, final_norm_w):
    batch, seq, d_model = x_prompt.shape
    n_s, t_s, _ = x_sample.shape
    depth = w_in.shape[0]
    width = w_in.shape[-1]
    assert t_s == 1 and seq == B_GROUPS[-1][1] * BAND and n_s % SAMPLE_CHUNK == 0
    assert width == OFF_G + 3 * d_model and d_model % TN == 0
    assert cache_a.shape[2] == BAND
    assert all(c.shape[2] == win for c, (win, _) in zip((cache_b1, cache_b2, cache_b3), B_GROUPS))

    slopes_a = jnp.exp2(-8.0 * jnp.arange(1, A_HEADS + 1, dtype=F32) / A_HEADS)
    nbh = len(B_GROUPS) * B_HEADS
    slopes_b = jnp.exp2(-8.0 * jnp.arange(1, nbh + 1, dtype=F32) / nbh)
    log_gamma = jnp.log1p(-jnp.exp2(-5.0 - jnp.arange(C_HEADS, dtype=F32)))
    caches = (cache_a, cache_b1, cache_b2, cache_b3)
    main_runs = _main_runs(width)
    g_col = _compact(main_runs, OFF_G)

    xp = x_prompt.reshape(batch * seq, d_model)
    xs = x_sample.reshape(n_s, d_model)
    p_window, p_c, s_states = None, [], []
    for l in range(depth):
        nw = norm_w[l].reshape(1, d_model)
        rw = ret_norm_w[l].reshape(1, C_WIDTH)
        sinks = attn_sinks[l]

        hp = _rmsnorm(xp, nw, BF16)
        hsm = _rmsnorm(xs, nw, BF16)
        main, main_s = _matmul(hp, hsm, w_in, l, BF16, runs=main_runs, name="proj_main")
        strided, strided_s = _matmul(hp, hsm, w_in, l, F32, runs=_STRIDED_RUNS, name="proj_strided")

        ya = _attn_a(main, slopes_a, sinks, main_runs, batch, seq)
        yb = _attn_b(main, strided, slopes_b, main_runs, batch, seq)
        yc, new_c = _retention(main, log_gamma, rw, main_runs, batch, seq)
        y_s, new_c_s = _sample_mixers(_reference_order(main_s, strided_s, main_runs, OFF_G), caches, state_c, l,
                                      log_gamma, slopes_a, slopes_b, sinks, rw)

        mixed, mixed_s = _branch_merge(ya, yb, yc, y_s, w_branch_a, w_branch_b, w_branch_c,
                                       main, main_s, g_col, l, d_model)
        xp, xs = _matmul(mixed, mixed_s, w_out, l, F32, residual=(xp, xs), name="proj_out")
        p_window = _window_rows_into(p_window, l, depth, main, strided, main_runs, batch)
        p_c.append(new_c)
        s_states.append(_window_rows(main_s, strided_s, main_runs, n_s) + [new_c_s])

    fw = final_norm_w.reshape(1, d_model)
    y_prompt = _rmsnorm(xp, fw, F32).reshape(batch, seq, d_model)
    y_sample = _rmsnorm(xs, fw, F32).reshape(n_s, 1, d_model)
    sample = [jnp.stack([s[i] for s in s_states]) for i in range(5)]
    return (y_prompt, y_sample, *p_window, jnp.stack(p_c), *sample)
```

```python
import functools

import jax
import jax.numpy as jnp
from jax import lax
from jax.experimental import pallas as pl
from jax.experimental.pallas import tpu as pltpu

F32 = jnp.float32
BF16 = jnp.bfloat16

HEAD_DIM = 128
BAND = 128
NORM_EPS = 1e-6
A_HEADS = 8
A_KV_HEADS = 2
A_GROUP = A_HEADS // A_KV_HEADS
B_GROUPS = ((128, 1), (512, 4), (2048, 16))
B_HEADS = 8
C_HEADS = 6
C_QK_DIM = 128
C_V_DIM = 256
C_CHUNK = 128

A_WIDTH = A_HEADS * HEAD_DIM
A_KV_WIDTH = A_KV_HEADS * HEAD_DIM
B_WIDTH = B_HEADS * HEAD_DIM
B_QKV_WIDTH = len(B_GROUPS) * B_WIDTH
C_QK_WIDTH = C_HEADS * C_QK_DIM
C_WIDTH = C_HEADS * C_V_DIM
Y_WIDTH = A_WIDTH + B_WIDTH + C_WIDTH

OFF_QA = 0
OFF_KA = OFF_QA + A_WIDTH
OFF_VA = OFF_KA + A_KV_WIDTH
OFF_ZA = OFF_VA + A_KV_WIDTH
OFF_QB = OFF_ZA + A_WIDTH
OFF_KB = OFF_QB + B_QKV_WIDTH
OFF_VB = OFF_KB + B_QKV_WIDTH
OFF_ZB = OFF_VB + B_QKV_WIDTH
OFF_QC = OFF_ZB + B_WIDTH
OFF_KC = OFF_QC + C_QK_WIDTH
OFF_VC = OFF_KC + C_QK_WIDTH
OFF_ZC = OFF_VC + C_WIDTH
OFF_G = OFF_ZC + C_WIDTH

TN = 512
PART = 256
PARTS = 4
WIDE = PARTS * PART
NEG_BIG = -1e30
SCALE = HEAD_DIM ** -0.5
assert BAND == HEAD_DIM
VMEM_LIMIT = 56 * 1024 * 1024
SAMPLE_CHUNK = 2
B_INFLIGHT = 8
A_INFLIGHT = 4
C_INFLIGHT = 8

_STRIDED_RUNS = tuple((off + B_WIDTH, off + B_QKV_WIDTH) for off in (OFF_QB, OFF_KB, OFF_VB))


def _main_runs(width):
    edges = [0] + [e for run in _STRIDED_RUNS for e in run] + [width]
    return tuple((edges[i], edges[i + 1]) for i in range(0, len(edges), 2))


def _compact(runs, col):
    base = 0
    for lo, hi in runs:
        if lo <= col < hi:
            return base + col - lo
        base += hi - lo
    raise ValueError(col)


def _block_map(runs):
    starts, gaps, pos, prev_hi = [], [], 0, 0
    for lo, hi in runs:
        starts.append(pos // PART)
        gaps.append((lo - prev_hi) // PART)
        pos += hi - lo
        prev_hi = hi

    def col_block(j):
        out = j
        for s, g in zip(starts, gaps):
            if g:
                out = out + jnp.where(j >= s, g, 0)
        return out

    return col_block, pos // PART


def _params(*sem):
    return pltpu.CompilerParams(dimension_semantics=sem, vmem_limit_bytes=VMEM_LIMIT)


def _smem():
    return pl.BlockSpec(memory_space=pltpu.SMEM)


def _silu(z):
    return z * jax.nn.sigmoid(z)


def _rmsnorm_kernel(x_ref, w_ref, o_ref):
    x = x_ref[...].astype(F32)
    ms = jnp.mean(x * x, axis=-1, keepdims=True)
    o_ref[...] = (x * lax.rsqrt(ms + NORM_EPS) * w_ref[...].astype(F32)).astype(o_ref.dtype)


def _rmsnorm(x, w_row, out_dtype):
    m, d = x.shape
    tr = min(m, 256)
    return pl.pallas_call(
        _rmsnorm_kernel,
        grid=(m // tr,),
        in_specs=[pl.BlockSpec((tr, d), lambda i: (i, 0)),
                  pl.BlockSpec((1, d), lambda i: (0, 0))],
        out_specs=pl.BlockSpec((tr, d), lambda i: (i, 0)),
        out_shape=jax.ShapeDtypeStruct((m, d), out_dtype),
        compiler_params=_params("arbitrary"),
        name="rmsnorm",
    )(x, w_row)


def _matmul_kernel(x_ref, xs_ref, w_ref, *rest, has_residual, row_steps):
    if has_residual:
        r_ref, rs_ref, o_ref, os_ref, wb_ref = rest
    else:
        o_ref, os_ref, wb_ref = rest
    j = pl.program_id(0)
    i = pl.program_id(1)
    steps_per_part = row_steps // PARTS
    kq = w_ref.shape[0] // steps_per_part

    def stage():
        rows = pl.ds(pl.multiple_of((i % steps_per_part) * kq, kq), kq)
        wb_ref[j % 2, i // steps_per_part, rows, :] = w_ref[rows, :].astype(BF16)

    def multiply(with_sample):
        tm = x_ref.shape[0]
        lhs = x_ref[...]
        targets = [(slice(0, tm), r_ref if has_residual else None, o_ref)]
        if with_sample:
            lhs = jnp.concatenate([lhs, xs_ref[...]], axis=0)
            targets.append((slice(tm, None), rs_ref if has_residual else None, os_ref))
        for part in range(PARTS):
            cols = slice(part * PART, (part + 1) * PART)
            acc = jnp.dot(lhs, wb_ref[(j + 1) % 2, part], preferred_element_type=F32)
            for rows, res_ref, out_ref in targets:
                out = acc[rows]
                if has_residual:
                    out = out + res_ref[:, cols]
                out_ref[:, cols] = out.astype(out_ref.dtype)

    @pl.when(j == 0)
    def _():
        stage()

    @pl.when(jnp.logical_and(j > 0, i == 0))
    def _():
        multiply(True)
        stage()

    @pl.when(jnp.logical_and(j > 0, i > 0))
    def _():
        multiply(False)
        stage()


def _matmul(x, xs, w_stack, layer, out_dtype, runs=None, residual=None, name="matmul"):
    m, k = x.shape
    ms = xs.shape[0]
    tm = min(m // PARTS, 1024 if jnp.dtype(out_dtype).itemsize == 2 else 512)
    row_steps = m // tm
    col_block, nparts = _block_map(runs if runs is not None else ((0, w_stack.shape[-1]),))
    nblocks = nparts // PARTS
    steps_per_part = row_steps // PARTS
    assert nparts % PARTS == 0 and row_steps % PARTS == 0 and k % steps_per_part == 0

    def row_map(j, i):
        return jnp.where(j == 0, 0, i)

    def out_map(j, i):
        return (row_map(j, i), jnp.maximum(j - 1, 0))

    def w_map(j, i):
        return (layer, 0, col_block(PARTS * jnp.minimum(j, nblocks - 1) + i // steps_per_part))

    in_specs = [pl.BlockSpec((tm, k), lambda j, i: (row_map(j, i), 0)),
                pl.BlockSpec((ms, k), lambda j, i: (0, 0)),
                pl.BlockSpec((None, k, PART), w_map)]
    args = [x, xs, w_stack]
    if residual is not None:
        in_specs += [pl.BlockSpec((tm, WIDE), out_map),
                     pl.BlockSpec((ms, WIDE), lambda j, i: (0, jnp.maximum(j - 1, 0)))]
        args += list(residual)
    return pl.pallas_call(
        functools.partial(_matmul_kernel, has_residual=residual is not None, row_steps=row_steps),
        grid=(nblocks + 1, row_steps),
        in_specs=in_specs,
        out_specs=[pl.BlockSpec((tm, WIDE), out_map),
                   pl.BlockSpec((ms, WIDE), lambda j, i: (0, jnp.maximum(j - 1, 0)))],
        out_shape=[jax.ShapeDtypeStruct((m, nblocks * WIDE), out_dtype),
                   jax.ShapeDtypeStruct((ms, nblocks * WIDE), F32)],
        scratch_shapes=[pltpu.VMEM((2, PARTS, k, PART), BF16)],
        compiler_params=_params("arbitrary", "arbitrary"),
        name=name,
    )(*args)


def _branch_kernel(ya_ref, yb_ref, yc_ref, ys_ref, wa_ref, wb_ref, wc_ref, ga_ref, gb_ref, gc_ref,
                   gsa_ref, gsb_ref, gsc_ref, o_ref, os_ref, wa_s, wb_s, wc_s, *, row_steps):
    j = pl.program_id(0)
    i = pl.program_id(1)
    steps_per_part = row_steps // PARTS
    staged = ((wa_ref, wa_s), (wb_ref, wb_s), (wc_ref, wc_s))

    def stage():
        for w_ref, w_s in staged:
            kq = w_ref.shape[0] // steps_per_part
            rows = pl.ds(pl.multiple_of((i % steps_per_part) * kq, kq), kq)
            w_s[j % 2, i // steps_per_part, rows, :] = w_ref[rows, :].astype(BF16)

    def merge(with_sample):
        tm = ya_ref.shape[0]
        lhs = [ya_ref[...], yb_ref[...], yc_ref[...]]
        targets = [(slice(0, tm), (ga_ref, gb_ref, gc_ref), o_ref)]
        if with_sample:
            ys = ys_ref[...].astype(BF16)
            bounds = (0, A_WIDTH, A_WIDTH + B_WIDTH, Y_WIDTH)
            lhs = [jnp.concatenate([y, ys[:, lo:hi]], axis=0) for y, lo, hi in zip(lhs, bounds, bounds[1:])]
            targets.append((slice(tm, None), (gsa_ref, gsb_ref, gsc_ref), os_ref))
        for part in range(PARTS):
            cols = slice(part * PART, (part + 1) * PART)
            products = [jnp.dot(y, w_s[(j + 1) % 2, part], preferred_element_type=F32)
                        for y, (_, w_s) in zip(lhs, staged)]
            for rows, g_refs, out_ref in targets:
                mixed = None
                for u, g_ref in zip(products, g_refs):
                    term = jax.nn.sigmoid(g_ref[:, cols].astype(F32)) * u[rows]
                    mixed = term if mixed is None else mixed + term
                out_ref[:, cols] = mixed.astype(out_ref.dtype)

    @pl.when(j == 0)
    def _():
        stage()

    @pl.when(jnp.logical_and(j > 0, i == 0))
    def _():
        merge(True)
        stage()

    @pl.when(jnp.logical_and(j > 0, i > 0))
    def _():
        merge(False)
        stage()


def _branch_merge(ya, yb, yc, y_s, w_a, w_b, w_c, main, main_s, g_col, layer, d_model):
    m = ya.shape[0]
    ms = y_s.shape[0]
    tm = min(m // PARTS, 512)
    row_steps = m // tm
    steps_per_part = row_steps // PARTS
    nblocks = d_model // WIDE
    assert row_steps % PARTS == 0 and g_col % WIDE == 0 and d_model % WIDE == 0
    assert all(w % steps_per_part == 0 for w in (A_WIDTH, B_WIDTH, C_WIDTH))

    def row_map(j, i):
        return jnp.where(j == 0, 0, i)

    def col_map(j):
        return jnp.maximum(j - 1, 0)

    def y_spec(width):
        return pl.BlockSpec((tm, width), lambda j, i: (row_map(j, i), 0))

    def w_spec(width):
        return pl.BlockSpec((None, width, PART),
                            lambda j, i: (layer, 0, PARTS * jnp.minimum(j, nblocks - 1) + i // steps_per_part))

    def g_spec(rows, idx, whole):
        first = (g_col + idx * d_model) // WIDE
        if whole:
            return pl.BlockSpec((rows, WIDE), lambda j, i: (0, first + col_map(j)))
        return pl.BlockSpec((rows, WIDE), lambda j, i: (row_map(j, i), first + col_map(j)))

    return pl.pallas_call(
        functools.partial(_branch_kernel, row_steps=row_steps),
        grid=(nblocks + 1, row_steps),
        in_specs=[y_spec(A_WIDTH), y_spec(B_WIDTH), y_spec(C_WIDTH),
                  pl.BlockSpec((ms, Y_WIDTH), lambda j, i: (0, 0)),
                  w_spec(A_WIDTH), w_spec(B_WIDTH), w_spec(C_WIDTH),
                  g_spec(tm, 0, False), g_spec(tm, 1, False), g_spec(tm, 2, False),
                  g_spec(ms, 0, True), g_spec(ms, 1, True), g_spec(ms, 2, True)],
        out_specs=[pl.BlockSpec((tm, WIDE), lambda j, i: (row_map(j, i), col_map(j))),
                   pl.BlockSpec((ms, WIDE), lambda j, i: (0, col_map(j)))],
        out_shape=[jax.ShapeDtypeStruct((m, d_model), BF16),
                   jax.ShapeDtypeStruct((ms, d_model), BF16)],
        scratch_shapes=[pltpu.VMEM((2, PARTS, w, PART), BF16) for w in (A_WIDTH, B_WIDTH, C_WIDTH)],
        compiler_params=_params("arbitrary", "arbitrary"),
        name="branch_merge",
    )(ya, yb, yc, y_s, w_a, w_b, w_c, main, main, main, main_s, main_s, main_s)


def _band_bias(slope_step, rows_per_tile=1):
    shape = (rows_per_tile * BAND, BAND)
    row = lax.broadcasted_iota(jnp.int32, shape, 0) % BAND
    col = lax.broadcasted_iota(jnp.int32, shape, 1)
    dist = (row - col).astype(F32)
    cur = jnp.where(col <= row, -slope_step * dist, NEG_BIG)
    prev = jnp.where(col >= row, -slope_step * (dist + float(BAND)), NEG_BIG)
    return cur, prev


def _block_rows(n):
    start = n * BAND if isinstance(n, int) else pl.multiple_of(n * BAND, BAND)
    return pl.ds(start, BAND)


def _qk(q, k):
    return lax.dot_general(q, k, (((1,), (1,)), ((), ())), preferred_element_type=F32) * SCALE


def _band_attention(blocks, sink=None, carry=None):
    logits = []
    for q, k_cur, _, bias_cur, k_prev, _, bias_prev in blocks:
        logit_p = None if k_prev is None else _qk(q, k_prev) + bias_prev
        logits.append((_qk(q, k_cur) + bias_cur, logit_p))
    probs = []
    for n, (logit_c, logit_p) in enumerate(logits):
        m = jnp.max(logit_c, axis=-1, keepdims=True)
        if logit_p is not None:
            m = jnp.maximum(m, jnp.max(logit_p, axis=-1, keepdims=True))
        if sink is not None:
            m = jnp.maximum(m, sink)
        m_tile = jnp.broadcast_to(m, logit_c.shape)
        alpha = None
        if carry is not None:
            m_tile = jnp.maximum(m_tile, carry[n][1])
            alpha = jnp.exp(carry[n][1] - m_tile)
        e_c = jnp.exp(logit_c - m_tile)
        den = jnp.sum(e_c, axis=-1, keepdims=True)
        e_p = None
        if logit_p is not None:
            e_p = jnp.exp(logit_p - m_tile)
            den = den + jnp.sum(e_p, axis=-1, keepdims=True)
            e_p = e_p.astype(BF16)
        if sink is not None:
            den = den + jnp.exp(sink - m)
        den_tile = jnp.broadcast_to(den, logit_c.shape)
        if carry is not None:
            den_tile = den_tile + carry[n][2] * alpha
        probs.append((e_c.astype(BF16), e_p, m_tile, den_tile, alpha))
    outs = []
    for n, ((_, _, v_cur, _, _, v_prev, _), (e_c, e_p, m, den, alpha)) in enumerate(zip(blocks, probs)):
        acc = jnp.dot(e_c, v_cur, preferred_element_type=F32)
        if e_p is not None:
            acc = acc + jnp.dot(e_p, v_prev, preferred_element_type=F32)
        if carry is not None:
            acc = acc + carry[n][0] * alpha
        outs.append((acc, m, den))
    return outs


def _attn_a_kernel(slopes_ref, sinks_ref, q_ref, k_ref, v_ref, z_ref, y_ref, bias_s, sink_s):
    kv = pl.program_id(1)
    rows = A_GROUP * BAND
    head = lax.broadcasted_iota(jnp.int32, (rows, 1), 0) // BAND
    slope_col = jnp.zeros((rows, 1), F32)
    sink_col = jnp.zeros((rows, 1), F32)
    for g in range(A_GROUP):
        slope_col = jnp.where(head == g, slopes_ref[kv * A_GROUP + g], slope_col)
        sink_col = jnp.where(head == g, sinks_ref[kv * A_GROUP + g], sink_col)
    bias_s[0], bias_s[1] = _band_bias(slope_col, A_GROUP)
    sink_s[...] = sink_col
    nb = q_ref.shape[0] // BAND

    def blocks(first, first_has_prev):
        rows_ = [_block_rows(first + j) for j in range(A_INFLIGHT)]
        kv_ = [(k_ref[rw, :], v_ref[rw, :]) for rw in rows_]
        if first_has_prev:
            before = _block_rows(first - 1)
            kv_before = (k_ref[before, :], v_ref[before, :])
        work = []
        for j, rw in enumerate(rows_):
            q = jnp.concatenate([q_ref[rw, g * HEAD_DIM:(g + 1) * HEAD_DIM] for g in range(A_GROUP)], axis=0)
            if j == 0 and not first_has_prev:
                work.append((q, *kv_[j], bias_s[0], None, None, None))
            else:
                work.append((q, *kv_[j], bias_s[0], *(kv_[j - 1] if j else kv_before), bias_s[1]))
        zs = [[z_ref[rw, g * HEAD_DIM:(g + 1) * HEAD_DIM].astype(F32) for g in range(A_GROUP)] for rw in rows_]
        outs = []
        for (acc, _, den), z in zip(_band_attention(work, sink=sink_s[...]), zs):
            o = acc / den
            outs.append([(o[g * BAND:(g + 1) * BAND] * _silu(z[g])).astype(y_ref.dtype) for g in range(A_GROUP)])
        for rw, ys in zip(rows_, outs):
            for g in range(A_GROUP):
                y_ref[rw, g * HEAD_DIM:(g + 1) * HEAD_DIM] = ys[g]

    blocks(0, False)

    def body(t, carry):
        blocks(t * A_INFLIGHT, True)
        return carry

    lax.fori_loop(1, nb // A_INFLIGHT, body, 0)


def _attn_a(main, slopes, sinks, main_runs, batch, seq):
    gw = A_GROUP * HEAD_DIM

    def spec(off, width):
        c = _compact(main_runs, off) // width
        return pl.BlockSpec((seq, width), lambda b, kv: (b, c + kv))

    return pl.pallas_call(
        _attn_a_kernel,
        grid=(batch, A_KV_HEADS),
        in_specs=[_smem(), _smem(), spec(OFF_QA, gw), spec(OFF_KA, HEAD_DIM), spec(OFF_VA, HEAD_DIM),
                  spec(OFF_ZA, gw)],
        out_specs=pl.BlockSpec((seq, gw), lambda b, kv: (b, kv)),
        out_shape=jax.ShapeDtypeStruct((batch * seq, A_WIDTH), BF16),
        scratch_shapes=[pltpu.VMEM((2, A_GROUP * BAND, BAND), F32), pltpu.VMEM((A_GROUP * BAND, 1), F32)],
        compiler_params=_params("arbitrary", "arbitrary"),
        name="attn_a",
    )(slopes, sinks, main, main, main, main)


def _attn_b_kernel(slopes_ref, q1, k1, v1, q2, k2, v2, q3, k3, v3, z_ref, y_ref, acc_s, m_s, l_s, bias_s):
    h = pl.program_id(1)
    seq = y_ref.shape[0]
    for g, (_, dil) in enumerate(B_GROUPS):
        bias_s[2 * g], bias_s[2 * g + 1] = _band_bias(slopes_ref[g * B_HEADS + h] * float(dil))

    def load_state(rows):
        return [(acc_s[rw, :], m_s[rw, :], l_s[rw, :]) for rw in rows]

    def store_state(rows, outs):
        for rw, (acc, m, den) in zip(rows, outs):
            acc_s[rw, :] = acc
            m_s[rw, :] = m
            l_s[rw, :] = den


    dil3 = B_GROUPS[2][1]
    assert seq // dil3 == BAND

    def g3_body(t, carry):
        rows = [pl.ds(t * B_INFLIGHT + j, BAND, stride=dil3) for j in range(B_INFLIGHT)]
        qkv = [tuple(a[rw, :].astype(BF16) for a in (q3, k3, v3)) for rw in rows]
        store_state(rows, _band_attention([(q, k, v, bias_s[4], None, None, None) for q, k, v in qkv]))
        return carry

    lax.fori_loop(0, dil3 // B_INFLIGHT, g3_body, 0)

    dil2 = B_GROUPS[1][1]
    nb2 = seq // dil2 // BAND
    per_body = max(1, B_INFLIGHT // nb2)

    def g2_body(t, carry):
        rows = [pl.ds(t * per_body + i + n * BAND * dil2, BAND, stride=dil2)
                for i in range(per_body) for n in range(nb2)]
        qkv = [tuple(a[rw, :].astype(BF16) for a in (q2, k2, v2)) for rw in rows]
        work = [(q, k, v, bias_s[2], None, None, None) if j % nb2 == 0 else
                (q, k, v, bias_s[2], qkv[j - 1][1], qkv[j - 1][2], bias_s[3]) for j, (q, k, v) in enumerate(qkv)]
        store_state(rows, _band_attention(work, carry=load_state(rows)))
        return carry

    lax.fori_loop(0, dil2 // per_body, g2_body, 0)

    def g1_blocks(first, first_has_prev):
        rows = [_block_rows(first + j) for j in range(B_INFLIGHT)]
        kv = [(k1[rw, :], v1[rw, :]) for rw in rows]
        if first_has_prev:
            before = _block_rows(first - 1)
            kv_before = (k1[before, :], v1[before, :])
        qs = [q1[rw, :] for rw in rows]
        zs = [z_ref[rw, :].astype(F32) for rw in rows]
        work = [(qs[j], *kv[j], bias_s[0], None, None, None) if j == 0 and not first_has_prev else
                (qs[j], *kv[j], bias_s[0], *(kv[j - 1] if j else kv_before), bias_s[1]) for j in range(B_INFLIGHT)]
        outs = [((acc / den) * _silu(z)).astype(y_ref.dtype)
                for (acc, _, den), z in zip(_band_attention(work, carry=load_state(rows)), zs)]
        for rw, y in zip(rows, outs):
            y_ref[rw, :] = y

    g1_blocks(0, False)

    def g1_body(t, carry):
        g1_blocks(t * B_INFLIGHT, True)
        return carry

    lax.fori_loop(1, seq // BAND // B_INFLIGHT, g1_body, 0)


def _attn_b(main, strided, slopes_b, main_runs, batch, seq):
    def spec(runs, off):
        c = _compact(runs, off) // HEAD_DIM
        return pl.BlockSpec((seq, HEAD_DIM), lambda b, h: (b, c + h))

    in_specs = [_smem()]
    args = [slopes_b]
    for g in range(len(B_GROUPS)):
        runs, arr = (main_runs, main) if g == 0 else (_STRIDED_RUNS, strided)
        in_specs += [spec(runs, off + g * B_WIDTH) for off in (OFF_QB, OFF_KB, OFF_VB)]
        args += [arr] * 3
    in_specs.append(spec(main_runs, OFF_ZB))
    args.append(main)
    return pl.pallas_call(
        _attn_b_kernel,
        grid=(batch, B_HEADS),
        in_specs=in_specs,
        out_specs=pl.BlockSpec((seq, HEAD_DIM), lambda b, h: (b, h)),
        out_shape=jax.ShapeDtypeStruct((batch * seq, B_WIDTH), BF16),
        scratch_shapes=[pltpu.VMEM((seq, HEAD_DIM), F32)] * 3 + [pltpu.VMEM((2 * len(B_GROUPS), BAND, BAND), F32)],
        compiler_params=_params("arbitrary", "arbitrary"),
        name="attn_b",
    )(*args)


def _head_norm_gate(o, w_row, z):
    mu = jnp.mean(o, axis=-1, keepdims=True)
    var = jnp.mean(jnp.square(o - mu), axis=-1, keepdims=True)
    y = (o - mu) * lax.rsqrt(var + NORM_EPS) * w_row
    return y * _silu(z)


def _retention_kernel(lg_ref, q_ref, k_ref, v_ref, z_ref, w_ref, y_ref, s_ref, decay_s):
    lg = lg_ref[pl.program_id(1)]
    row = lax.broadcasted_iota(jnp.int32, (C_CHUNK, C_CHUNK), 0)
    col = lax.broadcasted_iota(jnp.int32, (C_CHUNK, C_CHUNK), 1)
    diff = (row - col).astype(F32)
    decay_s[...] = jnp.where(diff >= 0, jnp.exp(lg * jnp.maximum(diff, 0.0)), 0.0)
    pos = lax.broadcasted_iota(jnp.int32, (C_CHUNK, 1), 0).astype(F32)
    q_dec = jnp.exp(lg * (pos + 1.0))
    k_dec = jnp.exp(lg * (float(C_CHUNK) - 1.0 - pos))
    chunk_decay = jnp.exp(lg * float(C_CHUNK))
    w_row = w_ref[...].astype(F32)

    def body(t, state):
        rows = [pl.ds(pl.multiple_of((t * C_INFLIGHT + j) * C_CHUNK, C_CHUNK), C_CHUNK) for j in range(C_INFLIGHT)]
        qs = [q_ref[rw, :].astype(F32) for rw in rows]
        ks = [k_ref[rw, :].astype(F32) * (C_QK_DIM ** -0.5) for rw in rows]
        vs = [v_ref[rw, :] for rw in rows]
        zs = [z_ref[rw, :].astype(F32) for rw in rows]
        attn = [lax.dot_general(q.astype(BF16), k.astype(BF16), (((1,), (1,)), ((), ())),
                                preferred_element_type=F32) for q, k in zip(qs, ks)]
        update = [jnp.dot(jnp.transpose(k * k_dec).astype(BF16), v, preferred_element_type=F32)
                  for k, v in zip(ks, vs)]
        intra = [jnp.dot((a * decay_s[...]).astype(BF16), v, preferred_element_type=F32)
                 for a, v in zip(attn, vs)]
        states = [state]
        for u in update:
            states.append(chunk_decay * states[-1] + u)
        cross = [jnp.dot((q * q_dec).astype(BF16), s0.astype(BF16), preferred_element_type=F32)
                 for q, s0 in zip(qs, states)]
        outs = [_head_norm_gate(i + c, w_row, z).astype(y_ref.dtype) for i, c, z in zip(intra, cross, zs)]
        for rw, y in zip(rows, outs):
            y_ref[rw, :] = y
        return states[-1]

    s_ref[...] = lax.fori_loop(0, q_ref.shape[0] // C_CHUNK // C_INFLIGHT, body,
                               jnp.zeros((C_QK_DIM, C_V_DIM), F32))


def _retention(main, log_gamma, ret_w_row, main_runs, batch, seq):
    def spec(off, width):
        c = _compact(main_runs, off) // width
        return pl.BlockSpec((seq, width), lambda b, h: (b, c + h))

    return pl.pallas_call(
        _retention_kernel,
        grid=(batch, C_HEADS),
        in_specs=[_smem(), spec(OFF_QC, C_QK_DIM), spec(OFF_KC, C_QK_DIM), spec(OFF_VC, C_V_DIM),
                  spec(OFF_ZC, C_V_DIM), pl.BlockSpec((1, C_V_DIM), lambda b, h: (0, h))],
        out_specs=[pl.BlockSpec((seq, C_V_DIM), lambda b, h: (b, h)),
                   pl.BlockSpec((None, None, C_QK_DIM, C_V_DIM), lambda b, h: (b, h, 0, 0))],
        out_shape=[jax.ShapeDtypeStruct((batch * seq, C_WIDTH), BF16),
                   jax.ShapeDtypeStruct((batch, C_HEADS, C_QK_DIM, C_V_DIM), F32)],
        scratch_shapes=[pltpu.VMEM((C_CHUNK, C_CHUNK), F32)],
        compiler_params=_params("arbitrary", "arbitrary"),
        name="retention",
    )(log_gamma, main, main, main, main, ret_w_row)


def _sample_attend(q, k_buf, v_buf, k_new, v_new, slope, sink=None):
    steps_back = float(BAND) - lax.broadcasted_iota(jnp.int32, (1, BAND, 1, 1), 1).astype(F32)
    logit_b = jnp.sum(k_buf * q, axis=-1, keepdims=True) * SCALE - slope * steps_back
    logit_n = jnp.sum(k_new * q, axis=-1, keepdims=True) * SCALE
    m = jnp.maximum(jnp.max(logit_b, axis=1, keepdims=True), logit_n)
    if sink is not None:
        m = jnp.maximum(m, sink)
    e_b = jnp.exp(logit_b - m)
    e_n = jnp.exp(logit_n - m)
    den = jnp.sum(e_b, axis=1, keepdims=True) + e_n
    if sink is not None:
        den = den + jnp.exp(sink - m)
    acc = jnp.sum(e_b * v_buf, axis=1, keepdims=True) + e_n * v_new
    return acc, m, den


def _column(row_vec):
    n = row_vec.shape[-1]
    eye = lax.broadcasted_iota(jnp.int32, (n, n), 0) == lax.broadcasted_iota(jnp.int32, (n, n), 1)
    return jnp.sum(jnp.where(eye, row_vec, 0.0), axis=-1, keepdims=True)


def _sample_kernel(lg_ref, sa_ref, sb_ref, sink_ref, hs_ref, p_ref, ca_ref, cb1_ref, cb2_ref, cb3_ref,
                   st_ref, w_ref, *rest):
    yab_ref, yc_ref, st_out = rest[-3:]

    def heads(off, count):
        u = off // HEAD_DIM
        return hs_ref[:, u:u + count, :][:, None]

    for kv in range(A_KV_HEADS):
        hq = slice(kv * A_GROUP, (kv + 1) * A_GROUP)
        acc, _, den = _sample_attend(
            heads(OFF_QA + kv * A_GROUP * HEAD_DIM, A_GROUP),
            ca_ref[:, :, 0, kv:kv + 1, :], ca_ref[:, :, 1, kv:kv + 1, :],
            heads(OFF_KA + kv * HEAD_DIM, 1), heads(OFF_VA + kv * HEAD_DIM, 1),
            sa_ref[hq, 0:1][None, None], sink_ref[hq, 0:1][None, None])
        z = heads(OFF_ZA + kv * A_GROUP * HEAD_DIM, A_GROUP)
        yab_ref[:, hq, :] = ((acc / den) * _silu(z))[:, 0]

    parts = []
    for g, (cache, (_, dil)) in enumerate(zip((cb1_ref, cb2_ref, cb3_ref), B_GROUPS)):
        off = g * B_WIDTH
        parts.append(_sample_attend(
            heads(OFF_QB + off, B_HEADS), cache[:, :, 0], cache[:, :, 1],
            heads(OFF_KB + off, B_HEADS), heads(OFF_VB + off, B_HEADS),
            sb_ref[g * B_HEADS:(g + 1) * B_HEADS, 0:1][None, None] * float(dil)))
    m_all = jnp.maximum(jnp.maximum(parts[0][1], parts[1][1]), parts[2][1])
    num = sum(acc * jnp.exp(m - m_all) for acc, m, _ in parts)
    den = sum(d * jnp.exp(m - m_all) for _, m, d in parts)
    yab_ref[:, A_HEADS:A_HEADS + B_HEADS, :] = ((num / den) * _silu(heads(OFF_ZB, B_HEADS)))[:, 0]

    for h in range(C_HEADS):
        gamma = jnp.exp(lg_ref[h])
        vo = h * C_V_DIM
        for i in range(SAMPLE_CHUNK):
            q = p_ref[i, :, OFF_QC + h * C_QK_DIM:OFF_QC + (h + 1) * C_QK_DIM]
            k = p_ref[i, :, OFF_KC + h * C_QK_DIM:OFF_KC + (h + 1) * C_QK_DIM] * (C_QK_DIM ** -0.5)
            v = p_ref[i, :, OFF_VC + vo:OFF_VC + vo + C_V_DIM]
            z = p_ref[i, :, OFF_ZC + vo:OFF_ZC + vo + C_V_DIM]
            s0 = st_ref[i, h]
            intra = jnp.sum(q * k, axis=-1, keepdims=True) * v
            cross = jnp.sum(_column(q * gamma) * s0, axis=0, keepdims=True)
            st_out[i, h] = gamma * s0 + _column(k) * v
            yc_ref[i, :, vo:vo + C_V_DIM] = _head_norm_gate(intra + cross, w_ref[:, vo:vo + C_V_DIM].astype(F32), z)


def _sample_mixers(flat, caches, state, state_stack, layer, log_gamma, slopes_a, slopes_b, sinks, ret_w_row):
    n, width = flat.shape
    nc = SAMPLE_CHUNK
    units = width // HEAD_DIM
    hs = flat.reshape(n, units, HEAD_DIM)
    p4 = flat.reshape(n // nc, nc, 1, width)
    depth = caches[0].shape[0]
    views = [caches[0]]
    for c, (win, dil) in zip(caches[1:], B_GROUPS):
        views.append(c.reshape(depth, n, BAND, dil, 2, B_HEADS, HEAD_DIM))

    def lanes(vec):
        return jnp.broadcast_to(vec.astype(F32)[:, None], (vec.shape[0], HEAD_DIM))

    def cache_b_spec():
        return pl.BlockSpec((None, nc, BAND, None, 2, B_HEADS, HEAD_DIM), lambda i: (layer, i, 0, 0, 0, 0, 0))

    args = [log_gamma, lanes(slopes_a), lanes(slopes_b), lanes(sinks), hs, p4, *views, state, ret_w_row]
    in_specs = [_smem(),
                pl.BlockSpec((A_HEADS, HEAD_DIM), lambda i: (0, 0)),
                pl.BlockSpec((len(B_GROUPS) * B_HEADS, HEAD_DIM), lambda i: (0, 0)),
                pl.BlockSpec((A_HEADS, HEAD_DIM), lambda i: (0, 0)),
                pl.BlockSpec((nc, units, HEAD_DIM), lambda i: (i, 0, 0)),
                pl.BlockSpec((None, nc, 1, width), lambda i: (i, 0, 0, 0)),
                pl.BlockSpec((None, nc, BAND, 2, A_KV_HEADS, HEAD_DIM), lambda i: (layer, i, 0, 0, 0, 0)),
                cache_b_spec(), cache_b_spec(), cache_b_spec(),
                pl.BlockSpec((None, nc, C_HEADS, C_QK_DIM, C_V_DIM), lambda i: (layer, i, 0, 0, 0)),
                pl.BlockSpec((1, C_WIDTH), lambda i: (0, 0))]
    aliases = {}
    if state_stack is not None:
        in_specs.append(pl.BlockSpec(memory_space=pl.ANY))
        aliases = {len(args): 2}
        args.append(state_stack)
    yab, yc, st = pl.pallas_call(
        _sample_kernel,
        grid=(n // nc,),
        in_specs=in_specs,
        out_specs=[pl.BlockSpec((nc, A_HEADS + B_HEADS, HEAD_DIM), lambda i: (i, 0, 0)),
                   pl.BlockSpec((None, nc, 1, C_WIDTH), lambda i: (i, 0, 0, 0)),
                   pl.BlockSpec((None, nc, C_HEADS, C_QK_DIM, C_V_DIM), lambda i: (layer, i, 0, 0, 0))],
        out_shape=[jax.ShapeDtypeStruct((n, A_HEADS + B_HEADS, HEAD_DIM), F32),
                   jax.ShapeDtypeStruct((n // nc, nc, 1, C_WIDTH), F32),
                   jax.ShapeDtypeStruct((depth, n, C_HEADS, C_QK_DIM, C_V_DIM), F32)],
        input_output_aliases=aliases,
        compiler_params=_params("arbitrary"),
        name="sample_mixers",
    )(*args)
    y = jnp.concatenate([yab.reshape(n, A_WIDTH + B_WIDTH), yc.reshape(n, C_WIDTH)], axis=-1)
    return y, st


def _kv_rows(k_arr, k_col, v_arr, v_col, batch, heads, rows):
    w = heads * HEAD_DIM
    t = k_arr.shape[0] // batch
    k = k_arr.reshape(batch, t, -1)[:, t - rows:, k_col:k_col + w]
    v = v_arr.reshape(batch, t, -1)[:, t - rows:, v_col:v_col + w]
    return jnp.stack([k, v], axis=2).reshape(batch, rows, 2, heads, HEAD_DIM).astype(F32)


def _window_rows(main, strided, main_runs, batch):
    t = main.shape[0] // batch
    out = [_kv_rows(main, _compact(main_runs, OFF_KA), main, _compact(main_runs, OFF_VA), batch,
                    A_KV_HEADS, min(BAND, t))]
    for g, (win, _) in enumerate(B_GROUPS):
        runs, arr = (main_runs, main) if g == 0 else (_STRIDED_RUNS, strided)
        out.append(_kv_rows(arr, _compact(runs, OFF_KB + g * B_WIDTH), arr, _compact(runs, OFF_VB + g * B_WIDTH),
                            batch, B_HEADS, min(win, t)))
    return out


def _kv_rows_kernel(*refs, pieces):
    k_refs, v_refs, o_ref = refs[:pieces], refs[pieces:2 * pieces], refs[-1]
    rows, _, heads, dh = o_ref.shape
    for slot, part_refs in enumerate((k_refs, v_refs)):
        flat = jnp.concatenate([r[...].astype(F32) for r in part_refs], axis=-1)
        o_ref[:, slot] = flat.reshape(rows, heads, dh)


def _kv_rows_into(stack, layer, depth, k_arr, k_col, v_arr, v_col, batch, heads, rows):
    w = heads * HEAD_DIM
    pw = min(w, TN)
    pieces = w // pw
    t = k_arr.shape[0] // batch
    tr = min(rows, 512)
    first, per_seq = (t - rows) // tr, t // tr

    def piece(col):
        return pl.BlockSpec((tr, pw), lambda b, i: (b * per_seq + first + i, col // pw))

    in_specs = ([piece(k_col + p * pw) for p in range(pieces)] + [piece(v_col + p * pw) for p in range(pieces)])
    args = [k_arr] * pieces + [v_arr] * pieces
    aliases = {}
    if stack is not None:
        in_specs.append(pl.BlockSpec(memory_space=pl.ANY))
        args.append(stack)
        aliases = {2 * pieces: 0}
    return pl.pallas_call(
        functools.partial(_kv_rows_kernel, pieces=pieces),
        grid=(batch, rows // tr),
        in_specs=in_specs,
        out_specs=pl.BlockSpec((None, None, tr, 2, heads, HEAD_DIM), lambda b, i: (layer, b, i, 0, 0, 0)),
        out_shape=jax.ShapeDtypeStruct((depth, batch, rows, 2, heads, HEAD_DIM), F32),
        input_output_aliases=aliases,
        compiler_params=_params("arbitrary", "arbitrary"),
        name="kv_rows",
    )(*args)


def _window_rows_into(stacks, layer, depth, main, strided, main_runs, batch):
    t = main.shape[0] // batch
    stacks = stacks or [None] * (1 + len(B_GROUPS))
    out = [_kv_rows_into(stacks[0], layer, depth, main, _compact(main_runs, OFF_KA), main,
                         _compact(main_runs, OFF_VA), batch, A_KV_HEADS, min(BAND, t))]
    for g, (win, _) in enumerate(B_GROUPS):
        runs, arr = (main_runs, main) if g == 0 else (_STRIDED_RUNS, strided)
        out.append(_kv_rows_into(stacks[1 + g], layer, depth, arr, _compact(runs, OFF_KB + g * B_WIDTH), arr,
                                 _compact(runs, OFF_VB + g * B_WIDTH), batch, B_HEADS, min(win, t)))
    return out


def _reference_order(main, strided, main_runs, upto):
    pieces = []
    for (lo, hi), src in sorted([(r, main) for r in main_runs] + [(r, strided) for r in _STRIDED_RUNS]):
        if lo >= upto:
            break
        runs = main_runs if src is main else _STRIDED_RUNS
        c = _compact(runs, lo)
        pieces.append(src[:, c:c + min(hi, upto) - lo])
    return jnp.concatenate(pieces, axis=-1)


def kernel(x_prompt, x_sample, cache_a, cache_b1, cache_b2, cache_b3, state_c, norm_w, w_in,
           w_branch_a, w_branch_b, w_branch_c, w_out, attn_sinks, ret_norm_w, final_norm_w):
    batch, seq, d_model = x_prompt.shape
    n_s, t_s, _ = x_sample.shape
    depth = w_in.shape[0]
    width = w_in.shape[-1]
    assert t_s == 1 and seq == B_GROUPS[-1][1] * BAND and n_s % SAMPLE_CHUNK == 0
    assert width == OFF_G + 3 * d_model and d_model % TN == 0
    assert cache_a.shape[2] == BAND
    assert all(c.shape[2] == win for c, (win, _) in zip((cache_b1, cache_b2, cache_b3), B_GROUPS))

    slopes_a = jnp.exp2(-8.0 * jnp.arange(1, A_HEADS + 1, dtype=F32) / A_HEADS)
    nbh = len(B_GROUPS) * B_HEADS
    slopes_b = jnp.exp2(-8.0 * jnp.arange(1, nbh + 1, dtype=F32) / nbh)
    log_gamma = jnp.log1p(-jnp.exp2(-5.0 - jnp.arange(C_HEADS, dtype=F32)))
    caches = (cache_a, cache_b1, cache_b2, cache_b3)
    main_runs = _main_runs(width)
    g_col = _compact(main_runs, OFF_G)

    xp = x_prompt.reshape(batch * seq, d_model)
    xs = x_sample.reshape(n_s, d_model)
    p_window, p_c, s_states, s_c = None, [], [], None
    for l in range(depth):
        nw = norm_w[l].reshape(1, d_model)
        rw = ret_norm_w[l].reshape(1, C_WIDTH)
        sinks = attn_sinks[l]

        hp = _rmsnorm(xp, nw, BF16)
        hsm = _rmsnorm(xs, nw, BF16)
        main, main_s = _matmul(hp, hsm, w_in, l, BF16, runs=main_runs, name="proj_main")
        strided, strided_s = _matmul(hp, hsm, w_in, l, F32, runs=_STRIDED_RUNS, name="proj_strided")

        ya = _attn_a(main, slopes_a, sinks, main_runs, batch, seq)
        yb = _attn_b(main, strided, slopes_b, main_runs, batch, seq)
        yc, new_c = _retention(main, log_gamma, rw, main_runs, batch, seq)
        y_s, s_c = _sample_mixers(_reference_order(main_s, strided_s, main_runs, OFF_G), caches, state_c, s_c, l,
                                  log_gamma, slopes_a, slopes_b, sinks, rw)

        mixed, mixed_s = _branch_merge(ya, yb, yc, y_s, w_branch_a, w_branch_b, w_branch_c,
                                       main, main_s, g_col, l, d_model)
        xp, xs = _matmul(mixed, mixed_s, w_out, l, F32, residual=(xp, xs), name="proj_out")
        p_window = _window_rows_into(p_window, l, depth, main, strided, main_runs, batch)
        p_c.append(new_c)
        s_states.append(_window_rows(main_s, strided_s, main_runs, n_s))

    fw = final_norm_w.reshape(1, d_model)
    y_prompt = _rmsnorm(xp, fw, F32).reshape(batch, seq, d_model)
    y_sample = _rmsnorm(xs, fw, F32).reshape(n_s, 1, d_model)
    sample = [jnp.stack([s[i] for s in s_states]) for i in range(4)]
    return (y_prompt, y_sample, *p_window, jnp.stack(p_c), *sample, s_c)
```

```python
import functools

import jax
import jax.numpy as jnp
from jax import lax
from jax.experimental import pallas as pl
from jax.experimental.pallas import tpu as pltpu

F32 = jnp.float32
BF16 = jnp.bfloat16

HEAD_DIM = 128
BAND = 128
NORM_EPS = 1e-6
A_HEADS = 8
A_KV_HEADS = 2
A_GROUP = A_HEADS // A_KV_HEADS
B_GROUPS = ((128, 1), (512, 4), (2048, 16))
B_HEADS = 8
C_HEADS = 6
C_QK_DIM = 128
C_V_DIM = 256
C_CHUNK = 128

A_WIDTH = A_HEADS * HEAD_DIM
A_KV_WIDTH = A_KV_HEADS * HEAD_DIM
B_WIDTH = B_HEADS * HEAD_DIM
B_QKV_WIDTH = len(B_GROUPS) * B_WIDTH
C_QK_WIDTH = C_HEADS * C_QK_DIM
C_WIDTH = C_HEADS * C_V_DIM
Y_WIDTH = A_WIDTH + B_WIDTH + C_WIDTH

OFF_QA = 0
OFF_KA = OFF_QA + A_WIDTH
OFF_VA = OFF_KA + A_KV_WIDTH
OFF_ZA = OFF_VA + A_KV_WIDTH
OFF_QB = OFF_ZA + A_WIDTH
OFF_KB = OFF_QB + B_QKV_WIDTH
OFF_VB = OFF_KB + B_QKV_WIDTH
OFF_ZB = OFF_VB + B_QKV_WIDTH
OFF_QC = OFF_ZB + B_WIDTH
OFF_KC = OFF_QC + C_QK_WIDTH
OFF_VC = OFF_KC + C_QK_WIDTH
OFF_ZC = OFF_VC + C_WIDTH
OFF_G = OFF_ZC + C_WIDTH

TN = 512
PART = 256
PARTS = 4
WIDE = PARTS * PART
NEG_BIG = -1e30
SCALE = HEAD_DIM ** -0.5
assert BAND == HEAD_DIM
VMEM_LIMIT = 56 * 1024 * 1024
SAMPLE_CHUNK = 2
B_INFLIGHT = 8
A_INFLIGHT = 4
C_INFLIGHT = 8

_STRIDED_RUNS = tuple((off + B_WIDTH, off + B_QKV_WIDTH) for off in (OFF_QB, OFF_KB, OFF_VB))


def _main_runs(width):
    edges = [0] + [e for run in _STRIDED_RUNS for e in run] + [width]
    return tuple((edges[i], edges[i + 1]) for i in range(0, len(edges), 2))


def _compact(runs, col):
    base = 0
    for lo, hi in runs:
        if lo <= col < hi:
            return base + col - lo
        base += hi - lo
    raise ValueError(col)


def _block_map(runs):
    starts, gaps, pos, prev_hi = [], [], 0, 0
    for lo, hi in runs:
        starts.append(pos // PART)
        gaps.append((lo - prev_hi) // PART)
        pos += hi - lo
        prev_hi = hi

    def col_block(j):
        out = j
        for s, g in zip(starts, gaps):
            if g:
                out = out + jnp.where(j >= s, g, 0)
        return out

    return col_block, pos // PART


def _params(*sem):
    return pltpu.CompilerParams(dimension_semantics=sem, vmem_limit_bytes=VMEM_LIMIT)


def _smem():
    return pl.BlockSpec(memory_space=pltpu.SMEM)


def _silu(z):
    return z * jax.nn.sigmoid(z)


def _rmsnorm_kernel(x_ref, w_ref, o_ref):
    x = x_ref[...].astype(F32)
    ms = jnp.mean(x * x, axis=-1, keepdims=True)
    o_ref[...] = (x * lax.rsqrt(ms + NORM_EPS) * w_ref[...].astype(F32)).astype(o_ref.dtype)


def _rmsnorm(x, w_row, out_dtype):
    m, d = x.shape
    tr = min(m, 256)
    return pl.pallas_call(
        _rmsnorm_kernel,
        grid=(m // tr,),
        in_specs=[pl.BlockSpec((tr, d), lambda i: (i, 0)),
                  pl.BlockSpec((1, d), lambda i: (0, 0))],
        out_specs=pl.BlockSpec((tr, d), lambda i: (i, 0)),
        out_shape=jax.ShapeDtypeStruct((m, d), out_dtype),
        compiler_params=_params("arbitrary"),
        name="rmsnorm",
    )(x, w_row)


def _matmul_kernel(x_ref, xs_ref, w_ref, *rest, has_residual, row_steps, n_stacks, states, steps_per_seq):
    rest = list(rest)
    r_ref, rs_ref = (rest.pop(0), rest.pop(0)) if has_residual else (None, None)
    del rest[:n_stacks]
    o_ref, os_ref, *st_refs, wb_ref = rest
    j = pl.program_id(0)
    i = pl.program_id(1)
    steps_per_part = row_steps // PARTS
    kq = w_ref.shape[0] // steps_per_part
    tm = x_ref.shape[0]

    def stage():
        rows = pl.ds(pl.multiple_of((i % steps_per_part) * kq, kq), kq)
        wb_ref[j % 2, i // steps_per_part, rows, :] = w_ref[rows, :].astype(BF16)

    def multiply(with_sample, st_ref):
        lhs = x_ref[...]
        targets = [(slice(0, tm), r_ref, o_ref)]
        if with_sample:
            lhs = jnp.concatenate([lhs, xs_ref[...]], axis=0)
            targets.append((slice(tm, None), rs_ref, os_ref))
        for part in range(PARTS):
            cols = slice(part * PART, (part + 1) * PART)
            acc = jnp.dot(lhs, wb_ref[(j + 1) % 2, part], preferred_element_type=F32)
            for rows, res_ref, out_ref in targets:
                out = acc[rows]
                if has_residual:
                    out = out + res_ref[:, cols]
                out_ref[:, cols] = out.astype(out_ref.dtype)
        if st_ref is not None:
            tr, heads, dh = st_ref.shape
            st_ref[...] = o_ref[tm - tr:, :].astype(F32).reshape(tr, heads, dh)

    @pl.when(j == 0)
    def _():
        stage()

    feeds = [jnp.logical_and(jnp.logical_or(j == k_block + 1, j == v_block + 1),
                             i % steps_per_seq >= first_active) for k_block, v_block, first_active in states]
    feeds_none = functools.reduce(jnp.logical_and, [jnp.logical_not(f) for f in feeds], j > 0)
    for with_sample in (True, False):
        rows_match = (i == 0) if with_sample else (i > 0)
        for feed, st_ref in [(feeds_none, None)] + list(zip(feeds, st_refs)):
            @pl.when(jnp.logical_and(jnp.logical_and(j > 0, rows_match), feed))
            def _(with_sample=with_sample, st_ref=st_ref):
                multiply(with_sample, st_ref)
                stage()


def _matmul(x, xs, w_stack, layer, out_dtype, runs=None, residual=None, states=(), seq=None, name="matmul"):
    m, k = x.shape
    ms = xs.shape[0]
    tm = min(m // PARTS, 1024 if jnp.dtype(out_dtype).itemsize == 2 else 512)
    row_steps = m // tm
    col_block, nparts = _block_map(runs if runs is not None else ((0, w_stack.shape[-1]),))
    nblocks = nparts // PARTS
    steps_per_part = row_steps // PARTS
    assert nparts % PARTS == 0 and row_steps % PARTS == 0 and k % steps_per_part == 0

    def row_map(j, i):
        return jnp.where(j == 0, 0, i)

    def out_map(j, i):
        return (row_map(j, i), jnp.maximum(j - 1, 0))

    def w_map(j, i):
        return (layer, 0, col_block(PARTS * jnp.minimum(j, nblocks - 1) + i // steps_per_part))

    in_specs = [pl.BlockSpec((tm, k), lambda j, i: (row_map(j, i), 0)),
                pl.BlockSpec((ms, k), lambda j, i: (0, 0)),
                pl.BlockSpec((None, k, PART), w_map)]
    args = [x, xs, w_stack]
    if residual is not None:
        in_specs += [pl.BlockSpec((tm, WIDE), out_map),
                     pl.BlockSpec((ms, WIDE), lambda j, i: (0, jnp.maximum(j - 1, 0)))]
        args += list(residual)
    out_specs = [pl.BlockSpec((tm, WIDE), out_map),
                 pl.BlockSpec((ms, WIDE), lambda j, i: (0, jnp.maximum(j - 1, 0)))]
    out_shape = [jax.ShapeDtypeStruct((m, nblocks * WIDE), out_dtype),
                 jax.ShapeDtypeStruct((ms, nblocks * WIDE), F32)]

    heads = WIDE // HEAD_DIM
    steps_per_seq = (seq // tm) if states else 1
    fed_blocks = [b for _, _, k_block, v_block, _ in states for b in (k_block, v_block)]
    assert len(set(fed_blocks)) == len(fed_blocks)
    kernel_states, aliases, n_stacks = [], {}, 0
    for n, (stack, depth, k_block, v_block, rows) in enumerate(states):
        batch = m // seq
        assert seq % tm == 0 and (rows % tm == 0 or rows < tm) and k_block < v_block
        n_active = max(rows // tm, 1)
        first_active = steps_per_seq - n_active
        kernel_states.append((k_block, v_block, first_active))

        def st_map(j, i, kb=k_block + 1, vb=v_block + 1, first_active=first_active, n_active=n_active, batch=batch):
            s, b = i % steps_per_seq, i // steps_per_seq
            started = s >= first_active
            b_here = jnp.maximum(jnp.where(started, b, b - 1), 0)
            r_here = jnp.where(jnp.logical_and(jnp.logical_not(started), b > 0), n_active - 1,
                               jnp.maximum(s - first_active, 0))
            writing = jnp.logical_or(j == kb, j == vb)
            done = jnp.logical_or(jnp.logical_and(j > kb, j < vb), j > vb)
            b_idx = jnp.where(writing, b_here, jnp.where(done, batch - 1, 0))
            r_idx = jnp.where(writing, r_here, jnp.where(done, n_active - 1, 0))
            return (layer, b_idx, r_idx, jnp.where(j >= vb, 1, 0), 0, 0)

        out_specs.append(pl.BlockSpec((None, None, min(rows, tm), None, heads, HEAD_DIM), st_map))
        out_shape.append(jax.ShapeDtypeStruct((depth, batch, rows, 2, heads, HEAD_DIM), F32))
        if stack is not None:
            in_specs.append(pl.BlockSpec(memory_space=pl.ANY))
            aliases[len(args)] = 2 + n
            args.append(stack)
            n_stacks += 1
    return pl.pallas_call(
        functools.partial(_matmul_kernel, has_residual=residual is not None, row_steps=row_steps,
                          n_stacks=n_stacks, states=tuple(kernel_states), steps_per_seq=steps_per_seq),
        grid=(nblocks + 1, row_steps),
        in_specs=in_specs,
        out_specs=out_specs,
        out_shape=out_shape,
        input_output_aliases=aliases,
        scratch_shapes=[pltpu.VMEM((2, PARTS, k, PART), BF16)],
        compiler_params=_params("arbitrary", "arbitrary"),
        name=name,
    )(*args)


def _branch_kernel(ya_ref, yb_ref, yc_ref, ys_ref, wa_ref, wb_ref, wc_ref, ga_ref, gb_ref, gc_ref,
                   gsa_ref, gsb_ref, gsc_ref, o_ref, os_ref, wa_s, wb_s, wc_s, *, row_steps):
    j = pl.program_id(0)
    i = pl.program_id(1)
    steps_per_part = row_steps // PARTS
    staged = ((wa_ref, wa_s), (wb_ref, wb_s), (wc_ref, wc_s))

    def stage():
        for w_ref, w_s in staged:
            kq = w_ref.shape[0] // steps_per_part
            rows = pl.ds(pl.multiple_of((i % steps_per_part) * kq, kq), kq)
            w_s[j % 2, i // steps_per_part, rows, :] = w_ref[rows, :].astype(BF16)

    def merge(with_sample):
        tm = ya_ref.shape[0]
        lhs = [ya_ref[...], yb_ref[...], yc_ref[...]]
        targets = [(slice(0, tm), (ga_ref, gb_ref, gc_ref), o_ref)]
        if with_sample:
            ys = ys_ref[...].astype(BF16)
            bounds = (0, A_WIDTH, A_WIDTH + B_WIDTH, Y_WIDTH)
            lhs = [jnp.concatenate([y, ys[:, lo:hi]], axis=0) for y, lo, hi in zip(lhs, bounds, bounds[1:])]
            targets.append((slice(tm, None), (gsa_ref, gsb_ref, gsc_ref), os_ref))
        for part in range(PARTS):
            cols = slice(part * PART, (part + 1) * PART)
            products = [jnp.dot(y, w_s[(j + 1) % 2, part], preferred_element_type=F32)
                        for y, (_, w_s) in zip(lhs, staged)]
            for rows, g_refs, out_ref in targets:
                mixed = None
                for u, g_ref in zip(products, g_refs):
                    term = jax.nn.sigmoid(g_ref[:, cols].astype(F32)) * u[rows]
                    mixed = term if mixed is None else mixed + term
                out_ref[:, cols] = mixed.astype(out_ref.dtype)

    @pl.when(j == 0)
    def _():
        stage()

    @pl.when(jnp.logical_and(j > 0, i == 0))
    def _():
        merge(True)
        stage()

    @pl.when(jnp.logical_and(j > 0, i > 0))
    def _():
        merge(False)
        stage()


def _branch_merge(ya, yb, yc, y_s, w_a, w_b, w_c, main, main_s, g_col, layer, d_model):
    m = ya.shape[0]
    ms = y_s.shape[0]
    tm = min(m // PARTS, 512)
    row_steps = m // tm
    steps_per_part = row_steps // PARTS
    nblocks = d_model // WIDE
    assert row_steps % PARTS == 0 and g_col % WIDE == 0 and d_model % WIDE == 0
    assert all(w % steps_per_part == 0 for w in (A_WIDTH, B_WIDTH, C_WIDTH))

    def row_map(j, i):
        return jnp.where(j == 0, 0, i)

    def col_map(j):
        return jnp.maximum(j - 1, 0)

    def y_spec(width):
        return pl.BlockSpec((tm, width), lambda j, i: (row_map(j, i), 0))

    def w_spec(width):
        return pl.BlockSpec((None, width, PART),
                            lambda j, i: (layer, 0, PARTS * jnp.minimum(j, nblocks - 1) + i // steps_per_part))

    def g_spec(rows, idx, whole):
        first = (g_col + idx * d_model) // WIDE
        if whole:
            return pl.BlockSpec((rows, WIDE), lambda j, i: (0, first + col_map(j)))
        return pl.BlockSpec((rows, WIDE), lambda j, i: (row_map(j, i), first + col_map(j)))

    return pl.pallas_call(
        functools.partial(_branch_kernel, row_steps=row_steps),
        grid=(nblocks + 1, row_steps),
        in_specs=[y_spec(A_WIDTH), y_spec(B_WIDTH), y_spec(C_WIDTH),
                  pl.BlockSpec((ms, Y_WIDTH), lambda j, i: (0, 0)),
                  w_spec(A_WIDTH), w_spec(B_WIDTH), w_spec(C_WIDTH),
                  g_spec(tm, 0, False), g_spec(tm, 1, False), g_spec(tm, 2, False),
                  g_spec(ms, 0, True), g_spec(ms, 1, True), g_spec(ms, 2, True)],
        out_specs=[pl.BlockSpec((tm, WIDE), lambda j, i: (row_map(j, i), col_map(j))),
                   pl.BlockSpec((ms, WIDE), lambda j, i: (0, col_map(j)))],
        out_shape=[jax.ShapeDtypeStruct((m, d_model), BF16),
                   jax.ShapeDtypeStruct((ms, d_model), BF16)],
        scratch_shapes=[pltpu.VMEM((2, PARTS, w, PART), BF16) for w in (A_WIDTH, B_WIDTH, C_WIDTH)],
        compiler_params=_params("arbitrary", "arbitrary"),
        name="branch_merge",
    )(ya, yb, yc, y_s, w_a, w_b, w_c, main, main, main, main_s, main_s, main_s)


def _band_bias(slope_step, rows_per_tile=1):
    shape = (rows_per_tile * BAND, BAND)
    row = lax.broadcasted_iota(jnp.int32, shape, 0) % BAND
    col = lax.broadcasted_iota(jnp.int32, shape, 1)
    dist = (row - col).astype(F32)
    cur = jnp.where(col <= row, -slope_step * dist, NEG_BIG)
    prev = jnp.where(col >= row, -slope_step * (dist + float(BAND)), NEG_BIG)
    return cur, prev


def _block_rows(n):
    start = n * BAND if isinstance(n, int) else pl.multiple_of(n * BAND, BAND)
    return pl.ds(start, BAND)


def _qk(q, k):
    return lax.dot_general(q, k, (((1,), (1,)), ((), ())), preferred_element_type=F32) * SCALE


def _band_attention(blocks, sink=None, carry=None):
    logits = []
    for q, k_cur, _, bias_cur, k_prev, _, bias_prev in blocks:
        logit_p = None if k_prev is None else _qk(q, k_prev) + bias_prev
        logits.append((_qk(q, k_cur) + bias_cur, logit_p))
    probs = []
    for n, (logit_c, logit_p) in enumerate(logits):
        m = jnp.max(logit_c, axis=-1, keepdims=True)
        if logit_p is not None:
            m = jnp.maximum(m, jnp.max(logit_p, axis=-1, keepdims=True))
        if sink is not None:
            m = jnp.maximum(m, sink)
        m_tile = jnp.broadcast_to(m, logit_c.shape)
        alpha = None
        if carry is not None:
            m_tile = jnp.maximum(m_tile, carry[n][1])
            alpha = jnp.exp(carry[n][1] - m_tile)
        e_c = jnp.exp(logit_c - m_tile)
        den = jnp.sum(e_c, axis=-1, keepdims=True)
        e_p = None
        if logit_p is not None:
            e_p = jnp.exp(logit_p - m_tile)
            den = den + jnp.sum(e_p, axis=-1, keepdims=True)
            e_p = e_p.astype(BF16)
        if sink is not None:
            den = den + jnp.exp(sink - m)
        den_tile = jnp.broadcast_to(den, logit_c.shape)
        if carry is not None:
            den_tile = den_tile + carry[n][2] * alpha
        probs.append((e_c.astype(BF16), e_p, m_tile, den_tile, alpha))
    outs = []
    for n, ((_, _, v_cur, _, _, v_prev, _), (e_c, e_p, m, den, alpha)) in enumerate(zip(blocks, probs)):
        acc = jnp.dot(e_c, v_cur, preferred_element_type=F32)
        if e_p is not None:
            acc = acc + jnp.dot(e_p, v_prev, preferred_element_type=F32)
        if carry is not None:
            acc = acc + carry[n][0] * alpha
        outs.append((acc, m, den))
    return outs


def _attn_a_kernel(slopes_ref, sinks_ref, q_ref, k_ref, v_ref, z_ref, y_ref, bias_s, sink_s):
    kv = pl.program_id(1)
    rows = A_GROUP * BAND
    head = lax.broadcasted_iota(jnp.int32, (rows, 1), 0) // BAND
    slope_col = jnp.zeros((rows, 1), F32)
    sink_col = jnp.zeros((rows, 1), F32)
    for g in range(A_GROUP):
        slope_col = jnp.where(head == g, slopes_ref[kv * A_GROUP + g], slope_col)
        sink_col = jnp.where(head == g, sinks_ref[kv * A_GROUP + g], sink_col)
    bias_s[0], bias_s[1] = _band_bias(slope_col, A_GROUP)
    sink_s[...] = sink_col
    nb = q_ref.shape[0] // BAND

    def blocks(first, first_has_prev):
        rows_ = [_block_rows(first + j) for j in range(A_INFLIGHT)]
        kv_ = [(k_ref[rw, :], v_ref[rw, :]) for rw in rows_]
        if first_has_prev:
            before = _block_rows(first - 1)
            kv_before = (k_ref[before, :], v_ref[before, :])
        work = []
        for j, rw in enumerate(rows_):
            q = jnp.concatenate([q_ref[rw, g * HEAD_DIM:(g + 1) * HEAD_DIM] for g in range(A_GROUP)], axis=0)
            if j == 0 and not first_has_prev:
                work.append((q, *kv_[j], bias_s[0], None, None, None))
            else:
                work.append((q, *kv_[j], bias_s[0], *(kv_[j - 1] if j else kv_before), bias_s[1]))
        zs = [[z_ref[rw, g * HEAD_DIM:(g + 1) * HEAD_DIM].astype(F32) for g in range(A_GROUP)] for rw in rows_]
        outs = []
        for (acc, _, den), z in zip(_band_attention(work, sink=sink_s[...]), zs):
            o = acc / den
            outs.append([(o[g * BAND:(g + 1) * BAND] * _silu(z[g])).astype(y_ref.dtype) for g in range(A_GROUP)])
        for rw, ys in zip(rows_, outs):
            for g in range(A_GROUP):
                y_ref[rw, g * HEAD_DIM:(g + 1) * HEAD_DIM] = ys[g]

    blocks(0, False)

    def body(t, carry):
        blocks(t * A_INFLIGHT, True)
        return carry

    lax.fori_loop(1, nb // A_INFLIGHT, body, 0)


def _attn_a(main, slopes, sinks, main_runs, batch, seq):
    gw = A_GROUP * HEAD_DIM

    def spec(off, width):
        c = _compact(main_runs, off) // width
        return pl.BlockSpec((seq, width), lambda b, kv: (b, c + kv))

    return pl.pallas_call(
        _attn_a_kernel,
        grid=(batch, A_KV_HEADS),
        in_specs=[_smem(), _smem(), spec(OFF_QA, gw), spec(OFF_KA, HEAD_DIM), spec(OFF_VA, HEAD_DIM),
                  spec(OFF_ZA, gw)],
        out_specs=pl.BlockSpec((seq, gw), lambda b, kv: (b, kv)),
        out_shape=jax.ShapeDtypeStruct((batch * seq, A_WIDTH), BF16),
        scratch_shapes=[pltpu.VMEM((2, A_GROUP * BAND, BAND), F32), pltpu.VMEM((A_GROUP * BAND, 1), F32)],
        compiler_params=_params("arbitrary", "arbitrary"),
        name="attn_a",
    )(slopes, sinks, main, main, main, main)


def _attn_b_kernel(slopes_ref, q1, k1, v1, q2, k2, v2, q3, k3, v3, z_ref, y_ref, acc_s, m_s, l_s, bias_s):
    h = pl.program_id(1)
    seq = y_ref.shape[0]
    for g, (_, dil) in enumerate(B_GROUPS):
        bias_s[2 * g], bias_s[2 * g + 1] = _band_bias(slopes_ref[g * B_HEADS + h] * float(dil))

    def load_state(rows):
        return [(acc_s[rw, :], m_s[rw, :], l_s[rw, :]) for rw in rows]

    def store_state(rows, outs):
        for rw, (acc, m, den) in zip(rows, outs):
            acc_s[rw, :] = acc
            m_s[rw, :] = m
            l_s[rw, :] = den


    dil3 = B_GROUPS[2][1]
    assert seq // dil3 == BAND

    def g3_body(t, carry):
        rows = [pl.ds(t * B_INFLIGHT + j, BAND, stride=dil3) for j in range(B_INFLIGHT)]
        qkv = [tuple(a[rw, :].astype(BF16) for a in (q3, k3, v3)) for rw in rows]
        store_state(rows, _band_attention([(q, k, v, bias_s[4], None, None, None) for q, k, v in qkv]))
        return carry

    lax.fori_loop(0, dil3 // B_INFLIGHT, g3_body, 0)

    dil2 = B_GROUPS[1][1]
    nb2 = seq // dil2 // BAND
    per_body = max(1, B_INFLIGHT // nb2)

    def g2_body(t, carry):
        rows = [pl.ds(t * per_body + i + n * BAND * dil2, BAND, stride=dil2)
                for i in range(per_body) for n in range(nb2)]
        qkv = [tuple(a[rw, :].astype(BF16) for a in (q2, k2, v2)) for rw in rows]
        work = [(q, k, v, bias_s[2], None, None, None) if j % nb2 == 0 else
                (q, k, v, bias_s[2], qkv[j - 1][1], qkv[j - 1][2], bias_s[3]) for j, (q, k, v) in enumerate(qkv)]
        store_state(rows, _band_attention(work, carry=load_state(rows)))
        return carry

    lax.fori_loop(0, dil2 // per_body, g2_body, 0)

    def g1_blocks(first, first_has_prev):
        rows = [_block_rows(first + j) for j in range(B_INFLIGHT)]
        kv = [(k1[rw, :], v1[rw, :]) for rw in rows]
        if first_has_prev:
            before = _block_rows(first - 1)
            kv_before = (k1[before, :], v1[before, :])
        qs = [q1[rw, :] for rw in rows]
        zs = [z_ref[rw, :].astype(F32) for rw in rows]
        work = [(qs[j], *kv[j], bias_s[0], None, None, None) if j == 0 and not first_has_prev else
                (qs[j], *kv[j], bias_s[0], *(kv[j - 1] if j else kv_before), bias_s[1]) for j in range(B_INFLIGHT)]
        outs = [((acc / den) * _silu(z)).astype(y_ref.dtype)
                for (acc, _, den), z in zip(_band_attention(work, carry=load_state(rows)), zs)]
        for rw, y in zip(rows, outs):
            y_ref[rw, :] = y

    g1_blocks(0, False)

    def g1_body(t, carry):
        g1_blocks(t * B_INFLIGHT, True)
        return carry

    lax.fori_loop(1, seq // BAND // B_INFLIGHT, g1_body, 0)


def _attn_b(main, strided, slopes_b, main_runs, batch, seq):
    def spec(runs, off):
        c = _compact(runs, off) // HEAD_DIM
        return pl.BlockSpec((seq, HEAD_DIM), lambda b, h: (b, c + h))

    in_specs = [_smem()]
    args = [slopes_b]
    for g in range(len(B_GROUPS)):
        runs, arr = (main_runs, main) if g == 0 else (_STRIDED_RUNS, strided)
        in_specs += [spec(runs, off + g * B_WIDTH) for off in (OFF_QB, OFF_KB, OFF_VB)]
        args += [arr] * 3
    in_specs.append(spec(main_runs, OFF_ZB))
    args.append(main)
    return pl.pallas_call(
        _attn_b_kernel,
        grid=(batch, B_HEADS),
        in_specs=in_specs,
        out_specs=pl.BlockSpec((seq, HEAD_DIM), lambda b, h: (b, h)),
        out_shape=jax.ShapeDtypeStruct((batch * seq, B_WIDTH), BF16),
        scratch_shapes=[pltpu.VMEM((seq, HEAD_DIM), F32)] * 3 + [pltpu.VMEM((2 * len(B_GROUPS), BAND, BAND), F32)],
        compiler_params=_params("arbitrary", "arbitrary"),
        name="attn_b",
    )(*args)


def _head_norm_gate(o, w_row, z):
    mu = jnp.mean(o, axis=-1, keepdims=True)
    var = jnp.mean(jnp.square(o - mu), axis=-1, keepdims=True)
    y = (o - mu) * lax.rsqrt(var + NORM_EPS) * w_row
    return y * _silu(z)


def _retention_kernel(lg_ref, q_ref, k_ref, v_ref, z_ref, w_ref, y_ref, s_ref, decay_s):
    lg = lg_ref[pl.program_id(1)]
    row = lax.broadcasted_iota(jnp.int32, (C_CHUNK, C_CHUNK), 0)
    col = lax.broadcasted_iota(jnp.int32, (C_CHUNK, C_CHUNK), 1)
    diff = (row - col).astype(F32)
    decay_s[...] = jnp.where(diff >= 0, jnp.exp(lg * jnp.maximum(diff, 0.0)), 0.0)
    pos = lax.broadcasted_iota(jnp.int32, (C_CHUNK, 1), 0).astype(F32)
    q_dec = jnp.exp(lg * (pos + 1.0))
    k_dec = jnp.exp(lg * (float(C_CHUNK) - 1.0 - pos))
    chunk_decay = jnp.exp(lg * float(C_CHUNK))
    w_row = w_ref[...].astype(F32)

    def body(t, state):
        rows = [pl.ds(pl.multiple_of((t * C_INFLIGHT + j) * C_CHUNK, C_CHUNK), C_CHUNK) for j in range(C_INFLIGHT)]
        qs = [q_ref[rw, :].astype(F32) for rw in rows]
        ks = [k_ref[rw, :].astype(F32) * (C_QK_DIM ** -0.5) for rw in rows]
        vs = [v_ref[rw, :] for rw in rows]
        zs = [z_ref[rw, :].astype(F32) for rw in rows]
        attn = [lax.dot_general(q.astype(BF16), k.astype(BF16), (((1,), (1,)), ((), ())),
                                preferred_element_type=F32) for q, k in zip(qs, ks)]
        update = [jnp.dot(jnp.transpose(k * k_dec).astype(BF16), v, preferred_element_type=F32)
                  for k, v in zip(ks, vs)]
        intra = [jnp.dot((a * decay_s[...]).astype(BF16), v, preferred_element_type=F32)
                 for a, v in zip(attn, vs)]
        states = [state]
        for u in update:
            states.append(chunk_decay * states[-1] + u)
        cross = [jnp.dot((q * q_dec).astype(BF16), s0.astype(BF16), preferred_element_type=F32)
                 for q, s0 in zip(qs, states)]
        outs = [_head_norm_gate(i + c, w_row, z).astype(y_ref.dtype) for i, c, z in zip(intra, cross, zs)]
        for rw, y in zip(rows, outs):
            y_ref[rw, :] = y
        return states[-1]

    s_ref[...] = lax.fori_loop(0, q_ref.shape[0] // C_CHUNK // C_INFLIGHT, body,
                               jnp.zeros((C_QK_DIM, C_V_DIM), F32))


def _retention(main, log_gamma, ret_w_row, main_runs, batch, seq):
    def spec(off, width):
        c = _compact(main_runs, off) // width
        return pl.BlockSpec((seq, width), lambda b, h: (b, c + h))

    return pl.pallas_call(
        _retention_kernel,
        grid=(batch, C_HEADS),
        in_specs=[_smem(), spec(OFF_QC, C_QK_DIM), spec(OFF_KC, C_QK_DIM), spec(OFF_VC, C_V_DIM),
                  spec(OFF_ZC, C_V_DIM), pl.BlockSpec((1, C_V_DIM), lambda b, h: (0, h))],
        out_specs=[pl.BlockSpec((seq, C_V_DIM), lambda b, h: (b, h)),
                   pl.BlockSpec((None, None, C_QK_DIM, C_V_DIM), lambda b, h: (b, h, 0, 0))],
        out_shape=[jax.ShapeDtypeStruct((batch * seq, C_WIDTH), BF16),
                   jax.ShapeDtypeStruct((batch, C_HEADS, C_QK_DIM, C_V_DIM), F32)],
        scratch_shapes=[pltpu.VMEM((C_CHUNK, C_CHUNK), F32)],
        compiler_params=_params("arbitrary", "arbitrary"),
        name="retention",
    )(log_gamma, main, main, main, main, ret_w_row)


def _sample_attend(q, k_buf, v_buf, k_new, v_new, slope, sink=None):
    steps_back = float(BAND) - lax.broadcasted_iota(jnp.int32, (1, BAND, 1, 1), 1).astype(F32)
    logit_b = jnp.sum(k_buf * q, axis=-1, keepdims=True) * SCALE - slope * steps_back
    logit_n = jnp.sum(k_new * q, axis=-1, keepdims=True) * SCALE
    m = jnp.maximum(jnp.max(logit_b, axis=1, keepdims=True), logit_n)
    if sink is not None:
        m = jnp.maximum(m, sink)
    e_b = jnp.exp(logit_b - m)
    e_n = jnp.exp(logit_n - m)
    den = jnp.sum(e_b, axis=1, keepdims=True) + e_n
    if sink is not None:
        den = den + jnp.exp(sink - m)
    acc = jnp.sum(e_b * v_buf, axis=1, keepdims=True) + e_n * v_new
    return acc, m, den


def _column(row_vec):
    n = row_vec.shape[-1]
    eye = lax.broadcasted_iota(jnp.int32, (n, n), 0) == lax.broadcasted_iota(jnp.int32, (n, n), 1)
    return jnp.sum(jnp.where(eye, row_vec, 0.0), axis=-1, keepdims=True)


def _sample_kernel(lg_ref, sa_ref, sb_ref, sink_ref, hs_ref, p_ref, ca_ref, cb1_ref, cb2_ref, cb3_ref,
                   st_ref, w_ref, *rest):
    yab_ref, yc_ref, st_out = rest[-3:]

    def heads(off, count):
        u = off // HEAD_DIM
        return hs_ref[:, u:u + count, :][:, None]

    for kv in range(A_KV_HEADS):
        hq = slice(kv * A_GROUP, (kv + 1) * A_GROUP)
        acc, _, den = _sample_attend(
            heads(OFF_QA + kv * A_GROUP * HEAD_DIM, A_GROUP),
            ca_ref[:, :, 0, kv:kv + 1, :], ca_ref[:, :, 1, kv:kv + 1, :],
            heads(OFF_KA + kv * HEAD_DIM, 1), heads(OFF_VA + kv * HEAD_DIM, 1),
            sa_ref[hq, 0:1][None, None], sink_ref[hq, 0:1][None, None])
        z = heads(OFF_ZA + kv * A_GROUP * HEAD_DIM, A_GROUP)
        yab_ref[:, hq, :] = ((acc / den) * _silu(z))[:, 0]

    parts = []
    for g, (cache, (_, dil)) in enumerate(zip((cb1_ref, cb2_ref, cb3_ref), B_GROUPS)):
        off = g * B_WIDTH
        parts.append(_sample_attend(
            heads(OFF_QB + off, B_HEADS), cache[:, :, 0], cache[:, :, 1],
            heads(OFF_KB + off, B_HEADS), heads(OFF_VB + off, B_HEADS),
            sb_ref[g * B_HEADS:(g + 1) * B_HEADS, 0:1][None, None] * float(dil)))
    m_all = jnp.maximum(jnp.maximum(parts[0][1], parts[1][1]), parts[2][1])
    num = sum(acc * jnp.exp(m - m_all) for acc, m, _ in parts)
    den = sum(d * jnp.exp(m - m_all) for _, m, d in parts)
    yab_ref[:, A_HEADS:A_HEADS + B_HEADS, :] = ((num / den) * _silu(heads(OFF_ZB, B_HEADS)))[:, 0]

    for h in range(C_HEADS):
        gamma = jnp.exp(lg_ref[h])
        vo = h * C_V_DIM
        for i in range(SAMPLE_CHUNK):
            q = p_ref[i, :, OFF_QC + h * C_QK_DIM:OFF_QC + (h + 1) * C_QK_DIM]
            k = p_ref[i, :, OFF_KC + h * C_QK_DIM:OFF_KC + (h + 1) * C_QK_DIM] * (C_QK_DIM ** -0.5)
            v = p_ref[i, :, OFF_VC + vo:OFF_VC + vo + C_V_DIM]
            z = p_ref[i, :, OFF_ZC + vo:OFF_ZC + vo + C_V_DIM]
            s0 = st_ref[i, h]
            intra = jnp.sum(q * k, axis=-1, keepdims=True) * v
            cross = jnp.sum(_column(q * gamma) * s0, axis=0, keepdims=True)
            st_out[i, h] = gamma * s0 + _column(k) * v
            yc_ref[i, :, vo:vo + C_V_DIM] = _head_norm_gate(intra + cross, w_ref[:, vo:vo + C_V_DIM].astype(F32), z)


def _sample_mixers(flat, caches, state, state_stack, layer, log_gamma, slopes_a, slopes_b, sinks, ret_w_row):
    n, width = flat.shape
    nc = SAMPLE_CHUNK
    units = width // HEAD_DIM
    hs = flat.reshape(n, units, HEAD_DIM)
    p4 = flat.reshape(n // nc, nc, 1, width)
    depth = caches[0].shape[0]
    views = [caches[0]]
    for c, (win, dil) in zip(caches[1:], B_GROUPS):
        views.append(c.reshape(depth, n, BAND, dil, 2, B_HEADS, HEAD_DIM))

    def lanes(vec):
        return jnp.broadcast_to(vec.astype(F32)[:, None], (vec.shape[0], HEAD_DIM))

    def cache_b_spec():
        return pl.BlockSpec((None, nc, BAND, None, 2, B_HEADS, HEAD_DIM), lambda i: (layer, i, 0, 0, 0, 0, 0))

    args = [log_gamma, lanes(slopes_a), lanes(slopes_b), lanes(sinks), hs, p4, *views, state, ret_w_row]
    in_specs = [_smem(),
                pl.BlockSpec((A_HEADS, HEAD_DIM), lambda i: (0, 0)),
                pl.BlockSpec((len(B_GROUPS) * B_HEADS, HEAD_DIM), lambda i: (0, 0)),
                pl.BlockSpec((A_HEADS, HEAD_DIM), lambda i: (0, 0)),
                pl.BlockSpec((nc, units, HEAD_DIM), lambda i: (i, 0, 0)),
                pl.BlockSpec((None, nc, 1, width), lambda i: (i, 0, 0, 0)),
                pl.BlockSpec((None, nc, BAND, 2, A_KV_HEADS, HEAD_DIM), lambda i: (layer, i, 0, 0, 0, 0)),
                cache_b_spec(), cache_b_spec(), cache_b_spec(),
                pl.BlockSpec((None, nc, C_HEADS, C_QK_DIM, C_V_DIM), lambda i: (layer, i, 0, 0, 0)),
                pl.BlockSpec((1, C_WIDTH), lambda i: (0, 0))]
    aliases = {}
    if state_stack is not None:
        in_specs.append(pl.BlockSpec(memory_space=pl.ANY))
        aliases = {len(args): 2}
        args.append(state_stack)
    yab, yc, st = pl.pallas_call(
        _sample_kernel,
        grid=(n // nc,),
        in_specs=in_specs,
        out_specs=[pl.BlockSpec((nc, A_HEADS + B_HEADS, HEAD_DIM), lambda i: (i, 0, 0)),
                   pl.BlockSpec((None, nc, 1, C_WIDTH), lambda i: (i, 0, 0, 0)),
                   pl.BlockSpec((None, nc, C_HEADS, C_QK_DIM, C_V_DIM), lambda i: (layer, i, 0, 0, 0))],
        out_shape=[jax.ShapeDtypeStruct((n, A_HEADS + B_HEADS, HEAD_DIM), F32),
                   jax.ShapeDtypeStruct((n // nc, nc, 1, C_WIDTH), F32),
                   jax.ShapeDtypeStruct((depth, n, C_HEADS, C_QK_DIM, C_V_DIM), F32)],
        input_output_aliases=aliases,
        compiler_params=_params("arbitrary"),
        name="sample_mixers",
    )(*args)
    y = jnp.concatenate([yab.reshape(n, A_WIDTH + B_WIDTH), yc.reshape(n, C_WIDTH)], axis=-1)
    return y, st


def _kv_rows(k_arr, k_col, v_arr, v_col, batch, heads, rows):
    w = heads * HEAD_DIM
    t = k_arr.shape[0] // batch
    k = k_arr.reshape(batch, t, -1)[:, t - rows:, k_col:k_col + w]
    v = v_arr.reshape(batch, t, -1)[:, t - rows:, v_col:v_col + w]
    return jnp.stack([k, v], axis=2).reshape(batch, rows, 2, heads, HEAD_DIM).astype(F32)


def _window_rows(main, strided, main_runs, batch):
    t = main.shape[0] // batch
    out = [_kv_rows(main, _compact(main_runs, OFF_KA), main, _compact(main_runs, OFF_VA), batch,
                    A_KV_HEADS, min(BAND, t))]
    for g, (win, _) in enumerate(B_GROUPS):
        runs, arr = (main_runs, main) if g == 0 else (_STRIDED_RUNS, strided)
        out.append(_kv_rows(arr, _compact(runs, OFF_KB + g * B_WIDTH), arr, _compact(runs, OFF_VB + g * B_WIDTH),
                            batch, B_HEADS, min(win, t)))
    return out


def _kv_rows_kernel(*refs, pieces):
    k_refs, v_refs, o_ref = refs[:pieces], refs[pieces:2 * pieces], refs[-1]
    rows, _, heads, dh = o_ref.shape
    for slot, part_refs in enumerate((k_refs, v_refs)):
        flat = jnp.concatenate([r[...].astype(F32) for r in part_refs], axis=-1)
        o_ref[:, slot] = flat.reshape(rows, heads, dh)


def _kv_rows_into(stack, layer, depth, k_arr, k_col, v_arr, v_col, batch, heads, rows):
    w = heads * HEAD_DIM
    pw = min(w, TN)
    pieces = w // pw
    t = k_arr.shape[0] // batch
    tr = min(rows, 512)
    first, per_seq = (t - rows) // tr, t // tr

    def piece(col):
        return pl.BlockSpec((tr, pw), lambda b, i: (b * per_seq + first + i, col // pw))

    in_specs = ([piece(k_col + p * pw) for p in range(pieces)] + [piece(v_col + p * pw) for p in range(pieces)])
    args = [k_arr] * pieces + [v_arr] * pieces
    aliases = {}
    if stack is not None:
        in_specs.append(pl.BlockSpec(memory_space=pl.ANY))
        args.append(stack)
        aliases = {2 * pieces: 0}
    return pl.pallas_call(
        functools.partial(_kv_rows_kernel, pieces=pieces),
        grid=(batch, rows // tr),
        in_specs=in_specs,
        out_specs=pl.BlockSpec((None, None, tr, 2, heads, HEAD_DIM), lambda b, i: (layer, b, i, 0, 0, 0)),
        out_shape=jax.ShapeDtypeStruct((depth, batch, rows, 2, heads, HEAD_DIM), F32),
        input_output_aliases=aliases,
        compiler_params=_params("arbitrary", "arbitrary"),
        name="kv_rows",
    )(*args)


def _window_rows_into(stacks, layer, depth, main, main_runs, batch):
    t = main.shape[0] // batch
    stacks = stacks or [None, None]
    return [_kv_rows_into(stacks[0], layer, depth, main, _compact(main_runs, OFF_KA), main,
                          _compact(main_runs, OFF_VA), batch, A_KV_HEADS, min(BAND, t)),
            _kv_rows_into(stacks[1], layer, depth, main, _compact(main_runs, OFF_KB), main,
                          _compact(main_runs, OFF_VB), batch, B_HEADS, min(B_GROUPS[0][0], t))]


def _reference_order(main, strided, main_runs, upto):
    pieces = []
    for (lo, hi), src in sorted([(r, main) for r in main_runs] + [(r, strided) for r in _STRIDED_RUNS]):
        if lo >= upto:
            break
        runs = main_runs if src is main else _STRIDED_RUNS
        c = _compact(runs, lo)
        pieces.append(src[:, c:c + min(hi, upto) - lo])
    return jnp.concatenate(pieces, axis=-1)


def kernel(x_prompt, x_sample, cache_a, cache_b1, cache_b2, cache_b3, state_c, norm_w, w_in,
           w_branch_a, w_branch_b, w_branch_c, w_out, attn_sinks, ret_norm_w, final_norm_w):
    batch, seq, d_model = x_prompt.shape
    n_s, t_s, _ = x_sample.shape
    depth = w_in.shape[0]
    width = w_in.shape[-1]
    assert t_s == 1 and seq == B_GROUPS[-1][1] * BAND and n_s % SAMPLE_CHUNK == 0
    assert width == OFF_G + 3 * d_model and d_model % TN == 0
    assert cache_a.shape[2] == BAND
    assert all(c.shape[2] == win for c, (win, _) in zip((cache_b1, cache_b2, cache_b3), B_GROUPS))

    slopes_a = jnp.exp2(-8.0 * jnp.arange(1, A_HEADS + 1, dtype=F32) / A_HEADS)
    nbh = len(B_GROUPS) * B_HEADS
    slopes_b = jnp.exp2(-8.0 * jnp.arange(1, nbh + 1, dtype=F32) / nbh)
    log_gamma = jnp.log1p(-jnp.exp2(-5.0 - jnp.arange(C_HEADS, dtype=F32)))
    caches = (cache_a, cache_b1, cache_b2, cache_b3)
    main_runs = _main_runs(width)
    g_col = _compact(main_runs, OFF_G)

    xp = x_prompt.reshape(batch * seq, d_model)
    xs = x_sample.reshape(n_s, d_model)
    p_window, p_c, s_states, s_c = None, [], [], None
    for l in range(depth):
        nw = norm_w[l].reshape(1, d_model)
        rw = ret_norm_w[l].reshape(1, C_WIDTH)
        sinks = attn_sinks[l]

        hp = _rmsnorm(xp, nw, BF16)
        hsm = _rmsnorm(xs, nw, BF16)
        main, main_s = _matmul(hp, hsm, w_in, l, BF16, runs=main_runs, name="proj_main")
        strided_states = tuple(
            (None if p_window is None else p_window[1 + g], depth,
             _compact(_STRIDED_RUNS, OFF_KB + g * B_WIDTH) // WIDE, _compact(_STRIDED_RUNS, OFF_VB + g * B_WIDTH) // WIDE,
             min(B_GROUPS[g][0], seq)) for g in (1, 2))
        strided, strided_s, *strided_stacks = _matmul(hp, hsm, w_in, l, F32, runs=_STRIDED_RUNS,
                                                      states=strided_states, seq=seq, name="proj_strided")

        ya = _attn_a(main, slopes_a, sinks, main_runs, batch, seq)
        yb = _attn_b(main, strided, slopes_b, main_runs, batch, seq)
        yc, new_c = _retention(main, log_gamma, rw, main_runs, batch, seq)
        y_s, s_c = _sample_mixers(_reference_order(main_s, strided_s, main_runs, OFF_G), caches, state_c, s_c, l,
                                  log_gamma, slopes_a, slopes_b, sinks, rw)

        mixed, mixed_s = _branch_merge(ya, yb, yc, y_s, w_branch_a, w_branch_b, w_branch_c,
                                       main, main_s, g_col, l, d_model)
        xp, xs = _matmul(mixed, mixed_s, w_out, l, F32, residual=(xp, xs), name="proj_out")
        p_window = _window_rows_into(p_window, l, depth, main, main_runs, batch) + strided_stacks
        p_c.append(new_c)
        s_states.append(_window_rows(main_s, strided_s, main_runs, n_s))

    fw = final_norm_w.reshape(1, d_model)
    y_prompt = _rmsnorm(xp, fw, F32).reshape(batch, seq, d_model)
    y_sample = _rmsnorm(xs, fw, F32).reshape(n_s, 1, d_model)
    sample = [jnp.stack([s[i] for s in s_states]) for i in range(4)]
    return (y_prompt, y_sample, *p_window, jnp.stack(p_c), *sample, s_c)
```

```python
import functools

import jax
import jax.numpy as jnp
from jax import lax
from jax.experimental import pallas as pl
from jax.experimental.pallas import tpu as pltpu

F32 = jnp.float32
BF16 = jnp.bfloat16

HEAD_DIM = 128
BAND = 128
NORM_EPS = 1e-6
A_HEADS = 8
A_KV_HEADS = 2
A_GROUP = A_HEADS // A_KV_HEADS
B_GROUPS = ((128, 1), (512, 4), (2048, 16))
B_HEADS = 8
C_HEADS = 6
C_QK_DIM = 128
C_V_DIM = 256
C_CHUNK = 128

A_WIDTH = A_HEADS * HEAD_DIM
A_KV_WIDTH = A_KV_HEADS * HEAD_DIM
B_WIDTH = B_HEADS * HEAD_DIM
B_QKV_WIDTH = len(B_GROUPS) * B_WIDTH
C_QK_WIDTH = C_HEADS * C_QK_DIM
C_WIDTH = C_HEADS * C_V_DIM
Y_WIDTH = A_WIDTH + B_WIDTH + C_WIDTH

OFF_QA = 0
OFF_KA = OFF_QA + A_WIDTH
OFF_VA = OFF_KA + A_KV_WIDTH
OFF_ZA = OFF_VA + A_KV_WIDTH
OFF_QB = OFF_ZA + A_WIDTH
OFF_KB = OFF_QB + B_QKV_WIDTH
OFF_VB = OFF_KB + B_QKV_WIDTH
OFF_ZB = OFF_VB + B_QKV_WIDTH
OFF_QC = OFF_ZB + B_WIDTH
OFF_KC = OFF_QC + C_QK_WIDTH
OFF_VC = OFF_KC + C_QK_WIDTH
OFF_ZC = OFF_VC + C_WIDTH
OFF_G = OFF_ZC + C_WIDTH

TN = 512
PART = 256
PARTS = 4
WIDE = PARTS * PART
NEG_BIG = -1e30
SCALE = HEAD_DIM ** -0.5
assert BAND == HEAD_DIM
VMEM_LIMIT = 56 * 1024 * 1024
SAMPLE_CHUNK = 2
B_INFLIGHT = 8
A_INFLIGHT = 4
C_INFLIGHT = 8

_STRIDED_RUNS = tuple((off + B_WIDTH, off + B_QKV_WIDTH) for off in (OFF_QB, OFF_KB, OFF_VB))


def _main_runs(width):
    edges = [0] + [e for run in _STRIDED_RUNS for e in run] + [width]
    return tuple((edges[i], edges[i + 1]) for i in range(0, len(edges), 2))


def _compact(runs, col):
    base = 0
    for lo, hi in runs:
        if lo <= col < hi:
            return base + col - lo
        base += hi - lo
    raise ValueError(col)


def _block_map(runs):
    starts, gaps, pos, prev_hi = [], [], 0, 0
    for lo, hi in runs:
        starts.append(pos // PART)
        gaps.append((lo - prev_hi) // PART)
        pos += hi - lo
        prev_hi = hi

    def col_block(j):
        out = j
        for s, g in zip(starts, gaps):
            if g:
                out = out + jnp.where(j >= s, g, 0)
        return out

    return col_block, pos // PART


def _params(*sem):
    return pltpu.CompilerParams(dimension_semantics=sem, vmem_limit_bytes=VMEM_LIMIT)


def _smem():
    return pl.BlockSpec(memory_space=pltpu.SMEM)


def _silu(z):
    return z * jax.nn.sigmoid(z)


def _rmsnorm_kernel(x_ref, w_ref, o_ref):
    x = x_ref[...].astype(F32)
    ms = jnp.mean(x * x, axis=-1, keepdims=True)
    o_ref[...] = (x * lax.rsqrt(ms + NORM_EPS) * w_ref[...].astype(F32)).astype(o_ref.dtype)


def _rmsnorm(x, w_row, out_dtype):
    m, d = x.shape
    tr = min(m, 512)
    return pl.pallas_call(
        _rmsnorm_kernel,
        grid=(m // tr,),
        in_specs=[pl.BlockSpec((tr, d), lambda i: (i, 0)),
                  pl.BlockSpec((1, d), lambda i: (0, 0))],
        out_specs=pl.BlockSpec((tr, d), lambda i: (i, 0)),
        out_shape=jax.ShapeDtypeStruct((m, d), out_dtype),
        compiler_params=_params("arbitrary"),
        name="rmsnorm",
    )(x, w_row)


def _matmul_kernel(x_ref, xs_ref, w_ref, *rest, has_residual, row_steps, n_stacks, states, steps_per_seq):
    rest = list(rest)
    r_ref, rs_ref = (rest.pop(0), rest.pop(0)) if has_residual else (None, None)
    del rest[:n_stacks]
    o_ref, os_ref, *st_refs, wb_ref = rest
    j = pl.program_id(0)
    i = pl.program_id(1)
    steps_per_part = row_steps // PARTS
    kq = w_ref.shape[0] // steps_per_part
    tm = x_ref.shape[0]

    def stage():
        rows = pl.ds(pl.multiple_of((i % steps_per_part) * kq, kq), kq)
        wb_ref[j % 2, i // steps_per_part, rows, :] = w_ref[rows, :].astype(BF16)

    def multiply(with_sample, st_ref):
        lhs = x_ref[...]
        targets = [(slice(0, tm), r_ref, o_ref)]
        if with_sample:
            lhs = jnp.concatenate([lhs, xs_ref[...]], axis=0)
            targets.append((slice(tm, None), rs_ref, os_ref))
        for part in range(PARTS):
            cols = slice(part * PART, (part + 1) * PART)
            acc = jnp.dot(lhs, wb_ref[(j + 1) % 2, part], preferred_element_type=F32)
            for rows, res_ref, out_ref in targets:
                out = acc[rows]
                if has_residual:
                    out = out + res_ref[:, cols]
                out_ref[:, cols] = out.astype(out_ref.dtype)
        if st_ref is not None:
            tr, heads, dh = st_ref.shape
            st_ref[...] = o_ref[tm - tr:, :].astype(F32).reshape(tr, heads, dh)

    @pl.when(j == 0)
    def _():
        stage()

    feeds = [jnp.logical_and(jnp.logical_or(j == k_block + 1, j == v_block + 1),
                             i % steps_per_seq >= first_active) for k_block, v_block, first_active in states]
    feeds_none = functools.reduce(jnp.logical_and, [jnp.logical_not(f) for f in feeds], j > 0)
    for with_sample in (True, False):
        rows_match = (i == 0) if with_sample else (i > 0)
        for feed, st_ref in [(feeds_none, None)] + list(zip(feeds, st_refs)):
            @pl.when(jnp.logical_and(jnp.logical_and(j > 0, rows_match), feed))
            def _(with_sample=with_sample, st_ref=st_ref):
                multiply(with_sample, st_ref)
                stage()


def _matmul(x, xs, w_stack, layer, out_dtype, runs=None, residual=None, states=(), seq=None, name="matmul"):
    m, k = x.shape
    ms = xs.shape[0]
    tm = min(m // PARTS, 1024 if jnp.dtype(out_dtype).itemsize == 2 else 512)
    row_steps = m // tm
    col_block, nparts = _block_map(runs if runs is not None else ((0, w_stack.shape[-1]),))
    nblocks = nparts // PARTS
    steps_per_part = row_steps // PARTS
    assert nparts % PARTS == 0 and row_steps % PARTS == 0 and k % steps_per_part == 0

    def row_map(j, i):
        return jnp.where(j == 0, 0, i)

    def out_map(j, i):
        return (row_map(j, i), jnp.maximum(j - 1, 0))

    def w_map(j, i):
        return (layer, 0, col_block(PARTS * jnp.minimum(j, nblocks - 1) + i // steps_per_part))

    in_specs = [pl.BlockSpec((tm, k), lambda j, i: (row_map(j, i), 0)),
                pl.BlockSpec((ms, k), lambda j, i: (0, 0)),
                pl.BlockSpec((None, k, PART), w_map)]
    args = [x, xs, w_stack]
    if residual is not None:
        in_specs += [pl.BlockSpec((tm, WIDE), out_map),
                     pl.BlockSpec((ms, WIDE), lambda j, i: (0, jnp.maximum(j - 1, 0)))]
        args += list(residual)
    out_specs = [pl.BlockSpec((tm, WIDE), out_map),
                 pl.BlockSpec((ms, WIDE), lambda j, i: (0, jnp.maximum(j - 1, 0)))]
    out_shape = [jax.ShapeDtypeStruct((m, nblocks * WIDE), out_dtype),
                 jax.ShapeDtypeStruct((ms, nblocks * WIDE), F32)]

    heads = WIDE // HEAD_DIM
    steps_per_seq = (seq // tm) if states else 1
    fed_blocks = [b for _, _, k_block, v_block, _ in states for b in (k_block, v_block)]
    assert len(set(fed_blocks)) == len(fed_blocks)
    kernel_states, aliases, n_stacks = [], {}, 0
    for n, (stack, depth, k_block, v_block, rows) in enumerate(states):
        batch = m // seq
        assert seq % tm == 0 and (rows % tm == 0 or rows < tm) and k_block < v_block
        n_active = max(rows // tm, 1)
        first_active = steps_per_seq - n_active
        kernel_states.append((k_block, v_block, first_active))

        def st_map(j, i, kb=k_block + 1, vb=v_block + 1, first_active=first_active, n_active=n_active, batch=batch):
            s, b = i % steps_per_seq, i // steps_per_seq
            started = s >= first_active
            b_here = jnp.maximum(jnp.where(started, b, b - 1), 0)
            r_here = jnp.where(jnp.logical_and(jnp.logical_not(started), b > 0), n_active - 1,
                               jnp.maximum(s - first_active, 0))
            writing = jnp.logical_or(j == kb, j == vb)
            done = jnp.logical_or(jnp.logical_and(j > kb, j < vb), j > vb)
            b_idx = jnp.where(writing, b_here, jnp.where(done, batch - 1, 0))
            r_idx = jnp.where(writing, r_here, jnp.where(done, n_active - 1, 0))
            return (layer, b_idx, r_idx, jnp.where(j >= vb, 1, 0), 0, 0)

        out_specs.append(pl.BlockSpec((None, None, min(rows, tm), None, heads, HEAD_DIM), st_map))
        out_shape.append(jax.ShapeDtypeStruct((depth, batch, rows, 2, heads, HEAD_DIM), F32))
        if stack is not None:
            in_specs.append(pl.BlockSpec(memory_space=pl.ANY))
            aliases[len(args)] = 2 + n
            args.append(stack)
            n_stacks += 1
    return pl.pallas_call(
        functools.partial(_matmul_kernel, has_residual=residual is not None, row_steps=row_steps,
                          n_stacks=n_stacks, states=tuple(kernel_states), steps_per_seq=steps_per_seq),
        grid=(nblocks + 1, row_steps),
        in_specs=in_specs,
        out_specs=out_specs,
        out_shape=out_shape,
        input_output_aliases=aliases,
        scratch_shapes=[pltpu.VMEM((2, PARTS, k, PART), BF16)],
        compiler_params=_params("arbitrary", "arbitrary"),
        name=name,
    )(*args)


def _branch_kernel(ya_ref, yb_ref, yc_ref, ys_ref, wa_ref, wb_ref, wc_ref, ga_ref, gb_ref, gc_ref,
                   gsa_ref, gsb_ref, gsc_ref, o_ref, os_ref, wa_s, wb_s, wc_s, *, row_steps):
    j = pl.program_id(0)
    i = pl.program_id(1)
    steps_per_part = row_steps // PARTS
    staged = ((wa_ref, wa_s), (wb_ref, wb_s), (wc_ref, wc_s))

    def stage():
        for w_ref, w_s in staged:
            kq = w_ref.shape[0] // steps_per_part
            rows = pl.ds(pl.multiple_of((i % steps_per_part) * kq, kq), kq)
            w_s[j % 2, i // steps_per_part, rows, :] = w_ref[rows, :].astype(BF16)

    def merge(with_sample):
        tm = ya_ref.shape[0]
        lhs = [ya_ref[...], yb_ref[...], yc_ref[...]]
        targets = [(slice(0, tm), (ga_ref, gb_ref, gc_ref), o_ref)]
        if with_sample:
            ys = ys_ref[...].astype(BF16)
            bounds = (0, A_WIDTH, A_WIDTH + B_WIDTH, Y_WIDTH)
            lhs = [jnp.concatenate([y, ys[:, lo:hi]], axis=0) for y, lo, hi in zip(lhs, bounds, bounds[1:])]
            targets.append((slice(tm, None), (gsa_ref, gsb_ref, gsc_ref), os_ref))
        for part in range(PARTS):
            cols = slice(part * PART, (part + 1) * PART)
            products = [jnp.dot(y, w_s[(j + 1) % 2, part], preferred_element_type=F32)
                        for y, (_, w_s) in zip(lhs, staged)]
            for rows, g_refs, out_ref in targets:
                mixed = None
                for u, g_ref in zip(products, g_refs):
                    term = jax.nn.sigmoid(g_ref[:, cols].astype(F32)) * u[rows]
                    mixed = term if mixed is None else mixed + term
                out_ref[:, cols] = mixed.astype(out_ref.dtype)

    @pl.when(j == 0)
    def _():
        stage()

    @pl.when(jnp.logical_and(j > 0, i == 0))
    def _():
        merge(True)
        stage()

    @pl.when(jnp.logical_and(j > 0, i > 0))
    def _():
        merge(False)
        stage()


def _branch_merge(ya, yb, yc, y_s, w_a, w_b, w_c, main, main_s, g_col, layer, d_model):
    m = ya.shape[0]
    ms = y_s.shape[0]
    tm = min(m // PARTS, 512)
    row_steps = m // tm
    steps_per_part = row_steps // PARTS
    nblocks = d_model // WIDE
    assert row_steps % PARTS == 0 and g_col % WIDE == 0 and d_model % WIDE == 0
    assert all(w % steps_per_part == 0 for w in (A_WIDTH, B_WIDTH, C_WIDTH))

    def row_map(j, i):
        return jnp.where(j == 0, 0, i)

    def col_map(j):
        return jnp.maximum(j - 1, 0)

    def y_spec(width):
        return pl.BlockSpec((tm, width), lambda j, i: (row_map(j, i), 0))

    def w_spec(width):
        return pl.BlockSpec((None, width, PART),
                            lambda j, i: (layer, 0, PARTS * jnp.minimum(j, nblocks - 1) + i // steps_per_part))

    def g_spec(rows, idx, whole):
        first = (g_col + idx * d_model) // WIDE
        if whole:
            return pl.BlockSpec((rows, WIDE), lambda j, i: (0, first + col_map(j)))
        return pl.BlockSpec((rows, WIDE), lambda j, i: (row_map(j, i), first + col_map(j)))

    return pl.pallas_call(
        functools.partial(_branch_kernel, row_steps=row_steps),
        grid=(nblocks + 1, row_steps),
        in_specs=[y_spec(A_WIDTH), y_spec(B_WIDTH), y_spec(C_WIDTH),
                  pl.BlockSpec((ms, Y_WIDTH), lambda j, i: (0, 0)),
                  w_spec(A_WIDTH), w_spec(B_WIDTH), w_spec(C_WIDTH),
                  g_spec(tm, 0, False), g_spec(tm, 1, False), g_spec(tm, 2, False),
                  g_spec(ms, 0, True), g_spec(ms, 1, True), g_spec(ms, 2, True)],
        out_specs=[pl.BlockSpec((tm, WIDE), lambda j, i: (row_map(j, i), col_map(j))),
                   pl.BlockSpec((ms, WIDE), lambda j, i: (0, col_map(j)))],
        out_shape=[jax.ShapeDtypeStruct((m, d_model), BF16),
                   jax.ShapeDtypeStruct((ms, d_model), BF16)],
        scratch_shapes=[pltpu.VMEM((2, PARTS, w, PART), BF16) for w in (A_WIDTH, B_WIDTH, C_WIDTH)],
        compiler_params=_params("arbitrary", "arbitrary"),
        name="branch_merge",
    )(ya, yb, yc, y_s, w_a, w_b, w_c, main, main, main, main_s, main_s, main_s)


def _band_bias(slope_step, rows_per_tile=1):
    shape = (rows_per_tile * BAND, BAND)
    row = lax.broadcasted_iota(jnp.int32, shape, 0) % BAND
    col = lax.broadcasted_iota(jnp.int32, shape, 1)
    dist = (row - col).astype(F32)
    cur = jnp.where(col <= row, -slope_step * dist, NEG_BIG)
    prev = jnp.where(col >= row, -slope_step * (dist + float(BAND)), NEG_BIG)
    return cur, prev


def _block_rows(n):
    start = n * BAND if isinstance(n, int) else pl.multiple_of(n * BAND, BAND)
    return pl.ds(start, BAND)


def _qk(q, k):
    return lax.dot_general(q, k, (((1,), (1,)), ((), ())), preferred_element_type=F32) * SCALE


def _band_attention(blocks, sink=None, carry=None):
    logits = []
    for q, k_cur, _, bias_cur, k_prev, _, bias_prev in blocks:
        logit_p = None if k_prev is None else _qk(q, k_prev) + bias_prev
        logits.append((_qk(q, k_cur) + bias_cur, logit_p))
    probs = []
    for n, (logit_c, logit_p) in enumerate(logits):
        m = jnp.max(logit_c, axis=-1, keepdims=True)
        if logit_p is not None:
            m = jnp.maximum(m, jnp.max(logit_p, axis=-1, keepdims=True))
        if sink is not None:
            m = jnp.maximum(m, sink)
        m_tile = jnp.broadcast_to(m, logit_c.shape)
        alpha = None
        if carry is not None:
            m_tile = jnp.maximum(m_tile, carry[n][1])
            alpha = jnp.exp(carry[n][1] - m_tile)
        e_c = jnp.exp(logit_c - m_tile)
        den = jnp.sum(e_c, axis=-1, keepdims=True)
        e_p = None
        if logit_p is not None:
            e_p = jnp.exp(logit_p - m_tile)
            den = den + jnp.sum(e_p, axis=-1, keepdims=True)
            e_p = e_p.astype(BF16)
        if sink is not None:
            den = den + jnp.exp(sink - m)
        den_tile = jnp.broadcast_to(den, logit_c.shape)
        if carry is not None:
            den_tile = den_tile + carry[n][2] * alpha
        probs.append((e_c.astype(BF16), e_p, m_tile, den_tile, alpha))
    outs = []
    for n, ((_, _, v_cur, _, _, v_prev, _), (e_c, e_p, m, den, alpha)) in enumerate(zip(blocks, probs)):
        acc = jnp.dot(e_c, v_cur, preferred_element_type=F32)
        if e_p is not None:
            acc = acc + jnp.dot(e_p, v_prev, preferred_element_type=F32)
        if carry is not None:
            acc = acc + carry[n][0] * alpha
        outs.append((acc, m, den))
    return outs


def _attn_a_kernel(slopes_ref, sinks_ref, q_ref, k_ref, v_ref, z_ref, y_ref, bias_s, sink_s):
    kv = pl.program_id(1)
    rows = A_GROUP * BAND
    head = lax.broadcasted_iota(jnp.int32, (rows, 1), 0) // BAND
    slope_col = jnp.zeros((rows, 1), F32)
    sink_col = jnp.zeros((rows, 1), F32)
    for g in range(A_GROUP):
        slope_col = jnp.where(head == g, slopes_ref[kv * A_GROUP + g], slope_col)
        sink_col = jnp.where(head == g, sinks_ref[kv * A_GROUP + g], sink_col)
    bias_s[0], bias_s[1] = _band_bias(slope_col, A_GROUP)
    sink_s[...] = sink_col
    nb = q_ref.shape[0] // BAND

    def blocks(first, first_has_prev):
        rows_ = [_block_rows(first + j) for j in range(A_INFLIGHT)]
        kv_ = [(k_ref[rw, :], v_ref[rw, :]) for rw in rows_]
        if first_has_prev:
            before = _block_rows(first - 1)
            kv_before = (k_ref[before, :], v_ref[before, :])
        work = []
        for j, rw in enumerate(rows_):
            q = jnp.concatenate([q_ref[rw, g * HEAD_DIM:(g + 1) * HEAD_DIM] for g in range(A_GROUP)], axis=0)
            if j == 0 and not first_has_prev:
                work.append((q, *kv_[j], bias_s[0], None, None, None))
            else:
                work.append((q, *kv_[j], bias_s[0], *(kv_[j - 1] if j else kv_before), bias_s[1]))
        zs = [[z_ref[rw, g * HEAD_DIM:(g + 1) * HEAD_DIM].astype(F32) for g in range(A_GROUP)] for rw in rows_]
        outs = []
        for (acc, _, den), z in zip(_band_attention(work, sink=sink_s[...]), zs):
            o = acc / den
            outs.append([(o[g * BAND:(g + 1) * BAND] * _silu(z[g])).astype(y_ref.dtype) for g in range(A_GROUP)])
        for rw, ys in zip(rows_, outs):
            for g in range(A_GROUP):
                y_ref[rw, g * HEAD_DIM:(g + 1) * HEAD_DIM] = ys[g]

    blocks(0, False)

    def body(t, carry):
        blocks(t * A_INFLIGHT, True)
        return carry

    lax.fori_loop(1, nb // A_INFLIGHT, body, 0)


def _attn_a(main, slopes, sinks, main_runs, batch, seq):
    gw = A_GROUP * HEAD_DIM

    def spec(off, width):
        c = _compact(main_runs, off) // width
        return pl.BlockSpec((seq, width), lambda b, kv: (b, c + kv))

    return pl.pallas_call(
        _attn_a_kernel,
        grid=(batch, A_KV_HEADS),
        in_specs=[_smem(), _smem(), spec(OFF_QA, gw), spec(OFF_KA, HEAD_DIM), spec(OFF_VA, HEAD_DIM),
                  spec(OFF_ZA, gw)],
        out_specs=pl.BlockSpec((seq, gw), lambda b, kv: (b, kv)),
        out_shape=jax.ShapeDtypeStruct((batch * seq, A_WIDTH), BF16),
        scratch_shapes=[pltpu.VMEM((2, A_GROUP * BAND, BAND), F32), pltpu.VMEM((A_GROUP * BAND, 1), F32)],
        compiler_params=_params("arbitrary", "arbitrary"),
        name="attn_a",
    )(slopes, sinks, main, main, main, main)


def _attn_b_kernel(slopes_ref, q1, k1, v1, q2, k2, v2, q3, k3, v3, z_ref, y_ref, acc_s, m_s, l_s, bias_s):
    h = pl.program_id(1)
    seq = y_ref.shape[0]
    for g, (_, dil) in enumerate(B_GROUPS):
        bias_s[2 * g], bias_s[2 * g + 1] = _band_bias(slopes_ref[g * B_HEADS + h] * float(dil))

    def load_state(rows):
        return [(acc_s[rw, :], m_s[rw, :], l_s[rw, :]) for rw in rows]

    def store_state(rows, outs):
        for rw, (acc, m, den) in zip(rows, outs):
            acc_s[rw, :] = acc
            m_s[rw, :] = m
            l_s[rw, :] = den


    dil3 = B_GROUPS[2][1]
    assert seq // dil3 == BAND

    def g3_body(t, carry):
        rows = [pl.ds(t * B_INFLIGHT + j, BAND, stride=dil3) for j in range(B_INFLIGHT)]
        qkv = [tuple(a[rw, :].astype(BF16) for a in (q3, k3, v3)) for rw in rows]
        store_state(rows, _band_attention([(q, k, v, bias_s[4], None, None, None) for q, k, v in qkv]))
        return carry

    lax.fori_loop(0, dil3 // B_INFLIGHT, g3_body, 0)

    dil2 = B_GROUPS[1][1]
    nb2 = seq // dil2 // BAND
    per_body = max(1, B_INFLIGHT // nb2)

    def g2_body(t, carry):
        rows = [pl.ds(t * per_body + i + n * BAND * dil2, BAND, stride=dil2)
                for i in range(per_body) for n in range(nb2)]
        qkv = [tuple(a[rw, :].astype(BF16) for a in (q2, k2, v2)) for rw in rows]
        work = [(q, k, v, bias_s[2], None, None, None) if j % nb2 == 0 else
                (q, k, v, bias_s[2], qkv[j - 1][1], qkv[j - 1][2], bias_s[3]) for j, (q, k, v) in enumerate(qkv)]
        store_state(rows, _band_attention(work, carry=load_state(rows)))
        return carry

    lax.fori_loop(0, dil2 // per_body, g2_body, 0)

    def g1_blocks(first, first_has_prev):
        rows = [_block_rows(first + j) for j in range(B_INFLIGHT)]
        kv = [(k1[rw, :], v1[rw, :]) for rw in rows]
        if first_has_prev:
            before = _block_rows(first - 1)
            kv_before = (k1[before, :], v1[before, :])
        qs = [q1[rw, :] for rw in rows]
        zs = [z_ref[rw, :].astype(F32) for rw in rows]
        work = [(qs[j], *kv[j], bias_s[0], None, None, None) if j == 0 and not first_has_prev else
                (qs[j], *kv[j], bias_s[0], *(kv[j - 1] if j else kv_before), bias_s[1]) for j in range(B_INFLIGHT)]
        outs = [((acc / den) * _silu(z)).astype(y_ref.dtype)
                for (acc, _, den), z in zip(_band_attention(work, carry=load_state(rows)), zs)]
        for rw, y in zip(rows, outs):
            y_ref[rw, :] = y

    g1_blocks(0, False)

    def g1_body(t, carry):
        g1_blocks(t * B_INFLIGHT, True)
        return carry

    lax.fori_loop(1, seq // BAND // B_INFLIGHT, g1_body, 0)


def _attn_b(main, strided, slopes_b, main_runs, batch, seq):
    def spec(runs, off):
        c = _compact(runs, off) // HEAD_DIM
        return pl.BlockSpec((seq, HEAD_DIM), lambda b, h: (b, c + h))

    in_specs = [_smem()]
    args = [slopes_b]
    for g in range(len(B_GROUPS)):
        runs, arr = (main_runs, main) if g == 0 else (_STRIDED_RUNS, strided)
        in_specs += [spec(runs, off + g * B_WIDTH) for off in (OFF_QB, OFF_KB, OFF_VB)]
        args += [arr] * 3
    in_specs.append(spec(main_runs, OFF_ZB))
    args.append(main)
    return pl.pallas_call(
        _attn_b_kernel,
        grid=(batch, B_HEADS),
        in_specs=in_specs,
        out_specs=pl.BlockSpec((seq, HEAD_DIM), lambda b, h: (b, h)),
        out_shape=jax.ShapeDtypeStruct((batch * seq, B_WIDTH), BF16),
        scratch_shapes=[pltpu.VMEM((seq, HEAD_DIM), F32)] * 3 + [pltpu.VMEM((2 * len(B_GROUPS), BAND, BAND), F32)],
        compiler_params=_params("arbitrary", "arbitrary"),
        name="attn_b",
    )(*args)


def _head_norm_gate(o, w_row, z):
    mu = jnp.mean(o, axis=-1, keepdims=True)
    var = jnp.mean(jnp.square(o - mu), axis=-1, keepdims=True)
    y = (o - mu) * lax.rsqrt(var + NORM_EPS) * w_row
    return y * _silu(z)


def _retention_kernel(lg_ref, q_ref, k_ref, v_ref, z_ref, w_ref, y_ref, s_ref, decay_s):
    lg = lg_ref[pl.program_id(1)]
    row = lax.broadcasted_iota(jnp.int32, (C_CHUNK, C_CHUNK), 0)
    col = lax.broadcasted_iota(jnp.int32, (C_CHUNK, C_CHUNK), 1)
    diff = (row - col).astype(F32)
    decay_s[...] = jnp.where(diff >= 0, jnp.exp(lg * jnp.maximum(diff, 0.0)), 0.0)
    pos = lax.broadcasted_iota(jnp.int32, (C_CHUNK, 1), 0).astype(F32)
    q_dec = jnp.exp(lg * (pos + 1.0))
    k_dec = jnp.exp(lg * (float(C_CHUNK) - 1.0 - pos))
    chunk_decay = jnp.exp(lg * float(C_CHUNK))
    w_row = w_ref[...].astype(F32)

    def body(t, state):
        rows = [pl.ds(pl.multiple_of((t * C_INFLIGHT + j) * C_CHUNK, C_CHUNK), C_CHUNK) for j in range(C_INFLIGHT)]
        qs = [q_ref[rw, :].astype(F32) for rw in rows]
        ks = [k_ref[rw, :].astype(F32) * (C_QK_DIM ** -0.5) for rw in rows]
        vs = [v_ref[rw, :] for rw in rows]
        zs = [z_ref[rw, :].astype(F32) for rw in rows]
        attn = [lax.dot_general(q.astype(BF16), k.astype(BF16), (((1,), (1,)), ((), ())),
                                preferred_element_type=F32) for q, k in zip(qs, ks)]
        update = [jnp.dot(jnp.transpose(k * k_dec).astype(BF16), v, preferred_element_type=F32)
                  for k, v in zip(ks, vs)]
        intra = [jnp.dot((a * decay_s[...]).astype(BF16), v, preferred_element_type=F32)
                 for a, v in zip(attn, vs)]
        states = [state]
        for u in update:
            states.append(chunk_decay * states[-1] + u)
        cross = [jnp.dot((q * q_dec).astype(BF16), s0.astype(BF16), preferred_element_type=F32)
                 for q, s0 in zip(qs, states)]
        outs = [_head_norm_gate(i + c, w_row, z).astype(y_ref.dtype) for i, c, z in zip(intra, cross, zs)]
        for rw, y in zip(rows, outs):
            y_ref[rw, :] = y
        return states[-1]

    s_ref[...] = lax.fori_loop(0, q_ref.shape[0] // C_CHUNK // C_INFLIGHT, body,
                               jnp.zeros((C_QK_DIM, C_V_DIM), F32))


def _retention(main, log_gamma, ret_w_row, main_runs, batch, seq):
    def spec(off, width):
        c = _compact(main_runs, off) // width
        return pl.BlockSpec((seq, width), lambda b, h: (b, c + h))

    return pl.pallas_call(
        _retention_kernel,
        grid=(batch, C_HEADS),
        in_specs=[_smem(), spec(OFF_QC, C_QK_DIM), spec(OFF_KC, C_QK_DIM), spec(OFF_VC, C_V_DIM),
                  spec(OFF_ZC, C_V_DIM), pl.BlockSpec((1, C_V_DIM), lambda b, h: (0, h))],
        out_specs=[pl.BlockSpec((seq, C_V_DIM), lambda b, h: (b, h)),
                   pl.BlockSpec((None, None, C_QK_DIM, C_V_DIM), lambda b, h: (b, h, 0, 0))],
        out_shape=[jax.ShapeDtypeStruct((batch * seq, C_WIDTH), BF16),
                   jax.ShapeDtypeStruct((batch, C_HEADS, C_QK_DIM, C_V_DIM), F32)],
        scratch_shapes=[pltpu.VMEM((C_CHUNK, C_CHUNK), F32)],
        compiler_params=_params("arbitrary", "arbitrary"),
        name="retention",
    )(log_gamma, main, main, main, main, ret_w_row)


def _lane_sum(x):
    shape = x.shape
    flat = x.reshape(-1, shape[-1])
    hi = flat.astype(BF16)
    lo = (flat - hi.astype(F32)).astype(BF16)
    ones = jnp.ones((shape[-1], shape[-1]), BF16)
    total = jnp.dot(hi, ones, preferred_element_type=F32) + jnp.dot(lo, ones, preferred_element_type=F32)
    return total.reshape(shape)


def _sample_attend(q, k_buf, v_buf, k_new, v_new, slope, sink=None):
    steps_back = float(BAND) - lax.broadcasted_iota(jnp.int32, (1,) + k_buf.shape[1:], 1).astype(F32)
    logit_b = _lane_sum(k_buf * q) * SCALE - slope * steps_back
    logit_n = _lane_sum(k_new * q) * SCALE
    m = jnp.maximum(jnp.max(logit_b, axis=1, keepdims=True), logit_n)
    if sink is not None:
        m = jnp.maximum(m, sink)
    e_b = jnp.exp(logit_b - m)
    e_n = jnp.exp(logit_n - m)
    den = jnp.sum(e_b, axis=1, keepdims=True) + e_n
    if sink is not None:
        den = den + jnp.exp(sink - m)
    acc = jnp.sum(e_b * v_buf, axis=1, keepdims=True) + e_n * v_new
    return acc, m, den


def _column(row_vec):
    n = row_vec.shape[-1]
    eye = lax.broadcasted_iota(jnp.int32, (n, n), 0) == lax.broadcasted_iota(jnp.int32, (n, n), 1)
    return jnp.sum(jnp.where(eye, row_vec, 0.0), axis=-1, keepdims=True)


def _sample_kernel(lg_ref, sa_ref, sb_ref, sink_ref, hs_ref, p_ref, ca_ref, cb1_ref, cb2_ref, cb3_ref,
                   st_ref, w_ref, *rest):
    yab_ref, yc_ref, st_out = rest[-3:]

    def heads(off, count):
        u = off // HEAD_DIM
        return hs_ref[:, u:u + count, :][:, None]

    def per_query_head(x):
        return jnp.concatenate([jnp.broadcast_to(x[..., kv:kv + 1, :], x.shape[:-2] + (A_GROUP, x.shape[-1]))
                                for kv in range(A_KV_HEADS)], axis=-2)

    acc, _, den = _sample_attend(
        heads(OFF_QA, A_HEADS), per_query_head(ca_ref[:, :, 0]), per_query_head(ca_ref[:, :, 1]),
        per_query_head(heads(OFF_KA, A_KV_HEADS)), per_query_head(heads(OFF_VA, A_KV_HEADS)),
        sa_ref[...][None, None], sink_ref[...][None, None])
    yab_ref[:, :A_HEADS, :] = ((acc / den) * _silu(heads(OFF_ZA, A_HEADS)))[:, 0]

    parts = []
    for g, (cache, (_, dil)) in enumerate(zip((cb1_ref, cb2_ref, cb3_ref), B_GROUPS)):
        off = g * B_WIDTH
        parts.append(_sample_attend(
            heads(OFF_QB + off, B_HEADS), cache[:, :, 0], cache[:, :, 1],
            heads(OFF_KB + off, B_HEADS), heads(OFF_VB + off, B_HEADS),
            sb_ref[g * B_HEADS:(g + 1) * B_HEADS, :][None, None] * float(dil)))
    m_all = jnp.maximum(jnp.maximum(parts[0][1], parts[1][1]), parts[2][1])
    num = sum(acc * jnp.exp(m - m_all) for acc, m, _ in parts)
    den = sum(d * jnp.exp(m - m_all) for _, m, d in parts)
    yab_ref[:, A_HEADS:A_HEADS + B_HEADS, :] = ((num / den) * _silu(heads(OFF_ZB, B_HEADS)))[:, 0]

    for h in range(C_HEADS):
        gamma = jnp.exp(lg_ref[h])
        vo = h * C_V_DIM
        for i in range(SAMPLE_CHUNK):
            q = p_ref[i, :, OFF_QC + h * C_QK_DIM:OFF_QC + (h + 1) * C_QK_DIM]
            k = p_ref[i, :, OFF_KC + h * C_QK_DIM:OFF_KC + (h + 1) * C_QK_DIM] * (C_QK_DIM ** -0.5)
            v = p_ref[i, :, OFF_VC + vo:OFF_VC + vo + C_V_DIM]
            z = p_ref[i, :, OFF_ZC + vo:OFF_ZC + vo + C_V_DIM]
            s0 = st_ref[i, h]
            intra = jnp.sum(q * k, axis=-1, keepdims=True) * v
            cross = jnp.sum(_column(q * gamma) * s0, axis=0, keepdims=True)
            st_out[i, h] = gamma * s0 + _column(k) * v
            yc_ref[i, :, vo:vo + C_V_DIM] = _head_norm_gate(intra + cross, w_ref[:, vo:vo + C_V_DIM].astype(F32), z)


def _sample_mixers(flat, caches, state, state_stack, layer, log_gamma, slopes_a, slopes_b, sinks, ret_w_row):
    n, width = flat.shape
    nc = SAMPLE_CHUNK
    units = width // HEAD_DIM
    hs = flat.reshape(n, units, HEAD_DIM)
    p4 = flat.reshape(n // nc, nc, 1, width)
    depth = caches[0].shape[0]
    views = [caches[0]]
    for c, (win, dil) in zip(caches[1:], B_GROUPS):
        views.append(c.reshape(depth, n, BAND, dil, 2, B_HEADS, HEAD_DIM))

    def lanes(vec):
        return jnp.broadcast_to(vec.astype(F32)[:, None], (vec.shape[0], HEAD_DIM))

    def cache_b_spec():
        return pl.BlockSpec((None, nc, BAND, None, 2, B_HEADS, HEAD_DIM), lambda i: (layer, i, 0, 0, 0, 0, 0))

    args = [log_gamma, lanes(slopes_a), lanes(slopes_b), lanes(sinks), hs, p4, *views, state, ret_w_row]
    in_specs = [_smem(),
                pl.BlockSpec((A_HEADS, HEAD_DIM), lambda i: (0, 0)),
                pl.BlockSpec((len(B_GROUPS) * B_HEADS, HEAD_DIM), lambda i: (0, 0)),
                pl.BlockSpec((A_HEADS, HEAD_DIM), lambda i: (0, 0)),
                pl.BlockSpec((nc, units, HEAD_DIM), lambda i: (i, 0, 0)),
                pl.BlockSpec((None, nc, 1, width), lambda i: (i, 0, 0, 0)),
                pl.BlockSpec((None, nc, BAND, 2, A_KV_HEADS, HEAD_DIM), lambda i: (layer, i, 0, 0, 0, 0)),
                cache_b_spec(), cache_b_spec(), cache_b_spec(),
                pl.BlockSpec((None, nc, C_HEADS, C_QK_DIM, C_V_DIM), lambda i: (layer, i, 0, 0, 0)),
                pl.BlockSpec((1, C_WIDTH), lambda i: (0, 0))]
    aliases = {}
    if state_stack is not None:
        in_specs.append(pl.BlockSpec(memory_space=pl.ANY))
        aliases = {len(args): 2}
        args.append(state_stack)
    yab, yc, st = pl.pallas_call(
        _sample_kernel,
        grid=(n // nc,),
        in_specs=in_specs,
        out_specs=[pl.BlockSpec((nc, A_HEADS + B_HEADS, HEAD_DIM), lambda i: (i, 0, 0)),
                   pl.BlockSpec((None, nc, 1, C_WIDTH), lambda i: (i, 0, 0, 0)),
                   pl.BlockSpec((None, nc, C_HEADS, C_QK_DIM, C_V_DIM), lambda i: (layer, i, 0, 0, 0))],
        out_shape=[jax.ShapeDtypeStruct((n, A_HEADS + B_HEADS, HEAD_DIM), F32),
                   jax.ShapeDtypeStruct((n // nc, nc, 1, C_WIDTH), F32),
                   jax.ShapeDtypeStruct((depth, n, C_HEADS, C_QK_DIM, C_V_DIM), F32)],
        input_output_aliases=aliases,
        compiler_params=_params("arbitrary"),
        name="sample_mixers",
    )(*args)
    y = jnp.concatenate([yab.reshape(n, A_WIDTH + B_WIDTH), yc.reshape(n, C_WIDTH)], axis=-1)
    return y, st


def _kv_rows(k_arr, k_col, v_arr, v_col, batch, heads, rows):
    w = heads * HEAD_DIM
    t = k_arr.shape[0] // batch
    k = k_arr.reshape(batch, t, -1)[:, t - rows:, k_col:k_col + w]
    v = v_arr.reshape(batch, t, -1)[:, t - rows:, v_col:v_col + w]
    return jnp.stack([k, v], axis=2).reshape(batch, rows, 2, heads, HEAD_DIM).astype(F32)


def _window_rows(main, strided, main_runs, batch):
    t = main.shape[0] // batch
    out = [_kv_rows(main, _compact(main_runs, OFF_KA), main, _compact(main_runs, OFF_VA), batch,
                    A_KV_HEADS, min(BAND, t))]
    for g, (win, _) in enumerate(B_GROUPS):
        runs, arr = (main_runs, main) if g == 0 else (_STRIDED_RUNS, strided)
        out.append(_kv_rows(arr, _compact(runs, OFF_KB + g * B_WIDTH), arr, _compact(runs, OFF_VB + g * B_WIDTH),
                            batch, B_HEADS, min(win, t)))
    return out


def _kv_rows_kernel(*refs, pieces):
    k_refs, v_refs, o_ref = refs[:pieces], refs[pieces:2 * pieces], refs[-1]
    rows, _, heads, dh = o_ref.shape
    for slot, part_refs in enumerate((k_refs, v_refs)):
        flat = jnp.concatenate([r[...].astype(F32) for r in part_refs], axis=-1)
        o_ref[:, slot] = flat.reshape(rows, heads, dh)


def _kv_rows_into(stack, layer, depth, k_arr, k_col, v_arr, v_col, batch, heads, rows):
    w = heads * HEAD_DIM
    pw = min(w, TN)
    pieces = w // pw
    t = k_arr.shape[0] // batch
    tr = min(rows, 512)
    first, per_seq = (t - rows) // tr, t // tr

    def piece(col):
        return pl.BlockSpec((tr, pw), lambda b, i: (b * per_seq + first + i, col // pw))

    in_specs = ([piece(k_col + p * pw) for p in range(pieces)] + [piece(v_col + p * pw) for p in range(pieces)])
    args = [k_arr] * pieces + [v_arr] * pieces
    aliases = {}
    if stack is not None:
        in_specs.append(pl.BlockSpec(memory_space=pl.ANY))
        args.append(stack)
        aliases = {2 * pieces: 0}
    return pl.pallas_call(
        functools.partial(_kv_rows_kernel, pieces=pieces),
        grid=(batch, rows // tr),
        in_specs=in_specs,
        out_specs=pl.BlockSpec((None, None, tr, 2, heads, HEAD_DIM), lambda b, i: (layer, b, i, 0, 0, 0)),
        out_shape=jax.ShapeDtypeStruct((depth, batch, rows, 2, heads, HEAD_DIM), F32),
        input_output_aliases=aliases,
        compiler_params=_params("arbitrary", "arbitrary"),
        name="kv_rows",
    )(*args)


def _window_rows_into(stacks, layer, depth, main, main_runs, batch):
    t = main.shape[0] // batch
    stacks = stacks or [None, None]
    return [_kv_rows_into(stacks[0], layer, depth, main, _compact(main_runs, OFF_KA), main,
                          _compact(main_runs, OFF_VA), batch, A_KV_HEADS, min(BAND, t)),
            _kv_rows_into(stacks[1], layer, depth, main, _compact(main_runs, OFF_KB), main,
                          _compact(main_runs, OFF_VB), batch, B_HEADS, min(B_GROUPS[0][0], t))]


def _reference_order(main, strided, main_runs, upto):
    pieces = []
    for (lo, hi), src in sorted([(r, main) for r in main_runs] + [(r, strided) for r in _STRIDED_RUNS]):
        if lo >= upto:
            break
        runs = main_runs if src is main else _STRIDED_RUNS
        c = _compact(runs, lo)
        pieces.append(src[:, c:c + min(hi, upto) - lo])
    return jnp.concatenate(pieces, axis=-1)


def kernel(x_prompt, x_sample, cache_a, cache_b1, cache_b2, cache_b3, state_c, norm_w, w_in,
           w_branch_a, w_branch_b, w_branch_c, w_out, attn_sinks, ret_norm_w, final_norm_w):
    batch, seq, d_model = x_prompt.shape
    n_s, t_s, _ = x_sample.shape
    depth = w_in.shape[0]
    width = w_in.shape[-1]
    assert t_s == 1 and seq == B_GROUPS[-1][1] * BAND and n_s % SAMPLE_CHUNK == 0
    assert width == OFF_G + 3 * d_model and d_model % TN == 0
    assert cache_a.shape[2] == BAND
    assert all(c.shape[2] == win for c, (win, _) in zip((cache_b1, cache_b2, cache_b3), B_GROUPS))

    slopes_a = jnp.exp2(-8.0 * jnp.arange(1, A_HEADS + 1, dtype=F32) / A_HEADS)
    nbh = len(B_GROUPS) * B_HEADS
    slopes_b = jnp.exp2(-8.0 * jnp.arange(1, nbh + 1, dtype=F32) / nbh)
    log_gamma = jnp.log1p(-jnp.exp2(-5.0 - jnp.arange(C_HEADS, dtype=F32)))
    caches = (cache_a, cache_b1, cache_b2, cache_b3)
    main_runs = _main_runs(width)
    g_col = _compact(main_runs, OFF_G)

    xp = x_prompt.reshape(batch * seq, d_model)
    xs = x_sample.reshape(n_s, d_model)
    p_window, p_c, s_states, s_c = None, [], [], None
    for l in range(depth):
        nw = norm_w[l].reshape(1, d_model)
        rw = ret_norm_w[l].reshape(1, C_WIDTH)
        sinks = attn_sinks[l]

        hp = _rmsnorm(xp, nw, BF16)
        hsm = _rmsnorm(xs, nw, BF16)
        main, main_s = _matmul(hp, hsm, w_in, l, BF16, runs=main_runs, name="proj_main")
        strided_states = tuple(
            (None if p_window is None else p_window[1 + g], depth,
             _compact(_STRIDED_RUNS, OFF_KB + g * B_WIDTH) // WIDE, _compact(_STRIDED_RUNS, OFF_VB + g * B_WIDTH) // WIDE,
             min(B_GROUPS[g][0], seq)) for g in (1, 2))
        strided, strided_s, *strided_stacks = _matmul(hp, hsm, w_in, l, F32, runs=_STRIDED_RUNS,
                                                      states=strided_states, seq=seq, name="proj_strided")

        ya = _attn_a(main, slopes_a, sinks, main_runs, batch, seq)
        yb = _attn_b(main, strided, slopes_b, main_runs, batch, seq)
        yc, new_c = _retention(main, log_gamma, rw, main_runs, batch, seq)
        y_s, s_c = _sample_mixers(_reference_order(main_s, strided_s, main_runs, OFF_G), caches, state_c, s_c, l,
                                  log_gamma, slopes_a, slopes_b, sinks, rw)

        mixed, mixed_s = _branch_merge(ya, yb, yc, y_s, w_branch_a, w_branch_b, w_branch_c,
                                       main, main_s, g_col, l, d_model)
        xp, xs = _matmul(mixed, mixed_s, w_out, l, F32, residual=(xp, xs), name="proj_out")
        p_window = _window_rows_into(p_window, l, depth, main, main_runs, batch) + strided_stacks
        p_c.append(new_c)
        s_states.append(_window_rows(main_s, strided_s, main_runs, n_s))

    fw = final_norm_w.reshape(1, d_model)
    y_prompt = _rmsnorm(xp, fw, F32).reshape(batch, seq, d_model)
    y_sample = _rmsnorm(xs, fw, F32).reshape(n_s, 1, d_model)
    sample = [jnp.stack([s[i] for s in s_states]) for i in range(4)]
    return (y_prompt, y_sample, *p_window, jnp.stack(p_c), *sample, s_c)
```

```python
import functools

import jax
import jax.numpy as jnp
from jax import lax
from jax.experimental import pallas as pl
from jax.experimental.pallas import tpu as pltpu

F32 = jnp.float32
BF16 = jnp.bfloat16

HEAD_DIM = 128
BAND = 128
NORM_EPS = 1e-6
A_HEADS = 8
A_KV_HEADS = 2
A_GROUP = A_HEADS // A_KV_HEADS
B_GROUPS = ((128, 1), (512, 4), (2048, 16))
B_HEADS = 8
C_HEADS = 6
C_QK_DIM = 128
C_V_DIM = 256
C_CHUNK = 128

A_WIDTH = A_HEADS * HEAD_DIM
A_KV_WIDTH = A_KV_HEADS * HEAD_DIM
B_WIDTH = B_HEADS * HEAD_DIM
B_QKV_WIDTH = len(B_GROUPS) * B_WIDTH
C_QK_WIDTH = C_HEADS * C_QK_DIM
C_WIDTH = C_HEADS * C_V_DIM
Y_WIDTH = A_WIDTH + B_WIDTH + C_WIDTH

OFF_QA = 0
OFF_KA = OFF_QA + A_WIDTH
OFF_VA = OFF_KA + A_KV_WIDTH
OFF_ZA = OFF_VA + A_KV_WIDTH
OFF_QB = OFF_ZA + A_WIDTH
OFF_KB = OFF_QB + B_QKV_WIDTH
OFF_VB = OFF_KB + B_QKV_WIDTH
OFF_ZB = OFF_VB + B_QKV_WIDTH
OFF_QC = OFF_ZB + B_WIDTH
OFF_KC = OFF_QC + C_QK_WIDTH
OFF_VC = OFF_KC + C_QK_WIDTH
OFF_ZC = OFF_VC + C_WIDTH
OFF_G = OFF_ZC + C_WIDTH

TN = 512
PART = 256
PARTS = 4
WIDE = PARTS * PART
NEG_BIG = -1e30
SCALE = HEAD_DIM ** -0.5
assert BAND == HEAD_DIM
VMEM_LIMIT = 56 * 1024 * 1024
SAMPLE_CHUNK = 2
B_INFLIGHT = 8
A_INFLIGHT = 4
C_INFLIGHT = 8

_STRIDED_RUNS = tuple((off + B_WIDTH, off + B_QKV_WIDTH) for off in (OFF_QB, OFF_KB, OFF_VB))


def _main_runs(width):
    edges = [0] + [e for run in _STRIDED_RUNS for e in run] + [width]
    return tuple((edges[i], edges[i + 1]) for i in range(0, len(edges), 2))


def _compact(runs, col):
    base = 0
    for lo, hi in runs:
        if lo <= col < hi:
            return base + col - lo
        base += hi - lo
    raise ValueError(col)


def _block_map(runs):
    starts, gaps, pos, prev_hi = [], [], 0, 0
    for lo, hi in runs:
        starts.append(pos // PART)
        gaps.append((lo - prev_hi) // PART)
        pos += hi - lo
        prev_hi = hi

    def col_block(j):
        out = j
        for s, g in zip(starts, gaps):
            if g:
                out = out + jnp.where(j >= s, g, 0)
        return out

    return col_block, pos // PART


def _params(*sem):
    return pltpu.CompilerParams(dimension_semantics=sem, vmem_limit_bytes=VMEM_LIMIT)


def _smem():
    return pl.BlockSpec(memory_space=pltpu.SMEM)


def _silu(z):
    return z * jax.nn.sigmoid(z)


def _rmsnorm_kernel(x_ref, w_ref, o_ref):
    x = x_ref[...].astype(F32)
    ms = jnp.mean(x * x, axis=-1, keepdims=True)
    o_ref[...] = (x * lax.rsqrt(ms + NORM_EPS) * w_ref[...].astype(F32)).astype(o_ref.dtype)


def _rmsnorm(x, w_row, out_dtype):
    m, d = x.shape
    tr = min(m, 512)
    return pl.pallas_call(
        _rmsnorm_kernel,
        grid=(m // tr,),
        in_specs=[pl.BlockSpec((tr, d), lambda i: (i, 0)),
                  pl.BlockSpec((1, d), lambda i: (0, 0))],
        out_specs=pl.BlockSpec((tr, d), lambda i: (i, 0)),
        out_shape=jax.ShapeDtypeStruct((m, d), out_dtype),
        compiler_params=_params("arbitrary"),
        name="rmsnorm",
    )(x, w_row)


def _matmul_kernel(x_ref, xs_ref, w_ref, *rest, has_residual, row_steps, n_stacks, states, steps_per_seq):
    rest = list(rest)
    r_ref, rs_ref = (rest.pop(0), rest.pop(0)) if has_residual else (None, None)
    del rest[:n_stacks]
    o_ref, os_ref, *st_refs, wb_ref = rest
    j = pl.program_id(0)
    i = pl.program_id(1)
    steps_per_part = row_steps // PARTS
    kq = w_ref.shape[0] // steps_per_part
    tm = x_ref.shape[0]

    def stage():
        rows = pl.ds(pl.multiple_of((i % steps_per_part) * kq, kq), kq)
        wb_ref[j % 2, i // steps_per_part, rows, :] = w_ref[rows, :].astype(BF16)

    def multiply(with_sample, st_ref):
        lhs = x_ref[...]
        targets = [(slice(0, tm), r_ref, o_ref)]
        if with_sample:
            lhs = jnp.concatenate([lhs, xs_ref[...]], axis=0)
            targets.append((slice(tm, None), rs_ref, os_ref))
        for part in range(PARTS):
            cols = slice(part * PART, (part + 1) * PART)
            acc = jnp.dot(lhs, wb_ref[(j + 1) % 2, part], preferred_element_type=F32)
            for rows, res_ref, out_ref in targets:
                out = acc[rows]
                if has_residual:
                    out = out + res_ref[:, cols]
                out_ref[:, cols] = out.astype(out_ref.dtype)
            if st_ref is not None:
                tr, _, dh = st_ref.shape
                hp = PART // dh
                st_ref[:, part * hp:(part + 1) * hp, :] = acc[tm - tr:tm].reshape(tr, hp, dh)

    @pl.when(j == 0)
    def _():
        stage()

    feeds = [jnp.logical_and(jnp.logical_or(j == k_block + 1, j == v_block + 1),
                             i % steps_per_seq >= first_active) for k_block, v_block, first_active in states]
    feeds_none = functools.reduce(jnp.logical_and, [jnp.logical_not(f) for f in feeds], j > 0)
    for with_sample in (True, False):
        rows_match = (i == 0) if with_sample else (i > 0)
        for feed, st_ref in [(feeds_none, None)] + list(zip(feeds, st_refs)):
            @pl.when(jnp.logical_and(jnp.logical_and(j > 0, rows_match), feed))
            def _(with_sample=with_sample, st_ref=st_ref):
                multiply(with_sample, st_ref)
                stage()


def _matmul(x, xs, w_stack, layer, out_dtype, runs=None, residual=None, states=(), seq=None, name="matmul"):
    m, k = x.shape
    ms = xs.shape[0]
    tm = min(m // PARTS, 1024 if jnp.dtype(out_dtype).itemsize == 2 else 512)
    row_steps = m // tm
    col_block, nparts = _block_map(runs if runs is not None else ((0, w_stack.shape[-1]),))
    nblocks = nparts // PARTS
    steps_per_part = row_steps // PARTS
    assert nparts % PARTS == 0 and row_steps % PARTS == 0 and k % steps_per_part == 0

    def row_map(j, i):
        return jnp.where(j == 0, 0, i)

    def out_map(j, i):
        return (row_map(j, i), jnp.maximum(j - 1, 0))

    def w_map(j, i):
        return (layer, 0, col_block(PARTS * jnp.minimum(j, nblocks - 1) + i // steps_per_part))

    in_specs = [pl.BlockSpec((tm, k), lambda j, i: (row_map(j, i), 0)),
                pl.BlockSpec((ms, k), lambda j, i: (0, 0)),
                pl.BlockSpec((None, k, PART), w_map)]
    args = [x, xs, w_stack]
    if residual is not None:
        in_specs += [pl.BlockSpec((tm, WIDE), out_map),
                     pl.BlockSpec((ms, WIDE), lambda j, i: (0, jnp.maximum(j - 1, 0)))]
        args += list(residual)
    out_specs = [pl.BlockSpec((tm, WIDE), out_map),
                 pl.BlockSpec((ms, WIDE), lambda j, i: (0, jnp.maximum(j - 1, 0)))]
    out_shape = [jax.ShapeDtypeStruct((m, nblocks * WIDE), out_dtype),
                 jax.ShapeDtypeStruct((ms, nblocks * WIDE), F32)]

    heads = WIDE // HEAD_DIM
    steps_per_seq = (seq // tm) if states else 1
    fed_blocks = [b for _, _, k_block, v_block, _ in states for b in (k_block, v_block)]
    assert len(set(fed_blocks)) == len(fed_blocks)
    assert not states or (residual is None and out_dtype == F32)
    kernel_states, aliases, n_stacks = [], {}, 0
    for n, (stack, depth, k_block, v_block, rows) in enumerate(states):
        batch = m // seq
        assert seq % tm == 0 and (rows % tm == 0 or rows < tm) and k_block < v_block
        n_active = max(rows // tm, 1)
        first_active = steps_per_seq - n_active
        kernel_states.append((k_block, v_block, first_active))

        def st_map(j, i, kb=k_block + 1, vb=v_block + 1, first_active=first_active, n_active=n_active, batch=batch):
            s, b = i % steps_per_seq, i // steps_per_seq
            started = s >= first_active
            b_here = jnp.maximum(jnp.where(started, b, b - 1), 0)
            r_here = jnp.where(jnp.logical_and(jnp.logical_not(started), b > 0), n_active - 1,
                               jnp.maximum(s - first_active, 0))
            writing = jnp.logical_or(j == kb, j == vb)
            done = jnp.logical_or(jnp.logical_and(j > kb, j < vb), j > vb)
            b_idx = jnp.where(writing, b_here, jnp.where(done, batch - 1, 0))
            r_idx = jnp.where(writing, r_here, jnp.where(done, n_active - 1, 0))
            return (layer, b_idx, r_idx, jnp.where(j >= vb, 1, 0), 0, 0)

        out_specs.append(pl.BlockSpec((None, None, min(rows, tm), None, heads, HEAD_DIM), st_map))
        out_shape.append(jax.ShapeDtypeStruct((depth, batch, rows, 2, heads, HEAD_DIM), F32))
        if stack is not None:
            in_specs.append(pl.BlockSpec(memory_space=pl.ANY))
            aliases[len(args)] = 2 + n
            args.append(stack)
            n_stacks += 1
    return pl.pallas_call(
        functools.partial(_matmul_kernel, has_residual=residual is not None, row_steps=row_steps,
                          n_stacks=n_stacks, states=tuple(kernel_states), steps_per_seq=steps_per_seq),
        grid=(nblocks + 1, row_steps),
        in_specs=in_specs,
        out_specs=out_specs,
        out_shape=out_shape,
        input_output_aliases=aliases,
        scratch_shapes=[pltpu.VMEM((2, PARTS, k, PART), BF16)],
        compiler_params=_params("arbitrary", "arbitrary"),
        name=name,
    )(*args)


def _branch_kernel(ya_ref, yb_ref, yc_ref, ys_ref, wa_ref, wb_ref, wc_ref, ga_ref, gb_ref, gc_ref,
                   gsa_ref, gsb_ref, gsc_ref, o_ref, os_ref, wa_s, wb_s, wc_s, *, row_steps):
    j = pl.program_id(0)
    i = pl.program_id(1)
    steps_per_part = row_steps // PARTS
    staged = ((wa_ref, wa_s), (wb_ref, wb_s), (wc_ref, wc_s))

    def stage():
        for w_ref, w_s in staged:
            kq = w_ref.shape[0] // steps_per_part
            rows = pl.ds(pl.multiple_of((i % steps_per_part) * kq, kq), kq)
            w_s[j % 2, i // steps_per_part, rows, :] = w_ref[rows, :].astype(BF16)

    def merge(with_sample):
        tm = ya_ref.shape[0]
        lhs = [ya_ref[...], yb_ref[...], yc_ref[...]]
        targets = [(slice(0, tm), (ga_ref, gb_ref, gc_ref), o_ref)]
        if with_sample:
            ys = ys_ref[...].astype(BF16)
            bounds = (0, A_WIDTH, A_WIDTH + B_WIDTH, Y_WIDTH)
            lhs = [jnp.concatenate([y, ys[:, lo:hi]], axis=0) for y, lo, hi in zip(lhs, bounds, bounds[1:])]
            targets.append((slice(tm, None), (gsa_ref, gsb_ref, gsc_ref), os_ref))
        for part in range(PARTS):
            cols = slice(part * PART, (part + 1) * PART)
            products = [jnp.dot(y, w_s[(j + 1) % 2, part], preferred_element_type=F32)
                        for y, (_, w_s) in zip(lhs, staged)]
            for rows, g_refs, out_ref in targets:
                mixed = None
                for u, g_ref in zip(products, g_refs):
                    term = jax.nn.sigmoid(g_ref[:, cols].astype(F32)) * u[rows]
                    mixed = term if mixed is None else mixed + term
                out_ref[:, cols] = mixed.astype(out_ref.dtype)

    @pl.when(j == 0)
    def _():
        stage()

    @pl.when(jnp.logical_and(j > 0, i == 0))
    def _():
        merge(True)
        stage()

    @pl.when(jnp.logical_and(j > 0, i > 0))
    def _():
        merge(False)
        stage()


def _branch_merge(ya, yb, yc, y_s, w_a, w_b, w_c, main, main_s, g_col, layer, d_model):
    m = ya.shape[0]
    ms = y_s.shape[0]
    tm = min(m // PARTS, 512)
    row_steps = m // tm
    steps_per_part = row_steps // PARTS
    nblocks = d_model // WIDE
    assert row_steps % PARTS == 0 and g_col % WIDE == 0 and d_model % WIDE == 0
    assert all(w % steps_per_part == 0 for w in (A_WIDTH, B_WIDTH, C_WIDTH))

    def row_map(j, i):
        return jnp.where(j == 0, 0, i)

    def col_map(j):
        return jnp.maximum(j - 1, 0)

    def y_spec(width):
        return pl.BlockSpec((tm, width), lambda j, i: (row_map(j, i), 0))

    def w_spec(width):
        return pl.BlockSpec((None, width, PART),
                            lambda j, i: (layer, 0, PARTS * jnp.minimum(j, nblocks - 1) + i // steps_per_part))

    def g_spec(rows, idx, whole):
        first = (g_col + idx * d_model) // WIDE
        if whole:
            return pl.BlockSpec((rows, WIDE), lambda j, i: (0, first + col_map(j)))
        return pl.BlockSpec((rows, WIDE), lambda j, i: (row_map(j, i), first + col_map(j)))

    return pl.pallas_call(
        functools.partial(_branch_kernel, row_steps=row_steps),
        grid=(nblocks + 1, row_steps),
        in_specs=[y_spec(A_WIDTH), y_spec(B_WIDTH), y_spec(C_WIDTH),
                  pl.BlockSpec((ms, Y_WIDTH), lambda j, i: (0, 0)),
                  w_spec(A_WIDTH), w_spec(B_WIDTH), w_spec(C_WIDTH),
                  g_spec(tm, 0, False), g_spec(tm, 1, False), g_spec(tm, 2, False),
                  g_spec(ms, 0, True), g_spec(ms, 1, True), g_spec(ms, 2, True)],
        out_specs=[pl.BlockSpec((tm, WIDE), lambda j, i: (row_map(j, i), col_map(j))),
                   pl.BlockSpec((ms, WIDE), lambda j, i: (0, col_map(j)))],
        out_shape=[jax.ShapeDtypeStruct((m, d_model), BF16),
                   jax.ShapeDtypeStruct((ms, d_model), BF16)],
        scratch_shapes=[pltpu.VMEM((2, PARTS, w, PART), BF16) for w in (A_WIDTH, B_WIDTH, C_WIDTH)],
        compiler_params=_params("arbitrary", "arbitrary"),
        name="branch_merge",
    )(ya, yb, yc, y_s, w_a, w_b, w_c, main, main, main, main_s, main_s, main_s)


def _band_bias(slope_step, rows_per_tile=1):
    shape = (rows_per_tile * BAND, BAND)
    row = lax.broadcasted_iota(jnp.int32, shape, 0) % BAND
    col = lax.broadcasted_iota(jnp.int32, shape, 1)
    dist = (row - col).astype(F32)
    cur = jnp.where(col <= row, -slope_step * dist, NEG_BIG)
    prev = jnp.where(col >= row, -slope_step * (dist + float(BAND)), NEG_BIG)
    return cur, prev


def _block_rows(n):
    start = n * BAND if isinstance(n, int) else pl.multiple_of(n * BAND, BAND)
    return pl.ds(start, BAND)


def _qk(q, k):
    return lax.dot_general(q, k, (((1,), (1,)), ((), ())), preferred_element_type=F32) * SCALE


def _band_attention(blocks, sink=None, carry=None):
    logits = []
    for q, k_cur, _, bias_cur, k_prev, _, bias_prev in blocks:
        logit_p = None if k_prev is None else _qk(q, k_prev) + bias_prev
        logits.append((_qk(q, k_cur) + bias_cur, logit_p))
    probs = []
    for n, (logit_c, logit_p) in enumerate(logits):
        m = jnp.max(logit_c, axis=-1, keepdims=True)
        if logit_p is not None:
            m = jnp.maximum(m, jnp.max(logit_p, axis=-1, keepdims=True))
        if sink is not None:
            m = jnp.maximum(m, sink)
        m_tile = jnp.broadcast_to(m, logit_c.shape)
        alpha = None
        if carry is not None:
            m_tile = jnp.maximum(m_tile, carry[n][1])
            alpha = jnp.exp(carry[n][1] - m_tile)
        e_c = jnp.exp(logit_c - m_tile)
        den = jnp.sum(e_c, axis=-1, keepdims=True)
        e_p = None
        if logit_p is not None:
            e_p = jnp.exp(logit_p - m_tile)
            den = den + jnp.sum(e_p, axis=-1, keepdims=True)
            e_p = e_p.astype(BF16)
        if sink is not None:
            den = den + jnp.exp(sink - m)
        den_tile = jnp.broadcast_to(den, logit_c.shape)
        if carry is not None:
            den_tile = den_tile + carry[n][2] * alpha
        probs.append((e_c.astype(BF16), e_p, m_tile, den_tile, alpha))
    outs = []
    for n, ((_, _, v_cur, _, _, v_prev, _), (e_c, e_p, m, den, alpha)) in enumerate(zip(blocks, probs)):
        acc = jnp.dot(e_c, v_cur, preferred_element_type=F32)
        if e_p is not None:
            acc = acc + jnp.dot(e_p, v_prev, preferred_element_type=F32)
        if carry is not None:
            acc = acc + carry[n][0] * alpha
        outs.append((acc, m, den))
    return outs


def _attn_a_kernel(slopes_ref, sinks_ref, q_ref, k_ref, v_ref, z_ref, y_ref, bias_s, sink_s):
    kv = pl.program_id(1)
    rows = A_GROUP * BAND
    head = lax.broadcasted_iota(jnp.int32, (rows, 1), 0) // BAND
    slope_col = jnp.zeros((rows, 1), F32)
    sink_col = jnp.zeros((rows, 1), F32)
    for g in range(A_GROUP):
        slope_col = jnp.where(head == g, slopes_ref[kv * A_GROUP + g], slope_col)
        sink_col = jnp.where(head == g, sinks_ref[kv * A_GROUP + g], sink_col)
    bias_s[0], bias_s[1] = _band_bias(slope_col, A_GROUP)
    sink_s[...] = sink_col
    nb = q_ref.shape[0] // BAND

    def blocks(first, first_has_prev):
        rows_ = [_block_rows(first + j) for j in range(A_INFLIGHT)]
        kv_ = [(k_ref[rw, :], v_ref[rw, :]) for rw in rows_]
        if first_has_prev:
            before = _block_rows(first - 1)
            kv_before = (k_ref[before, :], v_ref[before, :])
        work = []
        for j, rw in enumerate(rows_):
            q = jnp.concatenate([q_ref[rw, g * HEAD_DIM:(g + 1) * HEAD_DIM] for g in range(A_GROUP)], axis=0)
            if j == 0 and not first_has_prev:
                work.append((q, *kv_[j], bias_s[0], None, None, None))
            else:
                work.append((q, *kv_[j], bias_s[0], *(kv_[j - 1] if j else kv_before), bias_s[1]))
        zs = [[z_ref[rw, g * HEAD_DIM:(g + 1) * HEAD_DIM].astype(F32) for g in range(A_GROUP)] for rw in rows_]
        outs = []
        for (acc, _, den), z in zip(_band_attention(work, sink=sink_s[...]), zs):
            o = acc / den
            outs.append([(o[g * BAND:(g + 1) * BAND] * _silu(z[g])).astype(y_ref.dtype) for g in range(A_GROUP)])
        for rw, ys in zip(rows_, outs):
            for g in range(A_GROUP):
                y_ref[rw, g * HEAD_DIM:(g + 1) * HEAD_DIM] = ys[g]

    blocks(0, False)

    def body(t, carry):
        blocks(t * A_INFLIGHT, True)
        return carry

    lax.fori_loop(1, nb // A_INFLIGHT, body, 0)


def _attn_a(main, slopes, sinks, main_runs, batch, seq):
    gw = A_GROUP * HEAD_DIM

    def spec(off, width):
        c = _compact(main_runs, off) // width
        return pl.BlockSpec((seq, width), lambda b, kv: (b, c + kv))

    return pl.pallas_call(
        _attn_a_kernel,
        grid=(batch, A_KV_HEADS),
        in_specs=[_smem(), _smem(), spec(OFF_QA, gw), spec(OFF_KA, HEAD_DIM), spec(OFF_VA, HEAD_DIM),
                  spec(OFF_ZA, gw)],
        out_specs=pl.BlockSpec((seq, gw), lambda b, kv: (b, kv)),
        out_shape=jax.ShapeDtypeStruct((batch * seq, A_WIDTH), BF16),
        scratch_shapes=[pltpu.VMEM((2, A_GROUP * BAND, BAND), F32), pltpu.VMEM((A_GROUP * BAND, 1), F32)],
        compiler_params=_params("arbitrary", "arbitrary"),
        name="attn_a",
    )(slopes, sinks, main, main, main, main)


def _attn_b_kernel(slopes_ref, q1, k1, v1, q2, k2, v2, q3, k3, v3, z_ref, y_ref, acc_s, m_s, l_s, bias_s):
    h = pl.program_id(1)
    seq = y_ref.shape[0]
    for g, (_, dil) in enumerate(B_GROUPS):
        bias_s[2 * g], bias_s[2 * g + 1] = _band_bias(slopes_ref[g * B_HEADS + h] * float(dil))

    def load_state(rows):
        return [(acc_s[rw, :], m_s[rw, :], l_s[rw, :]) for rw in rows]

    def store_state(rows, outs):
        for rw, (acc, m, den) in zip(rows, outs):
            acc_s[rw, :] = acc
            m_s[rw, :] = m
            l_s[rw, :] = den


    dil3 = B_GROUPS[2][1]
    assert seq // dil3 == BAND

    def g3_body(t, carry):
        rows = [pl.ds(t * B_INFLIGHT + j, BAND, stride=dil3) for j in range(B_INFLIGHT)]
        qkv = [tuple(a[rw, :].astype(BF16) for a in (q3, k3, v3)) for rw in rows]
        store_state(rows, _band_attention([(q, k, v, bias_s[4], None, None, None) for q, k, v in qkv]))
        return carry

    lax.fori_loop(0, dil3 // B_INFLIGHT, g3_body, 0)

    dil2 = B_GROUPS[1][1]
    nb2 = seq // dil2 // BAND
    per_body = max(1, B_INFLIGHT // nb2)

    def g2_body(t, carry):
        rows = [pl.ds(t * per_body + i + n * BAND * dil2, BAND, stride=dil2)
                for i in range(per_body) for n in range(nb2)]
        qkv = [tuple(a[rw, :].astype(BF16) for a in (q2, k2, v2)) for rw in rows]
        work = [(q, k, v, bias_s[2], None, None, None) if j % nb2 == 0 else
                (q, k, v, bias_s[2], qkv[j - 1][1], qkv[j - 1][2], bias_s[3]) for j, (q, k, v) in enumerate(qkv)]
        store_state(rows, _band_attention(work, carry=load_state(rows)))
        return carry

    lax.fori_loop(0, dil2 // per_body, g2_body, 0)

    def g1_blocks(first, first_has_prev):
        rows = [_block_rows(first + j) for j in range(B_INFLIGHT)]
        kv = [(k1[rw, :], v1[rw, :]) for rw in rows]
        if first_has_prev:
            before = _block_rows(first - 1)
            kv_before = (k1[before, :], v1[before, :])
        qs = [q1[rw, :] for rw in rows]
        zs = [z_ref[rw, :].astype(F32) for rw in rows]
        work = [(qs[j], *kv[j], bias_s[0], None, None, None) if j == 0 and not first_has_prev else
                (qs[j], *kv[j], bias_s[0], *(kv[j - 1] if j else kv_before), bias_s[1]) for j in range(B_INFLIGHT)]
        outs = [((acc / den) * _silu(z)).astype(y_ref.dtype)
                for (acc, _, den), z in zip(_band_attention(work, carry=load_state(rows)), zs)]
        for rw, y in zip(rows, outs):
            y_ref[rw, :] = y

    g1_blocks(0, False)

    def g1_body(t, carry):
        g1_blocks(t * B_INFLIGHT, True)
        return carry

    lax.fori_loop(1, seq // BAND // B_INFLIGHT, g1_body, 0)


def _attn_b(main, strided, slopes_b, main_runs, batch, seq):
    def spec(runs, off):
        c = _compact(runs, off) // HEAD_DIM
        return pl.BlockSpec((seq, HEAD_DIM), lambda b, h: (b, c + h))

    in_specs = [_smem()]
    args = [slopes_b]
    for g in range(len(B_GROUPS)):
        runs, arr = (main_runs, main) if g == 0 else (_STRIDED_RUNS, strided)
        in_specs += [spec(runs, off + g * B_WIDTH) for off in (OFF_QB, OFF_KB, OFF_VB)]
        args += [arr] * 3
    in_specs.append(spec(main_runs, OFF_ZB))
    args.append(main)
    return pl.pallas_call(
        _attn_b_kernel,
        grid=(batch, B_HEADS),
        in_specs=in_specs,
        out_specs=pl.BlockSpec((seq, HEAD_DIM), lambda b, h: (b, h)),
        out_shape=jax.ShapeDtypeStruct((batch * seq, B_WIDTH), BF16),
        scratch_shapes=[pltpu.VMEM((seq, HEAD_DIM), F32)] * 3 + [pltpu.VMEM((2 * len(B_GROUPS), BAND, BAND), F32)],
        compiler_params=_params("arbitrary", "arbitrary"),
        name="attn_b",
    )(*args)


def _head_norm_gate(o, w_row, z):
    mu = jnp.mean(o, axis=-1, keepdims=True)
    var = jnp.mean(jnp.square(o - mu), axis=-1, keepdims=True)
    y = (o - mu) * lax.rsqrt(var + NORM_EPS) * w_row
    return y * _silu(z)


def _retention_kernel(lg_ref, q_ref, k_ref, v_ref, z_ref, w_ref, y_ref, s_ref, decay_s):
    lg = lg_ref[pl.program_id(1)]
    row = lax.broadcasted_iota(jnp.int32, (C_CHUNK, C_CHUNK), 0)
    col = lax.broadcasted_iota(jnp.int32, (C_CHUNK, C_CHUNK), 1)
    diff = (row - col).astype(F32)
    decay_s[...] = jnp.where(diff >= 0, jnp.exp(lg * jnp.maximum(diff, 0.0)), 0.0)
    pos = lax.broadcasted_iota(jnp.int32, (C_CHUNK, 1), 0).astype(F32)
    q_dec = jnp.exp(lg * (pos + 1.0))
    k_dec = jnp.exp(lg * (float(C_CHUNK) - 1.0 - pos))
    chunk_decay = jnp.exp(lg * float(C_CHUNK))
    w_row = w_ref[...].astype(F32)

    def body(t, state):
        rows = [pl.ds(pl.multiple_of((t * C_INFLIGHT + j) * C_CHUNK, C_CHUNK), C_CHUNK) for j in range(C_INFLIGHT)]
        qs = [q_ref[rw, :].astype(F32) for rw in rows]
        ks = [k_ref[rw, :].astype(F32) * (C_QK_DIM ** -0.5) for rw in rows]
        vs = [v_ref[rw, :] for rw in rows]
        zs = [z_ref[rw, :].astype(F32) for rw in rows]
        attn = [lax.dot_general(q.astype(BF16), k.astype(BF16), (((1,), (1,)), ((), ())),
                                preferred_element_type=F32) for q, k in zip(qs, ks)]
        update = [jnp.dot(jnp.transpose(k * k_dec).astype(BF16), v, preferred_element_type=F32)
                  for k, v in zip(ks, vs)]
        intra = [jnp.dot((a * decay_s[...]).astype(BF16), v, preferred_element_type=F32)
                 for a, v in zip(attn, vs)]
        states = [state]
        for u in update:
            states.append(chunk_decay * states[-1] + u)
        cross = [jnp.dot((q * q_dec).astype(BF16), s0.astype(BF16), preferred_element_type=F32)
                 for q, s0 in zip(qs, states)]
        outs = [_head_norm_gate(i + c, w_row, z).astype(y_ref.dtype) for i, c, z in zip(intra, cross, zs)]
        for rw, y in zip(rows, outs):
            y_ref[rw, :] = y
        return states[-1]

    s_ref[...] = lax.fori_loop(0, q_ref.shape[0] // C_CHUNK // C_INFLIGHT, body,
                               jnp.zeros((C_QK_DIM, C_V_DIM), F32))


def _retention(main, log_gamma, ret_w_row, main_runs, batch, seq):
    def spec(off, width):
        c = _compact(main_runs, off) // width
        return pl.BlockSpec((seq, width), lambda b, h: (b, c + h))

    return pl.pallas_call(
        _retention_kernel,
        grid=(batch, C_HEADS),
        in_specs=[_smem(), spec(OFF_QC, C_QK_DIM), spec(OFF_KC, C_QK_DIM), spec(OFF_VC, C_V_DIM),
                  spec(OFF_ZC, C_V_DIM), pl.BlockSpec((1, C_V_DIM), lambda b, h: (0, h))],
        out_specs=[pl.BlockSpec((seq, C_V_DIM), lambda b, h: (b, h)),
                   pl.BlockSpec((None, None, C_QK_DIM, C_V_DIM), lambda b, h: (b, h, 0, 0))],
        out_shape=[jax.ShapeDtypeStruct((batch * seq, C_WIDTH), BF16),
                   jax.ShapeDtypeStruct((batch, C_HEADS, C_QK_DIM, C_V_DIM), F32)],
        scratch_shapes=[pltpu.VMEM((C_CHUNK, C_CHUNK), F32)],
        compiler_params=_params("arbitrary", "arbitrary"),
        name="retention",
    )(log_gamma, main, main, main, main, ret_w_row)


def _lane_sum(x):
    shape = x.shape
    flat = x.reshape(-1, shape[-1])
    hi = flat.astype(BF16)
    lo = (flat - hi.astype(F32)).astype(BF16)
    ones = jnp.ones((shape[-1], shape[-1]), BF16)
    total = jnp.dot(hi, ones, preferred_element_type=F32) + jnp.dot(lo, ones, preferred_element_type=F32)
    return total.reshape(shape)


def _sample_attend(q, k_buf, v_buf, k_new, v_new, slope, sink=None):
    steps_back = float(BAND) - lax.broadcasted_iota(jnp.int32, (1,) + k_buf.shape[1:], 1).astype(F32)
    logit_b = _lane_sum(k_buf * q) * SCALE - slope * steps_back
    logit_n = _lane_sum(k_new * q) * SCALE
    m = jnp.maximum(jnp.max(logit_b, axis=1, keepdims=True), logit_n)
    if sink is not None:
        m = jnp.maximum(m, sink)
    e_b = jnp.exp(logit_b - m)
    e_n = jnp.exp(logit_n - m)
    den = jnp.sum(e_b, axis=1, keepdims=True) + e_n
    if sink is not None:
        den = den + jnp.exp(sink - m)
    acc = jnp.sum(e_b * v_buf, axis=1, keepdims=True) + e_n * v_new
    return acc, m, den


def _column(row_vec):
    n = row_vec.shape[-1]
    eye = lax.broadcasted_iota(jnp.int32, (n, n), 0) == lax.broadcasted_iota(jnp.int32, (n, n), 1)
    return jnp.sum(jnp.where(eye, row_vec, 0.0), axis=-1, keepdims=True)


def _sample_kernel(lg_ref, sa_ref, sb_ref, sink_ref, hs_ref, p_ref, ca_ref, cb1_ref, cb2_ref, cb3_ref,
                   st_ref, w_ref, *rest):
    yab_ref, yc_ref, st_out = rest[-3:]

    def heads(off, count):
        u = off // HEAD_DIM
        return hs_ref[:, u:u + count, :][:, None]

    def per_query_head(x):
        return jnp.concatenate([jnp.broadcast_to(x[..., kv:kv + 1, :], x.shape[:-2] + (A_GROUP, x.shape[-1]))
                                for kv in range(A_KV_HEADS)], axis=-2)

    acc, _, den = _sample_attend(
        heads(OFF_QA, A_HEADS), per_query_head(ca_ref[:, :, 0]), per_query_head(ca_ref[:, :, 1]),
        per_query_head(heads(OFF_KA, A_KV_HEADS)), per_query_head(heads(OFF_VA, A_KV_HEADS)),
        sa_ref[...][None, None], sink_ref[...][None, None])
    yab_ref[:, :A_HEADS, :] = ((acc / den) * _silu(heads(OFF_ZA, A_HEADS)))[:, 0]

    parts = []
    for g, (cache, (_, dil)) in enumerate(zip((cb1_ref, cb2_ref, cb3_ref), B_GROUPS)):
        off = g * B_WIDTH
        parts.append(_sample_attend(
            heads(OFF_QB + off, B_HEADS), cache[:, :, 0], cache[:, :, 1],
            heads(OFF_KB + off, B_HEADS), heads(OFF_VB + off, B_HEADS),
            sb_ref[g * B_HEADS:(g + 1) * B_HEADS, :][None, None] * float(dil)))
    m_all = jnp.maximum(jnp.maximum(parts[0][1], parts[1][1]), parts[2][1])
    num = sum(acc * jnp.exp(m - m_all) for acc, m, _ in parts)
    den = sum(d * jnp.exp(m - m_all) for _, m, d in parts)
    yab_ref[:, A_HEADS:A_HEADS + B_HEADS, :] = ((num / den) * _silu(heads(OFF_ZB, B_HEADS)))[:, 0]

    for h in range(C_HEADS):
        gamma = jnp.exp(lg_ref[h])
        vo = h * C_V_DIM
        for i in range(SAMPLE_CHUNK):
            q = p_ref[i, :, OFF_QC + h * C_QK_DIM:OFF_QC + (h + 1) * C_QK_DIM]
            k = p_ref[i, :, OFF_KC + h * C_QK_DIM:OFF_KC + (h + 1) * C_QK_DIM] * (C_QK_DIM ** -0.5)
            v = p_ref[i, :, OFF_VC + vo:OFF_VC + vo + C_V_DIM]
            z = p_ref[i, :, OFF_ZC + vo:OFF_ZC + vo + C_V_DIM]
            s0 = st_ref[i, h]
            intra = jnp.sum(q * k, axis=-1, keepdims=True) * v
            cross = jnp.sum(_column(q * gamma) * s0, axis=0, keepdims=True)
            st_out[i, h] = gamma * s0 + _column(k) * v
            yc_ref[i, :, vo:vo + C_V_DIM] = _head_norm_gate(intra + cross, w_ref[:, vo:vo + C_V_DIM].astype(F32), z)


def _sample_mixers(flat, caches, state, state_stack, layer, log_gamma, slopes_a, slopes_b, sinks, ret_w_row):
    n, width = flat.shape
    nc = SAMPLE_CHUNK
    units = width // HEAD_DIM
    hs = flat.reshape(n, units, HEAD_DIM)
    p4 = flat.reshape(n // nc, nc, 1, width)
    depth = caches[0].shape[0]
    views = [caches[0]]
    for c, (win, dil) in zip(caches[1:], B_GROUPS):
        views.append(c.reshape(depth, n, BAND, dil, 2, B_HEADS, HEAD_DIM))

    def lanes(vec):
        return jnp.broadcast_to(vec.astype(F32)[:, None], (vec.shape[0], HEAD_DIM))

    def cache_b_spec():
        return pl.BlockSpec((None, nc, BAND, None, 2, B_HEADS, HEAD_DIM), lambda i: (layer, i, 0, 0, 0, 0, 0))

    args = [log_gamma, lanes(slopes_a), lanes(slopes_b), lanes(sinks), hs, p4, *views, state, ret_w_row]
    in_specs = [_smem(),
                pl.BlockSpec((A_HEADS, HEAD_DIM), lambda i: (0, 0)),
                pl.BlockSpec((len(B_GROUPS) * B_HEADS, HEAD_DIM), lambda i: (0, 0)),
                pl.BlockSpec((A_HEADS, HEAD_DIM), lambda i: (0, 0)),
                pl.BlockSpec((nc, units, HEAD_DIM), lambda i: (i, 0, 0)),
                pl.BlockSpec((None, nc, 1, width), lambda i: (i, 0, 0, 0)),
                pl.BlockSpec((None, nc, BAND, 2, A_KV_HEADS, HEAD_DIM), lambda i: (layer, i, 0, 0, 0, 0)),
                cache_b_spec(), cache_b_spec(), cache_b_spec(),
                pl.BlockSpec((None, nc, C_HEADS, C_QK_DIM, C_V_DIM), lambda i: (layer, i, 0, 0, 0)),
                pl.BlockSpec((1, C_WIDTH), lambda i: (0, 0))]
    aliases = {}
    if state_stack is not None:
        in_specs.append(pl.BlockSpec(memory_space=pl.ANY))
        aliases = {len(args): 2}
        args.append(state_stack)
    yab, yc, st = pl.pallas_call(
        _sample_kernel,
        grid=(n // nc,),
        in_specs=in_specs,
        out_specs=[pl.BlockSpec((nc, A_HEADS + B_HEADS, HEAD_DIM), lambda i: (i, 0, 0)),
                   pl.BlockSpec((None, nc, 1, C_WIDTH), lambda i: (i, 0, 0, 0)),
                   pl.BlockSpec((None, nc, C_HEADS, C_QK_DIM, C_V_DIM), lambda i: (layer, i, 0, 0, 0))],
        out_shape=[jax.ShapeDtypeStruct((n, A_HEADS + B_HEADS, HEAD_DIM), F32),
                   jax.ShapeDtypeStruct((n // nc, nc, 1, C_WIDTH), F32),
                   jax.ShapeDtypeStruct((depth, n, C_HEADS, C_QK_DIM, C_V_DIM), F32)],
        input_output_aliases=aliases,
        compiler_params=_params("arbitrary"),
        name="sample_mixers",
    )(*args)
    y = jnp.concatenate([yab.reshape(n, A_WIDTH + B_WIDTH), yc.reshape(n, C_WIDTH)], axis=-1)
    return y, st


def _kv_rows(k_arr, k_col, v_arr, v_col, batch, heads, rows):
    w = heads * HEAD_DIM
    t = k_arr.shape[0] // batch
    k = k_arr.reshape(batch, t, -1)[:, t - rows:, k_col:k_col + w]
    v = v_arr.reshape(batch, t, -1)[:, t - rows:, v_col:v_col + w]
    return jnp.stack([k, v], axis=2).reshape(batch, rows, 2, heads, HEAD_DIM).astype(F32)


def _window_rows(main, strided, main_runs, batch):
    t = main.shape[0] // batch
    out = [_kv_rows(main, _compact(main_runs, OFF_KA), main, _compact(main_runs, OFF_VA), batch,
                    A_KV_HEADS, min(BAND, t))]
    for g, (win, _) in enumerate(B_GROUPS):
        runs, arr = (main_runs, main) if g == 0 else (_STRIDED_RUNS, strided)
        out.append(_kv_rows(arr, _compact(runs, OFF_KB + g * B_WIDTH), arr, _compact(runs, OFF_VB + g * B_WIDTH),
                            batch, B_HEADS, min(win, t)))
    return out


def _kv_rows_kernel(*refs, pieces):
    k_refs, v_refs, o_ref = refs[:pieces], refs[pieces:2 * pieces], refs[-1]
    rows, _, heads, dh = o_ref.shape
    for slot, part_refs in enumerate((k_refs, v_refs)):
        flat = jnp.concatenate([r[...].astype(F32) for r in part_refs], axis=-1)
        o_ref[:, slot] = flat.reshape(rows, heads, dh)


def _kv_rows_into(stack, layer, depth, k_arr, k_col, v_arr, v_col, batch, heads, rows):
    w = heads * HEAD_DIM
    pw = min(w, TN)
    pieces = w // pw
    t = k_arr.shape[0] // batch
    tr = min(rows, 512)
    first, per_seq = (t - rows) // tr, t // tr

    def piece(col):
        return pl.BlockSpec((tr, pw), lambda b, i: (b * per_seq + first + i, col // pw))

    in_specs = ([piece(k_col + p * pw) for p in range(pieces)] + [piece(v_col + p * pw) for p in range(pieces)])
    args = [k_arr] * pieces + [v_arr] * pieces
    aliases = {}
    if stack is not None:
        in_specs.append(pl.BlockSpec(memory_space=pl.ANY))
        args.append(stack)
        aliases = {2 * pieces: 0}
    return pl.pallas_call(
        functools.partial(_kv_rows_kernel, pieces=pieces),
        grid=(batch, rows // tr),
        in_specs=in_specs,
        out_specs=pl.BlockSpec((None, None, tr, 2, heads, HEAD_DIM), lambda b, i: (layer, b, i, 0, 0, 0)),
        out_shape=jax.ShapeDtypeStruct((depth, batch, rows, 2, heads, HEAD_DIM), F32),
        input_output_aliases=aliases,
        compiler_params=_params("arbitrary", "arbitrary"),
        name="kv_rows",
    )(*args)


def _window_rows_into(stacks, layer, depth, main, main_runs, batch):
    t = main.shape[0] // batch
    stacks = stacks or [None, None]
    return [_kv_rows_into(stacks[0], layer, depth, main, _compact(main_runs, OFF_KA), main,
                          _compact(main_runs, OFF_VA), batch, A_KV_HEADS, min(BAND, t)),
            _kv_rows_into(stacks[1], layer, depth, main, _compact(main_runs, OFF_KB), main,
                          _compact(main_runs, OFF_VB), batch, B_HEADS, min(B_GROUPS[0][0], t))]


def _reference_order(main, strided, main_runs, upto):
    pieces = []
    for (lo, hi), src in sorted([(r, main) for r in main_runs] + [(r, strided) for r in _STRIDED_RUNS]):
        if lo >= upto:
            break
        runs = main_runs if src is main else _STRIDED_RUNS
        c = _compact(runs, lo)
        pieces.append(src[:, c:c + min(hi, upto) - lo])
    return jnp.concatenate(pieces, axis=-1)


def kernel(x_prompt, x_sample, cache_a, cache_b1, cache_b2, cache_b3, state_c, norm_w, w_in,
           w_branch_a, w_branch_b, w_branch_c, w_out, attn_sinks, ret_norm_w, final_norm_w):
    batch, seq, d_model = x_prompt.shape
    n_s, t_s, _ = x_sample.shape
    depth = w_in.shape[0]
    width = w_in.shape[-1]
    assert t_s == 1 and seq == B_GROUPS[-1][1] * BAND and n_s % SAMPLE_CHUNK == 0
    assert width == OFF_G + 3 * d_model and d_model % TN == 0
    assert cache_a.shape[2] == BAND
    assert all(c.shape[2] == win for c, (win, _) in zip((cache_b1, cache_b2, cache_b3), B_GROUPS))

    slopes_a = jnp.exp2(-8.0 * jnp.arange(1, A_HEADS + 1, dtype=F32) / A_HEADS)
    nbh = len(B_GROUPS) * B_HEADS
    slopes_b = jnp.exp2(-8.0 * jnp.arange(1, nbh + 1, dtype=F32) / nbh)
    log_gamma = jnp.log1p(-jnp.exp2(-5.0 - jnp.arange(C_HEADS, dtype=F32)))
    caches = (cache_a, cache_b1, cache_b2, cache_b3)
    main_runs = _main_runs(width)
    g_col = _compact(main_runs, OFF_G)

    xp = x_prompt.reshape(batch * seq, d_model)
    xs = x_sample.reshape(n_s, d_model)
    p_window, p_c, s_states, s_c = None, [], [], None
    for l in range(depth):
        nw = norm_w[l].reshape(1, d_model)
        rw = ret_norm_w[l].reshape(1, C_WIDTH)
        sinks = attn_sinks[l]

        hp = _rmsnorm(xp, nw, BF16)
        hsm = _rmsnorm(xs, nw, BF16)
        main, main_s = _matmul(hp, hsm, w_in, l, BF16, runs=main_runs, name="proj_main")
        strided_states = tuple(
            (None if p_window is None else p_window[1 + g], depth,
             _compact(_STRIDED_RUNS, OFF_KB + g * B_WIDTH) // WIDE, _compact(_STRIDED_RUNS, OFF_VB + g * B_WIDTH) // WIDE,
             min(B_GROUPS[g][0], seq)) for g in (1, 2))
        strided, strided_s, *strided_stacks = _matmul(hp, hsm, w_in, l, F32, runs=_STRIDED_RUNS,
                                                      states=strided_states, seq=seq, name="proj_strided")

        ya = _attn_a(main, slopes_a, sinks, main_runs, batch, seq)
        yb = _attn_b(main, strided, slopes_b, main_runs, batch, seq)
        yc, new_c = _retention(main, log_gamma, rw, main_runs, batch, seq)
        y_s, s_c = _sample_mixers(_reference_order(main_s, strided_s, main_runs, OFF_G), caches, state_c, s_c, l,
                                  log_gamma, slopes_a, slopes_b, sinks, rw)

        mixed, mixed_s = _branch_merge(ya, yb, yc, y_s, w_branch_a, w_branch_b, w_branch_c,
                                       main, main_s, g_col, l, d_model)
        xp, xs = _matmul(mixed, mixed_s, w_out, l, F32, residual=(xp, xs), name="proj_out")
        p_window = _window_rows_into(p_window, l, depth, main, main_runs, batch) + strided_stacks
        p_c.append(new_c)
        s_states.append(_window_rows(main_s, strided_s, main_runs, n_s))

    fw = final_norm_w.reshape(1, d_model)
    y_prompt = _rmsnorm(xp, fw, F32).reshape(batch, seq, d_model)
    y_sample = _rmsnorm(xs, fw, F32).reshape(n_s, 1, d_model)
    sample = [jnp.stack([s[i] for s in s_states]) for i in range(4)]
    return (y_prompt, y_sample, *p_window, jnp.stack(p_c), *sample, s_c)
```

```python
import functools

import jax
import jax.numpy as jnp
from jax import lax
from jax.experimental import pallas as pl
from jax.experimental.pallas import tpu as pltpu

F32 = jnp.float32
BF16 = jnp.bfloat16

HEAD_DIM = 128
BAND = 128
NORM_EPS = 1e-6
A_HEADS = 8
A_KV_HEADS = 2
A_GROUP = A_HEADS // A_KV_HEADS
B_GROUPS = ((128, 1), (512, 4), (2048, 16))
B_HEADS = 8
C_HEADS = 6
C_QK_DIM = 128
C_V_DIM = 256
C_CHUNK = 128

A_WIDTH = A_HEADS * HEAD_DIM
A_KV_WIDTH = A_KV_HEADS * HEAD_DIM
B_WIDTH = B_HEADS * HEAD_DIM
B_QKV_WIDTH = len(B_GROUPS) * B_WIDTH
C_QK_WIDTH = C_HEADS * C_QK_DIM
C_WIDTH = C_HEADS * C_V_DIM
Y_WIDTH = A_WIDTH + B_WIDTH + C_WIDTH

OFF_QA = 0
OFF_KA = OFF_QA + A_WIDTH
OFF_VA = OFF_KA + A_KV_WIDTH
OFF_ZA = OFF_VA + A_KV_WIDTH
OFF_QB = OFF_ZA + A_WIDTH
OFF_KB = OFF_QB + B_QKV_WIDTH
OFF_VB = OFF_KB + B_QKV_WIDTH
OFF_ZB = OFF_VB + B_QKV_WIDTH
OFF_QC = OFF_ZB + B_WIDTH
OFF_KC = OFF_QC + C_QK_WIDTH
OFF_VC = OFF_KC + C_QK_WIDTH
OFF_ZC = OFF_VC + C_WIDTH
OFF_G = OFF_ZC + C_WIDTH

TN = 512
PART = 256
PARTS = 4
WIDE = PARTS * PART
NEG_BIG = -1e30
SCALE = HEAD_DIM ** -0.5
assert BAND == HEAD_DIM
VMEM_LIMIT = 56 * 1024 * 1024
SAMPLE_CHUNK = 2
B_INFLIGHT = 8
A_INFLIGHT = 4
C_INFLIGHT = 8

_STRIDED_RUNS = tuple((off + B_WIDTH, off + B_QKV_WIDTH) for off in (OFF_QB, OFF_KB, OFF_VB))


def _main_runs(width):
    edges = [0] + [e for run in _STRIDED_RUNS for e in run] + [width]
    return tuple((edges[i], edges[i + 1]) for i in range(0, len(edges), 2))


def _compact(runs, col):
    base = 0
    for lo, hi in runs:
        if lo <= col < hi:
            return base + col - lo
        base += hi - lo
    raise ValueError(col)


def _block_map(runs):
    starts, gaps, pos, prev_hi = [], [], 0, 0
    for lo, hi in runs:
        starts.append(pos // PART)
        gaps.append((lo - prev_hi) // PART)
        pos += hi - lo
        prev_hi = hi

    def col_block(j):
        out = j
        for s, g in zip(starts, gaps):
            if g:
                out = out + jnp.where(j >= s, g, 0)
        return out

    return col_block, pos // PART


def _params(*sem):
    return pltpu.CompilerParams(dimension_semantics=sem, vmem_limit_bytes=VMEM_LIMIT)


def _smem():
    return pl.BlockSpec(memory_space=pltpu.SMEM)


def _silu(z):
    return z * jax.nn.sigmoid(z)


def _rmsnorm_kernel(x_ref, w_ref, o_ref):
    x = x_ref[...].astype(F32)
    ms = jnp.mean(x * x, axis=-1, keepdims=True)
    o_ref[...] = (x * lax.rsqrt(ms + NORM_EPS) * w_ref[...].astype(F32)).astype(o_ref.dtype)


def _rmsnorm(x, w_row, out_dtype):
    m, d = x.shape
    tr = min(m, 512)
    return pl.pallas_call(
        _rmsnorm_kernel,
        grid=(m // tr,),
        in_specs=[pl.BlockSpec((tr, d), lambda i: (i, 0)),
                  pl.BlockSpec((1, d), lambda i: (0, 0))],
        out_specs=pl.BlockSpec((tr, d), lambda i: (i, 0)),
        out_shape=jax.ShapeDtypeStruct((m, d), out_dtype),
        compiler_params=_params("arbitrary"),
        name="rmsnorm",
    )(x, w_row)


def _matmul_kernel(x_ref, xs_ref, w_ref, *rest, has_residual, row_steps, n_stacks, states, steps_per_seq):
    rest = list(rest)
    r_ref, rs_ref = (rest.pop(0), rest.pop(0)) if has_residual else (None, None)
    del rest[:n_stacks]
    o_ref, os_ref, *st_refs, wb_ref = rest
    j = pl.program_id(0)
    i = pl.program_id(1)
    steps_per_part = row_steps // PARTS
    kq = w_ref.shape[0] // steps_per_part
    tm = x_ref.shape[0]

    def stage():
        rows = pl.ds(pl.multiple_of((i % steps_per_part) * kq, kq), kq)
        wb_ref[j % 2, i // steps_per_part, rows, :] = w_ref[rows, :].astype(BF16)

    def multiply(with_sample, st_ref):
        lhs = x_ref[...]
        targets = [(slice(0, tm), r_ref, o_ref)]
        if with_sample:
            lhs = jnp.concatenate([lhs, xs_ref[...]], axis=0)
            targets.append((slice(tm, None), rs_ref, os_ref))
        for part in range(PARTS):
            cols = slice(part * PART, (part + 1) * PART)
            acc = jnp.dot(lhs, wb_ref[(j + 1) % 2, part], preferred_element_type=F32)
            for rows, res_ref, out_ref in targets:
                out = acc[rows]
                if has_residual:
                    out = out + res_ref[:, cols]
                out_ref[:, cols] = out.astype(out_ref.dtype)
            if st_ref is not None:
                tr, _, dh = st_ref.shape
                hp = PART // dh
                st_ref[:, part * hp:(part + 1) * hp, :] = acc[tm - tr:tm].reshape(tr, hp, dh)

    @pl.when(j == 0)
    def _():
        stage()

    feeds = [jnp.logical_and(jnp.logical_or(j == k_block + 1, j == v_block + 1),
                             i % steps_per_seq >= first_active) for k_block, v_block, first_active in states]
    feeds_none = functools.reduce(jnp.logical_and, [jnp.logical_not(f) for f in feeds], j > 0)
    for with_sample in (True, False):
        rows_match = (i == 0) if with_sample else (i > 0)
        for feed, st_ref in [(feeds_none, None)] + list(zip(feeds, st_refs)):
            @pl.when(jnp.logical_and(jnp.logical_and(j > 0, rows_match), feed))
            def _(with_sample=with_sample, st_ref=st_ref):
                multiply(with_sample, st_ref)
                stage()


def _matmul(x, xs, w_stack, layer, out_dtype, runs=None, residual=None, states=(), seq=None, name="matmul"):
    m, k = x.shape
    ms = xs.shape[0]
    tm = min(m // PARTS, 1024 if jnp.dtype(out_dtype).itemsize == 2 else 512)
    row_steps = m // tm
    col_block, nparts = _block_map(runs if runs is not None else ((0, w_stack.shape[-1]),))
    nblocks = nparts // PARTS
    steps_per_part = row_steps // PARTS
    assert nparts % PARTS == 0 and row_steps % PARTS == 0 and k % steps_per_part == 0

    def row_map(j, i):
        return jnp.where(j == 0, 0, i)

    def out_map(j, i):
        return (row_map(j, i), jnp.maximum(j - 1, 0))

    def w_map(j, i):
        return (layer, 0, col_block(PARTS * jnp.minimum(j, nblocks - 1) + i // steps_per_part))

    in_specs = [pl.BlockSpec((tm, k), lambda j, i: (row_map(j, i), 0)),
                pl.BlockSpec((ms, k), lambda j, i: (0, 0)),
                pl.BlockSpec((None, k, PART), w_map)]
    args = [x, xs, w_stack]
    if residual is not None:
        in_specs += [pl.BlockSpec((tm, WIDE), out_map),
                     pl.BlockSpec((ms, WIDE), lambda j, i: (0, jnp.maximum(j - 1, 0)))]
        args += list(residual)
    out_specs = [pl.BlockSpec((tm, WIDE), out_map),
                 pl.BlockSpec((ms, WIDE), lambda j, i: (0, jnp.maximum(j - 1, 0)))]
    out_shape = [jax.ShapeDtypeStruct((m, nblocks * WIDE), out_dtype),
                 jax.ShapeDtypeStruct((ms, nblocks * WIDE), F32)]

    heads = WIDE // HEAD_DIM
    steps_per_seq = (seq // tm) if states else 1
    fed_blocks = [b for _, _, k_block, v_block, _ in states for b in (k_block, v_block)]
    assert len(set(fed_blocks)) == len(fed_blocks)
    assert not states or (residual is None and out_dtype == F32)
    kernel_states, aliases, n_stacks = [], {}, 0
    for n, (stack, depth, k_block, v_block, rows) in enumerate(states):
        batch = m // seq
        assert seq % tm == 0 and (rows % tm == 0 or rows < tm) and k_block < v_block
        n_active = max(rows // tm, 1)
        first_active = steps_per_seq - n_active
        kernel_states.append((k_block, v_block, first_active))

        def st_map(j, i, kb=k_block + 1, vb=v_block + 1, first_active=first_active, n_active=n_active, batch=batch):
            s, b = i % steps_per_seq, i // steps_per_seq
            started = s >= first_active
            b_here = jnp.maximum(jnp.where(started, b, b - 1), 0)
            r_here = jnp.where(jnp.logical_and(jnp.logical_not(started), b > 0), n_active - 1,
                               jnp.maximum(s - first_active, 0))
            writing = jnp.logical_or(j == kb, j == vb)
            done = jnp.logical_or(jnp.logical_and(j > kb, j < vb), j > vb)
            b_idx = jnp.where(writing, b_here, jnp.where(done, batch - 1, 0))
            r_idx = jnp.where(writing, r_here, jnp.where(done, n_active - 1, 0))
            return (layer, b_idx, r_idx, jnp.where(j >= vb, 1, 0), 0, 0)

        out_specs.append(pl.BlockSpec((None, None, min(rows, tm), None, heads, HEAD_DIM), st_map))
        out_shape.append(jax.ShapeDtypeStruct((depth, batch, rows, 2, heads, HEAD_DIM), F32))
        if stack is not None:
            in_specs.append(pl.BlockSpec(memory_space=pl.ANY))
            aliases[len(args)] = 2 + n
            args.append(stack)
            n_stacks += 1
    return pl.pallas_call(
        functools.partial(_matmul_kernel, has_residual=residual is not None, row_steps=row_steps,
                          n_stacks=n_stacks, states=tuple(kernel_states), steps_per_seq=steps_per_seq),
        grid=(nblocks + 1, row_steps),
        in_specs=in_specs,
        out_specs=out_specs,
        out_shape=out_shape,
        input_output_aliases=aliases,
        scratch_shapes=[pltpu.VMEM((2, PARTS, k, PART), BF16)],
        compiler_params=_params("arbitrary", "arbitrary"),
        name=name,
    )(*args)


def _branch_kernel(ya_ref, yb_ref, yc_ref, ys_ref, wa_ref, wb_ref, wc_ref, ga_ref, gb_ref, gc_ref,
                   gsa_ref, gsb_ref, gsc_ref, o_ref, os_ref, wa_s, wb_s, wc_s, *, row_steps):
    j = pl.program_id(0)
    i = pl.program_id(1)
    steps_per_part = row_steps // PARTS
    staged = ((wa_ref, wa_s), (wb_ref, wb_s), (wc_ref, wc_s))

    def stage():
        for w_ref, w_s in staged:
            kq = w_ref.shape[0] // steps_per_part
            rows = pl.ds(pl.multiple_of((i % steps_per_part) * kq, kq), kq)
            w_s[j % 2, i // steps_per_part, rows, :] = w_ref[rows, :].astype(BF16)

    def merge(with_sample):
        tm = ya_ref.shape[0]
        lhs = [ya_ref[...], yb_ref[...], yc_ref[...]]
        targets = [(slice(0, tm), (ga_ref, gb_ref, gc_ref), o_ref)]
        if with_sample:
            ys = ys_ref[...].astype(BF16)
            bounds = (0, A_WIDTH, A_WIDTH + B_WIDTH, Y_WIDTH)
            lhs = [jnp.concatenate([y, ys[:, lo:hi]], axis=0) for y, lo, hi in zip(lhs, bounds, bounds[1:])]
            targets.append((slice(tm, None), (gsa_ref, gsb_ref, gsc_ref), os_ref))
        for part in range(PARTS):
            cols = slice(part * PART, (part + 1) * PART)
            products = [jnp.dot(y, w_s[(j + 1) % 2, part], preferred_element_type=F32)
                        for y, (_, w_s) in zip(lhs, staged)]
            for rows, g_refs, out_ref in targets:
                mixed = None
                for u, g_ref in zip(products, g_refs):
                    term = jax.nn.sigmoid(g_ref[:, cols].astype(F32)) * u[rows]
                    mixed = term if mixed is None else mixed + term
                out_ref[:, cols] = mixed.astype(out_ref.dtype)

    @pl.when(j == 0)
    def _():
        stage()

    @pl.when(jnp.logical_and(j > 0, i == 0))
    def _():
        merge(True)
        stage()

    @pl.when(jnp.logical_and(j > 0, i > 0))
    def _():
        merge(False)
        stage()


def _branch_merge(ya, yb, yc, y_s, w_a, w_b, w_c, main, main_s, g_col, layer, d_model):
    m = ya.shape[0]
    ms = y_s.shape[0]
    tm = min(m // PARTS, 512)
    row_steps = m // tm
    steps_per_part = row_steps // PARTS
    nblocks = d_model // WIDE
    assert row_steps % PARTS == 0 and g_col % WIDE == 0 and d_model % WIDE == 0
    assert all(w % steps_per_part == 0 for w in (A_WIDTH, B_WIDTH, C_WIDTH))

    def row_map(j, i):
        return jnp.where(j == 0, 0, i)

    def col_map(j):
        return jnp.maximum(j - 1, 0)

    def y_spec(width):
        return pl.BlockSpec((tm, width), lambda j, i: (row_map(j, i), 0))

    def w_spec(width):
        return pl.BlockSpec((None, width, PART),
                            lambda j, i: (layer, 0, PARTS * jnp.minimum(j, nblocks - 1) + i // steps_per_part))

    def g_spec(rows, idx, whole):
        first = (g_col + idx * d_model) // WIDE
        if whole:
            return pl.BlockSpec((rows, WIDE), lambda j, i: (0, first + col_map(j)))
        return pl.BlockSpec((rows, WIDE), lambda j, i: (row_map(j, i), first + col_map(j)))

    return pl.pallas_call(
        functools.partial(_branch_kernel, row_steps=row_steps),
        grid=(nblocks + 1, row_steps),
        in_specs=[y_spec(A_WIDTH), y_spec(B_WIDTH), y_spec(C_WIDTH),
                  pl.BlockSpec((ms, Y_WIDTH), lambda j, i: (0, 0)),
                  w_spec(A_WIDTH), w_spec(B_WIDTH), w_spec(C_WIDTH),
                  g_spec(tm, 0, False), g_spec(tm, 1, False), g_spec(tm, 2, False),
                  g_spec(ms, 0, True), g_spec(ms, 1, True), g_spec(ms, 2, True)],
        out_specs=[pl.BlockSpec((tm, WIDE), lambda j, i: (row_map(j, i), col_map(j))),
                   pl.BlockSpec((ms, WIDE), lambda j, i: (0, col_map(j)))],
        out_shape=[jax.ShapeDtypeStruct((m, d_model), BF16),
                   jax.ShapeDtypeStruct((ms, d_model), BF16)],
        scratch_shapes=[pltpu.VMEM((2, PARTS, w, PART), BF16) for w in (A_WIDTH, B_WIDTH, C_WIDTH)],
        compiler_params=_params("arbitrary", "arbitrary"),
        name="branch_merge",
    )(ya, yb, yc, y_s, w_a, w_b, w_c, main, main, main, main_s, main_s, main_s)


def _band_bias(slope_step, rows_per_tile=1):
    shape = (rows_per_tile * BAND, BAND)
    row = lax.broadcasted_iota(jnp.int32, shape, 0) % BAND
    col = lax.broadcasted_iota(jnp.int32, shape, 1)
    dist = (row - col).astype(F32)
    cur = jnp.where(col <= row, -slope_step * dist, NEG_BIG)
    prev = jnp.where(col >= row, -slope_step * (dist + float(BAND)), NEG_BIG)
    return cur, prev


def _block_rows(n):
    start = n * BAND if isinstance(n, int) else pl.multiple_of(n * BAND, BAND)
    return pl.ds(start, BAND)


def _qk(q, k):
    return lax.dot_general(q, k, (((1,), (1,)), ((), ())), preferred_element_type=F32) * SCALE


def _band_attention(blocks, sink=None, carry=None):
    logits = []
    for q, k_cur, _, bias_cur, k_prev, _, bias_prev in blocks:
        logit_p = None if k_prev is None else _qk(q, k_prev) + bias_prev
        logits.append((_qk(q, k_cur) + bias_cur, logit_p))
    probs = []
    for n, (logit_c, logit_p) in enumerate(logits):
        m = jnp.max(logit_c, axis=-1, keepdims=True)
        if logit_p is not None:
            m = jnp.maximum(m, jnp.max(logit_p, axis=-1, keepdims=True))
        if sink is not None:
            m = jnp.maximum(m, sink)
        m_tile = jnp.broadcast_to(m, logit_c.shape)
        alpha = None
        if carry is not None:
            m_tile = jnp.maximum(m_tile, carry[n][1])
            alpha = jnp.exp(carry[n][1] - m_tile)
        e_c = jnp.exp(logit_c - m_tile)
        den = jnp.sum(e_c, axis=-1, keepdims=True)
        e_p = None
        if logit_p is not None:
            e_p = jnp.exp(logit_p - m_tile)
            den = den + jnp.sum(e_p, axis=-1, keepdims=True)
            e_p = e_p.astype(BF16)
        if sink is not None:
            den = den + jnp.exp(sink - m)
        den_tile = jnp.broadcast_to(den, logit_c.shape)
        if carry is not None:
            den_tile = den_tile + carry[n][2] * alpha
        probs.append((e_c.astype(BF16), e_p, m_tile, den_tile, alpha))
    outs = []
    for n, ((_, _, v_cur, _, _, v_prev, _), (e_c, e_p, m, den, alpha)) in enumerate(zip(blocks, probs)):
        acc = jnp.dot(e_c, v_cur, preferred_element_type=F32)
        if e_p is not None:
            acc = acc + jnp.dot(e_p, v_prev, preferred_element_type=F32)
        if carry is not None:
            acc = acc + carry[n][0] * alpha
        outs.append((acc, m, den))
    return outs


def _attn_a_kernel(slopes_ref, sinks_ref, q_ref, k_ref, v_ref, z_ref, y_ref, bias_s, sink_s):
    kv = pl.program_id(1)
    rows = A_GROUP * BAND
    head = lax.broadcasted_iota(jnp.int32, (rows, 1), 0) // BAND
    slope_col = jnp.zeros((rows, 1), F32)
    sink_col = jnp.zeros((rows, 1), F32)
    for g in range(A_GROUP):
        slope_col = jnp.where(head == g, slopes_ref[kv * A_GROUP + g], slope_col)
        sink_col = jnp.where(head == g, sinks_ref[kv * A_GROUP + g], sink_col)
    bias_s[0], bias_s[1] = _band_bias(slope_col, A_GROUP)
    sink_s[...] = sink_col
    nb = q_ref.shape[0] // BAND

    def blocks(first, first_has_prev):
        rows_ = [_block_rows(first + j) for j in range(A_INFLIGHT)]
        kv_ = [(k_ref[rw, :], v_ref[rw, :]) for rw in rows_]
        if first_has_prev:
            before = _block_rows(first - 1)
            kv_before = (k_ref[before, :], v_ref[before, :])
        work = []
        for j, rw in enumerate(rows_):
            q = jnp.concatenate([q_ref[rw, g * HEAD_DIM:(g + 1) * HEAD_DIM] for g in range(A_GROUP)], axis=0)
            if j == 0 and not first_has_prev:
                work.append((q, *kv_[j], bias_s[0], None, None, None))
            else:
                work.append((q, *kv_[j], bias_s[0], *(kv_[j - 1] if j else kv_before), bias_s[1]))
        zs = [[z_ref[rw, g * HEAD_DIM:(g + 1) * HEAD_DIM].astype(F32) for g in range(A_GROUP)] for rw in rows_]
        outs = []
        for (acc, _, den), z in zip(_band_attention(work, sink=sink_s[...]), zs):
            o = acc / den
            outs.append([(o[g * BAND:(g + 1) * BAND] * _silu(z[g])).astype(y_ref.dtype) for g in range(A_GROUP)])
        for rw, ys in zip(rows_, outs):
            for g in range(A_GROUP):
                y_ref[rw, g * HEAD_DIM:(g + 1) * HEAD_DIM] = ys[g]

    blocks(0, False)

    def body(t, carry):
        blocks(t * A_INFLIGHT, True)
        return carry

    lax.fori_loop(1, nb // A_INFLIGHT, body, 0)


def _attn_a(main, slopes, sinks, main_runs, batch, seq):
    gw = A_GROUP * HEAD_DIM

    def spec(off, width):
        c = _compact(main_runs, off) // width
        return pl.BlockSpec((seq, width), lambda b, kv: (b, c + kv))

    return pl.pallas_call(
        _attn_a_kernel,
        grid=(batch, A_KV_HEADS),
        in_specs=[_smem(), _smem(), spec(OFF_QA, gw), spec(OFF_KA, HEAD_DIM), spec(OFF_VA, HEAD_DIM),
                  spec(OFF_ZA, gw)],
        out_specs=pl.BlockSpec((seq, gw), lambda b, kv: (b, kv)),
        out_shape=jax.ShapeDtypeStruct((batch * seq, A_WIDTH), BF16),
        scratch_shapes=[pltpu.VMEM((2, A_GROUP * BAND, BAND), F32), pltpu.VMEM((A_GROUP * BAND, 1), F32)],
        compiler_params=_params("arbitrary", "arbitrary"),
        name="attn_a",
    )(slopes, sinks, main, main, main, main)


def _attn_b_kernel(slopes_ref, q1, k1, v1, q2, k2, v2, q3, k3, v3, z_ref, y_ref, acc_s, m_s, l_s, bias_s):
    h = pl.program_id(1)
    seq = y_ref.shape[0]
    for g, (_, dil) in enumerate(B_GROUPS):
        bias_s[2 * g], bias_s[2 * g + 1] = _band_bias(slopes_ref[g * B_HEADS + h] * float(dil))

    def load_state(rows):
        return [(acc_s[rw, :], m_s[rw, :], l_s[rw, :]) for rw in rows]

    def store_state(rows, outs):
        for rw, (acc, m, den) in zip(rows, outs):
            acc_s[rw, :] = acc
            m_s[rw, :] = m
            l_s[rw, :] = den


    dil3 = B_GROUPS[2][1]
    assert seq // dil3 == BAND

    def g3_body(t, carry):
        rows = [pl.ds(t * B_INFLIGHT + j, BAND, stride=dil3) for j in range(B_INFLIGHT)]
        qkv = [tuple(a[rw, :].astype(BF16) for a in (q3, k3, v3)) for rw in rows]
        store_state(rows, _band_attention([(q, k, v, bias_s[4], None, None, None) for q, k, v in qkv]))
        return carry

    lax.fori_loop(0, dil3 // B_INFLIGHT, g3_body, 0)

    dil2 = B_GROUPS[1][1]
    nb2 = seq // dil2 // BAND
    per_body = max(1, B_INFLIGHT // nb2)

    def g2_body(t, carry):
        rows = [pl.ds(t * per_body + i + n * BAND * dil2, BAND, stride=dil2)
                for i in range(per_body) for n in range(nb2)]
        qkv = [tuple(a[rw, :].astype(BF16) for a in (q2, k2, v2)) for rw in rows]
        work = [(q, k, v, bias_s[2], None, None, None) if j % nb2 == 0 else
                (q, k, v, bias_s[2], qkv[j - 1][1], qkv[j - 1][2], bias_s[3]) for j, (q, k, v) in enumerate(qkv)]
        store_state(rows, _band_attention(work, carry=load_state(rows)))
        return carry

    lax.fori_loop(0, dil2 // per_body, g2_body, 0)

    def g1_blocks(first, first_has_prev):
        rows = [_block_rows(first + j) for j in range(B_INFLIGHT)]
        kv = [(k1[rw, :], v1[rw, :]) for rw in rows]
        if first_has_prev:
            before = _block_rows(first - 1)
            kv_before = (k1[before, :], v1[before, :])
        qs = [q1[rw, :] for rw in rows]
        zs = [z_ref[rw, :].astype(F32) for rw in rows]
        work = [(qs[j], *kv[j], bias_s[0], None, None, None) if j == 0 and not first_has_prev else
                (qs[j], *kv[j], bias_s[0], *(kv[j - 1] if j else kv_before), bias_s[1]) for j in range(B_INFLIGHT)]
        outs = [((acc / den) * _silu(z)).astype(y_ref.dtype)
                for (acc, _, den), z in zip(_band_attention(work, carry=load_state(rows)), zs)]
        for rw, y in zip(rows, outs):
            y_ref[rw, :] = y

    g1_blocks(0, False)

    def g1_body(t, carry):
        g1_blocks(t * B_INFLIGHT, True)
        return carry

    lax.fori_loop(1, seq // BAND // B_INFLIGHT, g1_body, 0)


def _attn_b(main, strided, slopes_b, main_runs, batch, seq):
    def spec(runs, off):
        c = _compact(runs, off) // HEAD_DIM
        return pl.BlockSpec((seq, HEAD_DIM), lambda b, h: (b, c + h))

    in_specs = [_smem()]
    args = [slopes_b]
    for g in range(len(B_GROUPS)):
        runs, arr = (main_runs, main) if g == 0 else (_STRIDED_RUNS, strided)
        in_specs += [spec(runs, off + g * B_WIDTH) for off in (OFF_QB, OFF_KB, OFF_VB)]
        args += [arr] * 3
    in_specs.append(spec(main_runs, OFF_ZB))
    args.append(main)
    return pl.pallas_call(
        _attn_b_kernel,
        grid=(batch, B_HEADS),
        in_specs=in_specs,
        out_specs=pl.BlockSpec((seq, HEAD_DIM), lambda b, h: (b, h)),
        out_shape=jax.ShapeDtypeStruct((batch * seq, B_WIDTH), BF16),
        scratch_shapes=[pltpu.VMEM((seq, HEAD_DIM), F32)] * 3 + [pltpu.VMEM((2 * len(B_GROUPS), BAND, BAND), F32)],
        compiler_params=_params("arbitrary", "arbitrary"),
        name="attn_b",
    )(*args)


def _head_norm_gate(o, w_row, z):
    mu = jnp.mean(o, axis=-1, keepdims=True)
    var = jnp.mean(jnp.square(o - mu), axis=-1, keepdims=True)
    y = (o - mu) * lax.rsqrt(var + NORM_EPS) * w_row
    return y * _silu(z)


def _retention_kernel(lg_ref, q_ref, k_ref, v_ref, z_ref, w_ref, y_ref, s_ref, decay_s):
    lg = lg_ref[pl.program_id(1)]
    row = lax.broadcasted_iota(jnp.int32, (C_CHUNK, C_CHUNK), 0)
    col = lax.broadcasted_iota(jnp.int32, (C_CHUNK, C_CHUNK), 1)
    diff = (row - col).astype(F32)
    decay_s[...] = jnp.where(diff >= 0, jnp.exp(lg * jnp.maximum(diff, 0.0)), 0.0)
    pos = lax.broadcasted_iota(jnp.int32, (C_CHUNK, 1), 0).astype(F32)
    q_dec = jnp.exp(lg * (pos + 1.0))
    k_dec = jnp.exp(lg * (float(C_CHUNK) - 1.0 - pos))
    chunk_decay = jnp.exp(lg * float(C_CHUNK))
    w_row = w_ref[...].astype(F32)

    def body(t, state):
        rows = [pl.ds(pl.multiple_of((t * C_INFLIGHT + j) * C_CHUNK, C_CHUNK), C_CHUNK) for j in range(C_INFLIGHT)]
        qs = [q_ref[rw, :].astype(F32) for rw in rows]
        ks = [k_ref[rw, :].astype(F32) * (C_QK_DIM ** -0.5) for rw in rows]
        vs = [v_ref[rw, :] for rw in rows]
        zs = [z_ref[rw, :].astype(F32) for rw in rows]
        attn = [lax.dot_general(q.astype(BF16), k.astype(BF16), (((1,), (1,)), ((), ())),
                                preferred_element_type=F32) for q, k in zip(qs, ks)]
        update = [jnp.dot(jnp.transpose(k * k_dec).astype(BF16), v, preferred_element_type=F32)
                  for k, v in zip(ks, vs)]
        intra = [jnp.dot((a * decay_s[...]).astype(BF16), v, preferred_element_type=F32)
                 for a, v in zip(attn, vs)]
        states = [state]
        for u in update:
            states.append(chunk_decay * states[-1] + u)
        cross = [jnp.dot((q * q_dec).astype(BF16), s0.astype(BF16), preferred_element_type=F32)
                 for q, s0 in zip(qs, states)]
        outs = [_head_norm_gate(i + c, w_row, z).astype(y_ref.dtype) for i, c, z in zip(intra, cross, zs)]
        for rw, y in zip(rows, outs):
            y_ref[rw, :] = y
        return states[-1]

    s_ref[...] = lax.fori_loop(0, q_ref.shape[0] // C_CHUNK // C_INFLIGHT, body,
                               jnp.zeros((C_QK_DIM, C_V_DIM), F32))


def _retention(main, log_gamma, ret_w_row, main_runs, batch, seq):
    def spec(off, width):
        c = _compact(main_runs, off) // width
        return pl.BlockSpec((seq, width), lambda b, h: (b, c + h))

    return pl.pallas_call(
        _retention_kernel,
        grid=(batch, C_HEADS),
        in_specs=[_smem(), spec(OFF_QC, C_QK_DIM), spec(OFF_KC, C_QK_DIM), spec(OFF_VC, C_V_DIM),
                  spec(OFF_ZC, C_V_DIM), pl.BlockSpec((1, C_V_DIM), lambda b, h: (0, h))],
        out_specs=[pl.BlockSpec((seq, C_V_DIM), lambda b, h: (b, h)),
                   pl.BlockSpec((None, None, C_QK_DIM, C_V_DIM), lambda b, h: (b, h, 0, 0))],
        out_shape=[jax.ShapeDtypeStruct((batch * seq, C_WIDTH), BF16),
                   jax.ShapeDtypeStruct((batch, C_HEADS, C_QK_DIM, C_V_DIM), F32)],
        scratch_shapes=[pltpu.VMEM((C_CHUNK, C_CHUNK), F32)],
        compiler_params=_params("arbitrary", "arbitrary"),
        name="retention",
    )(log_gamma, main, main, main, main, ret_w_row)


def _lane_sum(x):
    shape = x.shape
    flat = x.reshape(-1, shape[-1])
    hi = flat.astype(BF16)
    lo = (flat - hi.astype(F32)).astype(BF16)
    ones = jnp.ones((shape[-1], shape[-1]), BF16)
    total = jnp.dot(hi, ones, preferred_element_type=F32) + jnp.dot(lo, ones, preferred_element_type=F32)
    return total.reshape(shape)


def _sample_attend(q, k_buf, v_buf, k_new, v_new, slope, sink=None):
    steps_back = float(BAND) - lax.broadcasted_iota(jnp.int32, (1,) + k_buf.shape[1:], 1).astype(F32)
    logit_b = _lane_sum(k_buf * q) * SCALE - slope * steps_back
    logit_n = _lane_sum(k_new * q) * SCALE
    m = jnp.maximum(jnp.max(logit_b, axis=1, keepdims=True), logit_n)
    if sink is not None:
        m = jnp.maximum(m, sink)
    e_b = jnp.exp(logit_b - m)
    e_n = jnp.exp(logit_n - m)
    den = jnp.sum(e_b, axis=1, keepdims=True) + e_n
    if sink is not None:
        den = den + jnp.exp(sink - m)
    acc = jnp.sum(e_b * v_buf, axis=1, keepdims=True) + e_n * v_new
    return acc, m, den


def _column(row_vec):
    n = row_vec.shape[-1]
    eye = lax.broadcasted_iota(jnp.int32, (n, n), 0) == lax.broadcasted_iota(jnp.int32, (n, n), 1)
    return jnp.sum(jnp.where(eye, row_vec, 0.0), axis=-1, keepdims=True)


def _sample_kernel(lg_ref, sa_ref, sb_ref, sink_ref, hs_ref, p_ref, ca_ref, cb1_ref, cb2_ref, cb3_ref,
                   st_ref, w_ref, *rest):
    yab_ref, yc_ref, st_out = rest[-3:]

    def heads(off, count):
        u = off // HEAD_DIM
        return hs_ref[:, u:u + count, :][:, None]

    def per_query_head(x):
        return jnp.concatenate([jnp.broadcast_to(x[..., kv:kv + 1, :], x.shape[:-2] + (A_GROUP, x.shape[-1]))
                                for kv in range(A_KV_HEADS)], axis=-2)

    acc, _, den = _sample_attend(
        heads(OFF_QA, A_HEADS), per_query_head(ca_ref[:, :, 0]), per_query_head(ca_ref[:, :, 1]),
        per_query_head(heads(OFF_KA, A_KV_HEADS)), per_query_head(heads(OFF_VA, A_KV_HEADS)),
        sa_ref[...][None, None], sink_ref[...][None, None])
    yab_ref[:, :A_HEADS, :] = ((acc / den) * _silu(heads(OFF_ZA, A_HEADS)))[:, 0]

    parts = []
    for g, (cache, (_, dil)) in enumerate(zip((cb1_ref, cb2_ref, cb3_ref), B_GROUPS)):
        off = g * B_WIDTH
        parts.append(_sample_attend(
            heads(OFF_QB + off, B_HEADS), cache[:, :, 0], cache[:, :, 1],
            heads(OFF_KB + off, B_HEADS), heads(OFF_VB + off, B_HEADS),
            sb_ref[g * B_HEADS:(g + 1) * B_HEADS, :][None, None] * float(dil)))
    m_all = jnp.maximum(jnp.maximum(parts[0][1], parts[1][1]), parts[2][1])
    num = sum(acc * jnp.exp(m - m_all) for acc, m, _ in parts)
    den = sum(d * jnp.exp(m - m_all) for _, m, d in parts)
    yab_ref[:, A_HEADS:A_HEADS + B_HEADS, :] = ((num / den) * _silu(heads(OFF_ZB, B_HEADS)))[:, 0]

    for h in range(C_HEADS):
        gamma = jnp.exp(lg_ref[h])
        vo = h * C_V_DIM
        for i in range(SAMPLE_CHUNK):
            q = p_ref[i, :, OFF_QC + h * C_QK_DIM:OFF_QC + (h + 1) * C_QK_DIM]
            k = p_ref[i, :, OFF_KC + h * C_QK_DIM:OFF_KC + (h + 1) * C_QK_DIM] * (C_QK_DIM ** -0.5)
            v = p_ref[i, :, OFF_VC + vo:OFF_VC + vo + C_V_DIM]
            z = p_ref[i, :, OFF_ZC + vo:OFF_ZC + vo + C_V_DIM]
            s0 = st_ref[i, h]
            intra = jnp.sum(q * k, axis=-1, keepdims=True) * v
            cross = jnp.sum(_column(q * gamma) * s0, axis=0, keepdims=True)
            st_out[i, h] = gamma * s0 + _column(k) * v
            yc_ref[i, :, vo:vo + C_V_DIM] = _head_norm_gate(intra + cross, w_ref[:, vo:vo + C_V_DIM].astype(F32), z)


def _sample_mixers(flat, caches, state, state_stack, layer, log_gamma, slopes_a, slopes_b, sinks, ret_w_row):
    n, width = flat.shape
    nc = SAMPLE_CHUNK
    units = width // HEAD_DIM
    hs = flat.reshape(n, units, HEAD_DIM)
    p4 = flat.reshape(n // nc, nc, 1, width)
    depth = caches[0].shape[0]
    views = [caches[0]]
    for c, (win, dil) in zip(caches[1:], B_GROUPS):
        views.append(c.reshape(depth, n, BAND, dil, 2, B_HEADS, HEAD_DIM))

    def lanes(vec):
        return jnp.broadcast_to(vec.astype(F32)[:, None], (vec.shape[0], HEAD_DIM))

    def cache_b_spec():
        return pl.BlockSpec((None, nc, BAND, None, 2, B_HEADS, HEAD_DIM), lambda i: (layer, i, 0, 0, 0, 0, 0))

    args = [log_gamma, lanes(slopes_a), lanes(slopes_b), lanes(sinks), hs, p4, *views, state, ret_w_row]
    in_specs = [_smem(),
                pl.BlockSpec((A_HEADS, HEAD_DIM), lambda i: (0, 0)),
                pl.BlockSpec((len(B_GROUPS) * B_HEADS, HEAD_DIM), lambda i: (0, 0)),
                pl.BlockSpec((A_HEADS, HEAD_DIM), lambda i: (0, 0)),
                pl.BlockSpec((nc, units, HEAD_DIM), lambda i: (i, 0, 0)),
                pl.BlockSpec((None, nc, 1, width), lambda i: (i, 0, 0, 0)),
                pl.BlockSpec((None, nc, BAND, 2, A_KV_HEADS, HEAD_DIM), lambda i: (layer, i, 0, 0, 0, 0)),
                cache_b_spec(), cache_b_spec(), cache_b_spec(),
                pl.BlockSpec((None, nc, C_HEADS, C_QK_DIM, C_V_DIM), lambda i: (layer, i, 0, 0, 0)),
                pl.BlockSpec((1, C_WIDTH), lambda i: (0, 0))]
    aliases = {}
    if state_stack is not None:
        in_specs.append(pl.BlockSpec(memory_space=pl.ANY))
        aliases = {len(args): 2}
        args.append(state_stack)
    yab, yc, st = pl.pallas_call(
        _sample_kernel,
        grid=(n // nc,),
        in_specs=in_specs,
        out_specs=[pl.BlockSpec((nc, A_HEADS + B_HEADS, HEAD_DIM), lambda i: (i, 0, 0)),
                   pl.BlockSpec((None, nc, 1, C_WIDTH), lambda i: (i, 0, 0, 0)),
                   pl.BlockSpec((None, nc, C_HEADS, C_QK_DIM, C_V_DIM), lambda i: (layer, i, 0, 0, 0))],
        out_shape=[jax.ShapeDtypeStruct((n, A_HEADS + B_HEADS, HEAD_DIM), F32),
                   jax.ShapeDtypeStruct((n // nc, nc, 1, C_WIDTH), F32),
                   jax.ShapeDtypeStruct((depth, n, C_HEADS, C_QK_DIM, C_V_DIM), F32)],
        input_output_aliases=aliases,
        compiler_params=_params("arbitrary"),
        name="sample_mixers",
    )(*args)
    y = jnp.concatenate([yab.reshape(n, A_WIDTH + B_WIDTH), yc.reshape(n, C_WIDTH)], axis=-1)
    return y, st


def _kv_rows(k_arr, k_col, v_arr, v_col, batch, heads, rows):
    w = heads * HEAD_DIM
    t = k_arr.shape[0] // batch
    k = k_arr.reshape(batch, t, -1)[:, t - rows:, k_col:k_col + w]
    v = v_arr.reshape(batch, t, -1)[:, t - rows:, v_col:v_col + w]
    return jnp.stack([k, v], axis=2).reshape(batch, rows, 2, heads, HEAD_DIM).astype(F32)


def _window_rows(main, strided, main_runs, batch):
    t = main.shape[0] // batch
    out = [_kv_rows(main, _compact(main_runs, OFF_KA), main, _compact(main_runs, OFF_VA), batch,
                    A_KV_HEADS, min(BAND, t))]
    for g, (win, _) in enumerate(B_GROUPS):
        runs, arr = (main_runs, main) if g == 0 else (_STRIDED_RUNS, strided)
        out.append(_kv_rows(arr, _compact(runs, OFF_KB + g * B_WIDTH), arr, _compact(runs, OFF_VB + g * B_WIDTH),
                            batch, B_HEADS, min(win, t)))
    return out


def _kv_rows_kernel(*refs, pieces):
    k_refs, v_refs, o_ref = refs[:pieces], refs[pieces:2 * pieces], refs[-1]
    rows, _, heads, dh = o_ref.shape
    for slot, part_refs in enumerate((k_refs, v_refs)):
        flat = jnp.concatenate([r[...].astype(F32) for r in part_refs], axis=-1)
        o_ref[:, slot] = flat.reshape(rows, heads, dh)


def _kv_rows_into(stack, layer, depth, k_arr, k_col, v_arr, v_col, batch, heads, rows):
    w = heads * HEAD_DIM
    pw = min(w, TN)
    pieces = w // pw
    t = k_arr.shape[0] // batch
    tr = min(rows, 512)
    first, per_seq = (t - rows) // tr, t // tr

    def piece(col):
        return pl.BlockSpec((tr, pw), lambda b, i: (b * per_seq + first + i, col // pw))

    in_specs = ([piece(k_col + p * pw) for p in range(pieces)] + [piece(v_col + p * pw) for p in range(pieces)])
    args = [k_arr] * pieces + [v_arr] * pieces
    aliases = {}
    if stack is not None:
        in_specs.append(pl.BlockSpec(memory_space=pl.ANY))
        args.append(stack)
        aliases = {2 * pieces: 0}
    return pl.pallas_call(
        functools.partial(_kv_rows_kernel, pieces=pieces),
        grid=(batch, rows // tr),
        in_specs=in_specs,
        out_specs=pl.BlockSpec((None, None, tr, 2, heads, HEAD_DIM), lambda b, i: (layer, b, i, 0, 0, 0)),
        out_shape=jax.ShapeDtypeStruct((depth, batch, rows, 2, heads, HEAD_DIM), F32),
        input_output_aliases=aliases,
        compiler_params=_params("arbitrary", "arbitrary"),
        name="kv_rows",
    )(*args)


def _window_rows_into(stacks, layer, depth, main, main_runs, batch):
    t = main.shape[0] // batch
    stacks = stacks or [None, None]
    return [_kv_rows_into(stacks[0], layer, depth, main, _compact(main_runs, OFF_KA), main,
                          _compact(main_runs, OFF_VA), batch, A_KV_HEADS, min(BAND, t)),
            _kv_rows_into(stacks[1], layer, depth, main, _compact(main_runs, OFF_KB), main,
                          _compact(main_runs, OFF_VB), batch, B_HEADS, min(B_GROUPS[0][0], t))]


def _reference_order(main, strided, main_runs, upto):
    pieces = []
    for (lo, hi), src in sorted([(r, main) for r in main_runs] + [(r, strided) for r in _STRIDED_RUNS]):
        if lo >= upto:
            break
        runs = main_runs if src is main else _STRIDED_RUNS
        c = _compact(runs, lo)
        pieces.append(src[:, c:c + min(hi, upto) - lo])
    return jnp.concatenate(pieces, axis=-1)


def kernel(x_prompt, x_sample, cache_a, cache_b1, cache_b2, cache_b3, state_c, norm_w, w_in,
           w_branch_a, w_branch_b, w_branch_c, w_out, attn_sinks, ret_norm_w, final_norm_w):
    batch, seq, d_model = x_prompt.shape
    n_s, t_s, _ = x_sample.shape
    depth = w_in.shape[0]
    width = w_in.shape[-1]
    assert t_s == 1 and seq == B_GROUPS[-1][1] * BAND and n_s % SAMPLE_CHUNK == 0
    assert width == OFF_G + 3 * d_model and d_model % TN == 0
    assert cache_a.shape[2] == BAND
    assert all(c.shape[2] == win for c, (win, _) in zip((cache_b1, cache_b2, cache_b3), B_GROUPS))

    slopes_a = jnp.exp2(-8.0 * jnp.arange(1, A_HEADS + 1, dtype=F32) / A_HEADS)
    nbh = len(B_GROUPS) * B_HEADS
    slopes_b = jnp.exp2(-8.0 * jnp.arange(1, nbh + 1, dtype=F32) / nbh)
    log_gamma = jnp.log1p(-jnp.exp2(-5.0 - jnp.arange(C_HEADS, dtype=F32)))
    caches = (cache_a, cache_b1, cache_b2, cache_b3)
    main_runs = _main_runs(width)
    g_col = _compact(main_runs, OFF_G)

    xp = x_prompt.reshape(batch * seq, d_model)
    xs = x_sample.reshape(n_s, d_model)
    window_rows = [min(BAND, seq)] + [min(win, seq) for win, _ in B_GROUPS]
    window_heads = [A_KV_HEADS] + [B_HEADS] * len(B_GROUPS)
    p_window = [jnp.zeros((depth, batch, r, 2, h, HEAD_DIM), F32) for r, h in zip(window_rows, window_heads)]
    s_c = jnp.zeros((depth, n_s, C_HEADS, C_QK_DIM, C_V_DIM), F32)
    p_c, s_states = [], []
    for l in range(depth):
        nw = norm_w[l].reshape(1, d_model)
        rw = ret_norm_w[l].reshape(1, C_WIDTH)
        sinks = attn_sinks[l]

        hp = _rmsnorm(xp, nw, BF16)
        hsm = _rmsnorm(xs, nw, BF16)
        main, main_s = _matmul(hp, hsm, w_in, l, BF16, runs=main_runs, name="proj_main")
        strided_states = tuple(
            (p_window[1 + g], depth,
             _compact(_STRIDED_RUNS, OFF_KB + g * B_WIDTH) // WIDE, _compact(_STRIDED_RUNS, OFF_VB + g * B_WIDTH) // WIDE,
             min(B_GROUPS[g][0], seq)) for g in (1, 2))
        strided, strided_s, *strided_stacks = _matmul(hp, hsm, w_in, l, F32, runs=_STRIDED_RUNS,
                                                      states=strided_states, seq=seq, name="proj_strided")

        ya = _attn_a(main, slopes_a, sinks, main_runs, batch, seq)
        yb = _attn_b(main, strided, slopes_b, main_runs, batch, seq)
        yc, new_c = _retention(main, log_gamma, rw, main_runs, batch, seq)
        y_s, s_c = _sample_mixers(_reference_order(main_s, strided_s, main_runs, OFF_G), caches, state_c, s_c, l,
                                  log_gamma, slopes_a, slopes_b, sinks, rw)

        mixed, mixed_s = _branch_merge(ya, yb, yc, y_s, w_branch_a, w_branch_b, w_branch_c,
                                       main, main_s, g_col, l, d_model)
        xp, xs = _matmul(mixed, mixed_s, w_out, l, F32, residual=(xp, xs), name="proj_out")
        p_window = _window_rows_into(p_window, l, depth, main, main_runs, batch) + strided_stacks
        p_c.append(new_c)
        s_states.append(_window_rows(main_s, strided_s, main_runs, n_s))

    fw = final_norm_w.reshape(1, d_model)
    y_prompt = _rmsnorm(xp, fw, F32).reshape(batch, seq, d_model)
    y_sample = _rmsnorm(xs, fw, F32).reshape(n_s, 1, d_model)
    sample = [jnp.stack([s[i] for s in s_states]) for i in range(4)]
    return (y_prompt, y_sample, *p_window, jnp.stack(p_c), *sample, s_c)
```

```python
import functools

import jax
import jax.numpy as jnp
from jax import lax
from jax.experimental import pallas as pl
from jax.experimental.pallas import tpu as pltpu

F32 = jnp.float32
BF16 = jnp.bfloat16

HEAD_DIM = 128
BAND = 128
NORM_EPS = 1e-6
A_HEADS = 8
A_KV_HEADS = 2
A_GROUP = A_HEADS // A_KV_HEADS
B_GROUPS = ((128, 1), (512, 4), (2048, 16))
B_HEADS = 8
C_HEADS = 6
C_QK_DIM = 128
C_V_DIM = 256
C_CHUNK = 128

A_WIDTH = A_HEADS * HEAD_DIM
A_KV_WIDTH = A_KV_HEADS * HEAD_DIM
B_WIDTH = B_HEADS * HEAD_DIM
B_QKV_WIDTH = len(B_GROUPS) * B_WIDTH
C_QK_WIDTH = C_HEADS * C_QK_DIM
C_WIDTH = C_HEADS * C_V_DIM
Y_WIDTH = A_WIDTH + B_WIDTH + C_WIDTH

OFF_QA = 0
OFF_KA = OFF_QA + A_WIDTH
OFF_VA = OFF_KA + A_KV_WIDTH
OFF_ZA = OFF_VA + A_KV_WIDTH
OFF_QB = OFF_ZA + A_WIDTH
OFF_KB = OFF_QB + B_QKV_WIDTH
OFF_VB = OFF_KB + B_QKV_WIDTH
OFF_ZB = OFF_VB + B_QKV_WIDTH
OFF_QC = OFF_ZB + B_WIDTH
OFF_KC = OFF_QC + C_QK_WIDTH
OFF_VC = OFF_KC + C_QK_WIDTH
OFF_ZC = OFF_VC + C_WIDTH
OFF_G = OFF_ZC + C_WIDTH

TN = 512
PART = 256
PARTS = 4
WIDE = PARTS * PART
NEG_BIG = -1e30
SCALE = HEAD_DIM ** -0.5
LOG2E = 1.4426950408889634
assert BAND == HEAD_DIM
VMEM_LIMIT = 56 * 1024 * 1024
SAMPLE_CHUNK = 2
B_INFLIGHT = 8
A_INFLIGHT = 4
C_INFLIGHT = 8

_STRIDED_RUNS = tuple((off + B_WIDTH, off + B_QKV_WIDTH) for off in (OFF_QB, OFF_KB, OFF_VB))


def _main_runs(width):
    edges = [0] + [e for run in _STRIDED_RUNS for e in run] + [width]
    return tuple((edges[i], edges[i + 1]) for i in range(0, len(edges), 2))


def _compact(runs, col):
    base = 0
    for lo, hi in runs:
        if lo <= col < hi:
            return base + col - lo
        base += hi - lo
    raise ValueError(col)


def _block_map(runs):
    starts, gaps, pos, prev_hi = [], [], 0, 0
    for lo, hi in runs:
        starts.append(pos // PART)
        gaps.append((lo - prev_hi) // PART)
        pos += hi - lo
        prev_hi = hi

    def col_block(j):
        out = j
        for s, g in zip(starts, gaps):
            if g:
                out = out + jnp.where(j >= s, g, 0)
        return out

    return col_block, pos // PART


def _params(*sem):
    return pltpu.CompilerParams(dimension_semantics=sem, vmem_limit_bytes=VMEM_LIMIT)


def _smem():
    return pl.BlockSpec(memory_space=pltpu.SMEM)


def _silu(z):
    return z * jax.nn.sigmoid(z)


def _rmsnorm_kernel(x_ref, w_ref, o_ref):
    x = x_ref[...].astype(F32)
    ms = jnp.mean(x * x, axis=-1, keepdims=True)
    o_ref[...] = (x * lax.rsqrt(ms + NORM_EPS) * w_ref[...].astype(F32)).astype(o_ref.dtype)


def _rmsnorm(x, w_row, out_dtype):
    m, d = x.shape
    tr = min(m, 512)
    return pl.pallas_call(
        _rmsnorm_kernel,
        grid=(m // tr,),
        in_specs=[pl.BlockSpec((tr, d), lambda i: (i, 0)),
                  pl.BlockSpec((1, d), lambda i: (0, 0))],
        out_specs=pl.BlockSpec((tr, d), lambda i: (i, 0)),
        out_shape=jax.ShapeDtypeStruct((m, d), out_dtype),
        compiler_params=_params("arbitrary"),
        name="rmsnorm",
    )(x, w_row)


def _matmul_kernel(x_ref, xs_ref, w_ref, *rest, has_residual, row_steps, n_stacks, states, steps_per_seq):
    rest = list(rest)
    r_ref, rs_ref = (rest.pop(0), rest.pop(0)) if has_residual else (None, None)
    del rest[:n_stacks]
    o_ref, os_ref, *st_refs, wb_ref = rest
    j = pl.program_id(0)
    i = pl.program_id(1)
    steps_per_part = row_steps // PARTS
    kq = w_ref.shape[0] // steps_per_part
    tm = x_ref.shape[0]

    def stage():
        rows = pl.ds(pl.multiple_of((i % steps_per_part) * kq, kq), kq)
        wb_ref[j % 2, i // steps_per_part, rows, :] = w_ref[rows, :].astype(BF16)

    def multiply(with_sample, st_ref):
        lhs = x_ref[...]
        targets = [(slice(0, tm), r_ref, o_ref)]
        if with_sample:
            lhs = jnp.concatenate([lhs, xs_ref[...]], axis=0)
            targets.append((slice(tm, None), rs_ref, os_ref))
        for part in range(PARTS):
            cols = slice(part * PART, (part + 1) * PART)
            acc = jnp.dot(lhs, wb_ref[(j + 1) % 2, part], preferred_element_type=F32)
            for rows, res_ref, out_ref in targets:
                out = acc[rows]
                if has_residual:
                    out = out + res_ref[:, cols]
                out_ref[:, cols] = out.astype(out_ref.dtype)
            if st_ref is not None:
                tr, _, dh = st_ref.shape
                hp = PART // dh
                st_ref[:, part * hp:(part + 1) * hp, :] = acc[tm - tr:tm].reshape(tr, hp, dh)

    @pl.when(j == 0)
    def _():
        stage()

    feeds = [jnp.logical_and(jnp.logical_or(j == k_block + 1, j == v_block + 1),
                             i % steps_per_seq >= first_active) for k_block, v_block, first_active in states]
    feeds_none = functools.reduce(jnp.logical_and, [jnp.logical_not(f) for f in feeds], j > 0)
    for with_sample in (True, False):
        rows_match = (i == 0) if with_sample else (i > 0)
        for feed, st_ref in [(feeds_none, None)] + list(zip(feeds, st_refs)):
            @pl.when(jnp.logical_and(jnp.logical_and(j > 0, rows_match), feed))
            def _(with_sample=with_sample, st_ref=st_ref):
                multiply(with_sample, st_ref)
                stage()


def _matmul(x, xs, w_stack, layer, out_dtype, runs=None, residual=None, states=(), seq=None, name="matmul"):
    m, k = x.shape
    ms = xs.shape[0]
    tm = min(m // PARTS, 1024 if jnp.dtype(out_dtype).itemsize == 2 else 512)
    row_steps = m // tm
    col_block, nparts = _block_map(runs if runs is not None else ((0, w_stack.shape[-1]),))
    nblocks = nparts // PARTS
    steps_per_part = row_steps // PARTS
    assert nparts % PARTS == 0 and row_steps % PARTS == 0 and k % steps_per_part == 0

    def row_map(j, i):
        return jnp.where(j == 0, 0, i)

    def out_map(j, i):
        return (row_map(j, i), jnp.maximum(j - 1, 0))

    def w_map(j, i):
        return (layer, 0, col_block(PARTS * jnp.minimum(j, nblocks - 1) + i // steps_per_part))

    in_specs = [pl.BlockSpec((tm, k), lambda j, i: (row_map(j, i), 0)),
                pl.BlockSpec((ms, k), lambda j, i: (0, 0)),
                pl.BlockSpec((None, k, PART), w_map)]
    args = [x, xs, w_stack]
    if residual is not None:
        in_specs += [pl.BlockSpec((tm, WIDE), out_map),
                     pl.BlockSpec((ms, WIDE), lambda j, i: (0, jnp.maximum(j - 1, 0)))]
        args += list(residual)
    out_specs = [pl.BlockSpec((tm, WIDE), out_map),
                 pl.BlockSpec((ms, WIDE), lambda j, i: (0, jnp.maximum(j - 1, 0)))]
    out_shape = [jax.ShapeDtypeStruct((m, nblocks * WIDE), out_dtype),
                 jax.ShapeDtypeStruct((ms, nblocks * WIDE), F32)]

    heads = WIDE // HEAD_DIM
    steps_per_seq = (seq // tm) if states else 1
    fed_blocks = [b for _, _, k_block, v_block, _ in states for b in (k_block, v_block)]
    assert len(set(fed_blocks)) == len(fed_blocks)
    assert not states or (residual is None and out_dtype == F32)
    kernel_states, aliases, n_stacks = [], {}, 0
    for n, (stack, depth, k_block, v_block, rows) in enumerate(states):
        batch = m // seq
        assert seq % tm == 0 and (rows % tm == 0 or rows < tm) and k_block < v_block
        n_active = max(rows // tm, 1)
        first_active = steps_per_seq - n_active
        kernel_states.append((k_block, v_block, first_active))

        def st_map(j, i, kb=k_block + 1, vb=v_block + 1, first_active=first_active, n_active=n_active, batch=batch):
            s, b = i % steps_per_seq, i // steps_per_seq
            started = s >= first_active
            b_here = jnp.maximum(jnp.where(started, b, b - 1), 0)
            r_here = jnp.where(jnp.logical_and(jnp.logical_not(started), b > 0), n_active - 1,
                               jnp.maximum(s - first_active, 0))
            writing = jnp.logical_or(j == kb, j == vb)
            done = jnp.logical_or(jnp.logical_and(j > kb, j < vb), j > vb)
            b_idx = jnp.where(writing, b_here, jnp.where(done, batch - 1, 0))
            r_idx = jnp.where(writing, r_here, jnp.where(done, n_active - 1, 0))
            return (layer, b_idx, r_idx, jnp.where(j >= vb, 1, 0), 0, 0)

        out_specs.append(pl.BlockSpec((None, None, min(rows, tm), None, heads, HEAD_DIM), st_map))
        out_shape.append(jax.ShapeDtypeStruct((depth, batch, rows, 2, heads, HEAD_DIM), F32))
        if stack is not None:
            in_specs.append(pl.BlockSpec(memory_space=pl.ANY))
            aliases[len(args)] = 2 + n
            args.append(stack)
            n_stacks += 1
    return pl.pallas_call(
        functools.partial(_matmul_kernel, has_residual=residual is not None, row_steps=row_steps,
                          n_stacks=n_stacks, states=tuple(kernel_states), steps_per_seq=steps_per_seq),
        grid=(nblocks + 1, row_steps),
        in_specs=in_specs,
        out_specs=out_specs,
        out_shape=out_shape,
        input_output_aliases=aliases,
        scratch_shapes=[pltpu.VMEM((2, PARTS, k, PART), BF16)],
        compiler_params=_params("arbitrary", "arbitrary"),
        name=name,
    )(*args)


def _branch_kernel(ya_ref, yb_ref, yc_ref, ys_ref, wa_ref, wb_ref, wc_ref, ga_ref, gb_ref, gc_ref,
                   gsa_ref, gsb_ref, gsc_ref, o_ref, os_ref, wa_s, wb_s, wc_s, *, row_steps):
    j = pl.program_id(0)
    i = pl.program_id(1)
    steps_per_part = row_steps // PARTS
    staged = ((wa_ref, wa_s), (wb_ref, wb_s), (wc_ref, wc_s))

    def stage():
        for w_ref, w_s in staged:
            kq = w_ref.shape[0] // steps_per_part
            rows = pl.ds(pl.multiple_of((i % steps_per_part) * kq, kq), kq)
            w_s[j % 2, i // steps_per_part, rows, :] = w_ref[rows, :].astype(BF16)

    def merge(with_sample):
        tm = ya_ref.shape[0]
        lhs = [ya_ref[...], yb_ref[...], yc_ref[...]]
        targets = [(slice(0, tm), (ga_ref, gb_ref, gc_ref), o_ref)]
        if with_sample:
            ys = ys_ref[...].astype(BF16)
            bounds = (0, A_WIDTH, A_WIDTH + B_WIDTH, Y_WIDTH)
            lhs = [jnp.concatenate([y, ys[:, lo:hi]], axis=0) for y, lo, hi in zip(lhs, bounds, bounds[1:])]
            targets.append((slice(tm, None), (gsa_ref, gsb_ref, gsc_ref), os_ref))
        for part in range(PARTS):
            cols = slice(part * PART, (part + 1) * PART)
            products = [jnp.dot(y, w_s[(j + 1) % 2, part], preferred_element_type=F32)
                        for y, (_, w_s) in zip(lhs, staged)]
            for rows, g_refs, out_ref in targets:
                mixed = None
                for u, g_ref in zip(products, g_refs):
                    term = jax.nn.sigmoid(g_ref[:, cols].astype(F32)) * u[rows]
                    mixed = term if mixed is None else mixed + term
                out_ref[:, cols] = mixed.astype(out_ref.dtype)

    @pl.when(j == 0)
    def _():
        stage()

    @pl.when(jnp.logical_and(j > 0, i == 0))
    def _():
        merge(True)
        stage()

    @pl.when(jnp.logical_and(j > 0, i > 0))
    def _():
        merge(False)
        stage()


def _branch_merge(ya, yb, yc, y_s, w_a, w_b, w_c, main, main_s, g_col, layer, d_model):
    m = ya.shape[0]
    ms = y_s.shape[0]
    tm = min(m // PARTS, 512)
    row_steps = m // tm
    steps_per_part = row_steps // PARTS
    nblocks = d_model // WIDE
    assert row_steps % PARTS == 0 and g_col % WIDE == 0 and d_model % WIDE == 0
    assert all(w % steps_per_part == 0 for w in (A_WIDTH, B_WIDTH, C_WIDTH))

    def row_map(j, i):
        return jnp.where(j == 0, 0, i)

    def col_map(j):
        return jnp.maximum(j - 1, 0)

    def y_spec(width):
        return pl.BlockSpec((tm, width), lambda j, i: (row_map(j, i), 0))

    def w_spec(width):
        return pl.BlockSpec((None, width, PART),
                            lambda j, i: (layer, 0, PARTS * jnp.minimum(j, nblocks - 1) + i // steps_per_part))

    def g_spec(rows, idx, whole):
        first = (g_col + idx * d_model) // WIDE
        if whole:
            return pl.BlockSpec((rows, WIDE), lambda j, i: (0, first + col_map(j)))
        return pl.BlockSpec((rows, WIDE), lambda j, i: (row_map(j, i), first + col_map(j)))

    return pl.pallas_call(
        functools.partial(_branch_kernel, row_steps=row_steps),
        grid=(nblocks + 1, row_steps),
        in_specs=[y_spec(A_WIDTH), y_spec(B_WIDTH), y_spec(C_WIDTH),
                  pl.BlockSpec((ms, Y_WIDTH), lambda j, i: (0, 0)),
                  w_spec(A_WIDTH), w_spec(B_WIDTH), w_spec(C_WIDTH),
                  g_spec(tm, 0, False), g_spec(tm, 1, False), g_spec(tm, 2, False),
                  g_spec(ms, 0, True), g_spec(ms, 1, True), g_spec(ms, 2, True)],
        out_specs=[pl.BlockSpec((tm, WIDE), lambda j, i: (row_map(j, i), col_map(j))),
                   pl.BlockSpec((ms, WIDE), lambda j, i: (0, col_map(j)))],
        out_shape=[jax.ShapeDtypeStruct((m, d_model), BF16),
                   jax.ShapeDtypeStruct((ms, d_model), BF16)],
        scratch_shapes=[pltpu.VMEM((2, PARTS, w, PART), BF16) for w in (A_WIDTH, B_WIDTH, C_WIDTH)],
        compiler_params=_params("arbitrary", "arbitrary"),
        name="branch_merge",
    )(ya, yb, yc, y_s, w_a, w_b, w_c, main, main, main, main_s, main_s, main_s)


def _band_bias(slope_step, rows_per_tile=1):
    shape = (rows_per_tile * BAND, BAND)
    row = lax.broadcasted_iota(jnp.int32, shape, 0) % BAND
    col = lax.broadcasted_iota(jnp.int32, shape, 1)
    dist = (row - col).astype(F32)
    cur = jnp.where(col <= row, -(slope_step * LOG2E) * dist, NEG_BIG)
    prev = jnp.where(col >= row, -(slope_step * LOG2E) * (dist + float(BAND)), NEG_BIG)
    return cur, prev


def _block_rows(n):
    start = n * BAND if isinstance(n, int) else pl.multiple_of(n * BAND, BAND)
    return pl.ds(start, BAND)


def _qk(q, k):
    return lax.dot_general(q, k, (((1,), (1,)), ((), ())), preferred_element_type=F32) * (SCALE * LOG2E)


def _band_attention(blocks, sink=None, carry=None):
    logits = []
    for q, k_cur, _, bias_cur, k_prev, _, bias_prev in blocks:
        logit_p = None if k_prev is None else _qk(q, k_prev) + bias_prev
        logits.append((_qk(q, k_cur) + bias_cur, logit_p))
    probs = []
    for n, (logit_c, logit_p) in enumerate(logits):
        m = jnp.max(logit_c, axis=-1, keepdims=True)
        if logit_p is not None:
            m = jnp.maximum(m, jnp.max(logit_p, axis=-1, keepdims=True))
        if sink is not None:
            m = jnp.maximum(m, sink)
        m_tile = jnp.broadcast_to(m, logit_c.shape)
        alpha = None
        if carry is not None:
            m_tile = jnp.maximum(m_tile, carry[n][1])
            alpha = jnp.exp2(carry[n][1] - m_tile)
        e_c = jnp.exp2(logit_c - m_tile)
        den = jnp.sum(e_c, axis=-1, keepdims=True)
        e_p = None
        if logit_p is not None:
            e_p = jnp.exp2(logit_p - m_tile)
            den = den + jnp.sum(e_p, axis=-1, keepdims=True)
            e_p = e_p.astype(BF16)
        if sink is not None:
            den = den + jnp.exp2(sink - m)
        den_tile = jnp.broadcast_to(den, logit_c.shape)
        if carry is not None:
            den_tile = den_tile + carry[n][2] * alpha
        probs.append((e_c.astype(BF16), e_p, m_tile, den_tile, alpha))
    outs = []
    for n, ((_, _, v_cur, _, _, v_prev, _), (e_c, e_p, m, den, alpha)) in enumerate(zip(blocks, probs)):
        acc = jnp.dot(e_c, v_cur, preferred_element_type=F32)
        if e_p is not None:
            acc = acc + jnp.dot(e_p, v_prev, preferred_element_type=F32)
        if carry is not None:
            acc = acc + carry[n][0] * alpha
        outs.append((acc, m, den))
    return outs


def _attn_a_kernel(slopes_ref, sinks_ref, q_ref, k_ref, v_ref, z_ref, y_ref, bias_s, sink_s):
    kv = pl.program_id(1)
    rows = A_GROUP * BAND
    head = lax.broadcasted_iota(jnp.int32, (rows, 1), 0) // BAND
    slope_col = jnp.zeros((rows, 1), F32)
    sink_col = jnp.zeros((rows, 1), F32)
    for g in range(A_GROUP):
        slope_col = jnp.where(head == g, slopes_ref[kv * A_GROUP + g], slope_col)
        sink_col = jnp.where(head == g, sinks_ref[kv * A_GROUP + g], sink_col)
    bias_s[0], bias_s[1] = _band_bias(slope_col, A_GROUP)
    sink_s[...] = sink_col * LOG2E
    nb = q_ref.shape[0] // BAND

    def blocks(first, first_has_prev):
        rows_ = [_block_rows(first + j) for j in range(A_INFLIGHT)]
        kv_ = [(k_ref[rw, :], v_ref[rw, :]) for rw in rows_]
        if first_has_prev:
            before = _block_rows(first - 1)
            kv_before = (k_ref[before, :], v_ref[before, :])
        work = []
        for j, rw in enumerate(rows_):
            q = jnp.concatenate([q_ref[rw, g * HEAD_DIM:(g + 1) * HEAD_DIM] for g in range(A_GROUP)], axis=0)
            if j == 0 and not first_has_prev:
                work.append((q, *kv_[j], bias_s[0], None, None, None))
            else:
                work.append((q, *kv_[j], bias_s[0], *(kv_[j - 1] if j else kv_before), bias_s[1]))
        zs = [[z_ref[rw, g * HEAD_DIM:(g + 1) * HEAD_DIM].astype(F32) for g in range(A_GROUP)] for rw in rows_]
        outs = []
        for (acc, _, den), z in zip(_band_attention(work, sink=sink_s[...]), zs):
            o = acc / den
            outs.append([(o[g * BAND:(g + 1) * BAND] * _silu(z[g])).astype(y_ref.dtype) for g in range(A_GROUP)])
        for rw, ys in zip(rows_, outs):
            for g in range(A_GROUP):
                y_ref[rw, g * HEAD_DIM:(g + 1) * HEAD_DIM] = ys[g]

    blocks(0, False)

    def body(t, carry):
        blocks(t * A_INFLIGHT, True)
        return carry

    lax.fori_loop(1, nb // A_INFLIGHT, body, 0)


def _attn_a(main, slopes, sinks, main_runs, batch, seq):
    gw = A_GROUP * HEAD_DIM

    def spec(off, width):
        c = _compact(main_runs, off) // width
        return pl.BlockSpec((seq, width), lambda b, kv: (b, c + kv))

    return pl.pallas_call(
        _attn_a_kernel,
        grid=(batch, A_KV_HEADS),
        in_specs=[_smem(), _smem(), spec(OFF_QA, gw), spec(OFF_KA, HEAD_DIM), spec(OFF_VA, HEAD_DIM),
                  spec(OFF_ZA, gw)],
        out_specs=pl.BlockSpec((seq, gw), lambda b, kv: (b, kv)),
        out_shape=jax.ShapeDtypeStruct((batch * seq, A_WIDTH), BF16),
        scratch_shapes=[pltpu.VMEM((2, A_GROUP * BAND, BAND), F32), pltpu.VMEM((A_GROUP * BAND, 1), F32)],
        compiler_params=_params("arbitrary", "arbitrary"),
        name="attn_a",
    )(slopes, sinks, main, main, main, main)


def _attn_b_kernel(slopes_ref, q1, k1, v1, q2, k2, v2, q3, k3, v3, z_ref, y_ref, acc_s, m_s, l_s, bias_s):
    h = pl.program_id(1)
    seq = y_ref.shape[0]
    for g, (_, dil) in enumerate(B_GROUPS):
        bias_s[2 * g], bias_s[2 * g + 1] = _band_bias(slopes_ref[g * B_HEADS + h] * float(dil))

    def load_state(rows):
        return [(acc_s[rw, :], m_s[rw, :], l_s[rw, :]) for rw in rows]

    def store_state(rows, outs):
        for rw, (acc, m, den) in zip(rows, outs):
            acc_s[rw, :] = acc
            m_s[rw, :] = m
            l_s[rw, :] = den


    dil3 = B_GROUPS[2][1]
    assert seq // dil3 == BAND

    def g3_body(t, carry):
        rows = [pl.ds(t * B_INFLIGHT + j, BAND, stride=dil3) for j in range(B_INFLIGHT)]
        qkv = [tuple(a[rw, :].astype(BF16) for a in (q3, k3, v3)) for rw in rows]
        store_state(rows, _band_attention([(q, k, v, bias_s[4], None, None, None) for q, k, v in qkv]))
        return carry

    lax.fori_loop(0, dil3 // B_INFLIGHT, g3_body, 0)

    dil2 = B_GROUPS[1][1]
    nb2 = seq // dil2 // BAND
    per_body = max(1, B_INFLIGHT // nb2)

    def g2_body(t, carry):
        rows = [pl.ds(t * per_body + i + n * BAND * dil2, BAND, stride=dil2)
                for i in range(per_body) for n in range(nb2)]
        qkv = [tuple(a[rw, :].astype(BF16) for a in (q2, k2, v2)) for rw in rows]
        work = [(q, k, v, bias_s[2], None, None, None) if j % nb2 == 0 else
                (q, k, v, bias_s[2], qkv[j - 1][1], qkv[j - 1][2], bias_s[3]) for j, (q, k, v) in enumerate(qkv)]
        store_state(rows, _band_attention(work, carry=load_state(rows)))
        return carry

    lax.fori_loop(0, dil2 // per_body, g2_body, 0)

    def g1_blocks(first, first_has_prev):
        rows = [_block_rows(first + j) for j in range(B_INFLIGHT)]
        kv = [(k1[rw, :], v1[rw, :]) for rw in rows]
        if first_has_prev:
            before = _block_rows(first - 1)
            kv_before = (k1[before, :], v1[before, :])
        qs = [q1[rw, :] for rw in rows]
        zs = [z_ref[rw, :].astype(F32) for rw in rows]
        work = [(qs[j], *kv[j], bias_s[0], None, None, None) if j == 0 and not first_has_prev else
                (qs[j], *kv[j], bias_s[0], *(kv[j - 1] if j else kv_before), bias_s[1]) for j in range(B_INFLIGHT)]
        outs = [((acc / den) * _silu(z)).astype(y_ref.dtype)
                for (acc, _, den), z in zip(_band_attention(work, carry=load_state(rows)), zs)]
        for rw, y in zip(rows, outs):
            y_ref[rw, :] = y

    g1_blocks(0, False)

    def g1_body(t, carry):
        g1_blocks(t * B_INFLIGHT, True)
        return carry

    lax.fori_loop(1, seq // BAND // B_INFLIGHT, g1_body, 0)


def _attn_b(main, strided, slopes_b, main_runs, batch, seq):
    def spec(runs, off):
        c = _compact(runs, off) // HEAD_DIM
        return pl.BlockSpec((seq, HEAD_DIM), lambda b, h: (b, c + h))

    in_specs = [_smem()]
    args = [slopes_b]
    for g in range(len(B_GROUPS)):
        runs, arr = (main_runs, main) if g == 0 else (_STRIDED_RUNS, strided)
        in_specs += [spec(runs, off + g * B_WIDTH) for off in (OFF_QB, OFF_KB, OFF_VB)]
        args += [arr] * 3
    in_specs.append(spec(main_runs, OFF_ZB))
    args.append(main)
    return pl.pallas_call(
        _attn_b_kernel,
        grid=(batch, B_HEADS),
        in_specs=in_specs,
        out_specs=pl.BlockSpec((seq, HEAD_DIM), lambda b, h: (b, h)),
        out_shape=jax.ShapeDtypeStruct((batch * seq, B_WIDTH), BF16),
        scratch_shapes=[pltpu.VMEM((seq, HEAD_DIM), F32)] * 3 + [pltpu.VMEM((2 * len(B_GROUPS), BAND, BAND), F32)],
        compiler_params=_params("arbitrary", "arbitrary"),
        name="attn_b",
    )(*args)


def _head_norm_gate(o, w_row, z):
    mu = jnp.mean(o, axis=-1, keepdims=True)
    var = jnp.mean(jnp.square(o - mu), axis=-1, keepdims=True)
    y = (o - mu) * lax.rsqrt(var + NORM_EPS) * w_row
    return y * _silu(z)


def _retention_kernel(lg_ref, q_ref, k_ref, v_ref, z_ref, w_ref, y_ref, s_ref, decay_s):
    lg = lg_ref[pl.program_id(1)]
    row = lax.broadcasted_iota(jnp.int32, (C_CHUNK, C_CHUNK), 0)
    col = lax.broadcasted_iota(jnp.int32, (C_CHUNK, C_CHUNK), 1)
    diff = (row - col).astype(F32)
    decay_s[...] = jnp.where(diff >= 0, jnp.exp(lg * jnp.maximum(diff, 0.0)), 0.0)
    pos = lax.broadcasted_iota(jnp.int32, (C_CHUNK, 1), 0).astype(F32)
    q_dec = jnp.exp(lg * (pos + 1.0))
    k_dec = jnp.exp(lg * (float(C_CHUNK) - 1.0 - pos))
    chunk_decay = jnp.exp(lg * float(C_CHUNK))
    w_row = w_ref[...].astype(F32)

    def body(t, state):
        rows = [pl.ds(pl.multiple_of((t * C_INFLIGHT + j) * C_CHUNK, C_CHUNK), C_CHUNK) for j in range(C_INFLIGHT)]
        qs = [q_ref[rw, :].astype(F32) for rw in rows]
        ks = [k_ref[rw, :].astype(F32) * (C_QK_DIM ** -0.5) for rw in rows]
        vs = [v_ref[rw, :] for rw in rows]
        zs = [z_ref[rw, :].astype(F32) for rw in rows]
        attn = [lax.dot_general(q.astype(BF16), k.astype(BF16), (((1,), (1,)), ((), ())),
                                preferred_element_type=F32) for q, k in zip(qs, ks)]
        update = [jnp.dot(jnp.transpose(k * k_dec).astype(BF16), v, preferred_element_type=F32)
                  for k, v in zip(ks, vs)]
        intra = [jnp.dot((a * decay_s[...]).astype(BF16), v, preferred_element_type=F32)
                 for a, v in zip(attn, vs)]
        states = [state]
        for u in update:
            states.append(chunk_decay * states[-1] + u)
        cross = [jnp.dot((q * q_dec).astype(BF16), s0.astype(BF16), preferred_element_type=F32)
                 for q, s0 in zip(qs, states)]
        outs = [_head_norm_gate(i + c, w_row, z).astype(y_ref.dtype) for i, c, z in zip(intra, cross, zs)]
        for rw, y in zip(rows, outs):
            y_ref[rw, :] = y
        return states[-1]

    s_ref[...] = lax.fori_loop(0, q_ref.shape[0] // C_CHUNK // C_INFLIGHT, body,
                               jnp.zeros((C_QK_DIM, C_V_DIM), F32))


def _retention(main, log_gamma, ret_w_row, main_runs, batch, seq):
    def spec(off, width):
        c = _compact(main_runs, off) // width
        return pl.BlockSpec((seq, width), lambda b, h: (b, c + h))

    return pl.pallas_call(
        _retention_kernel,
        grid=(batch, C_HEADS),
        in_specs=[_smem(), spec(OFF_QC, C_QK_DIM), spec(OFF_KC, C_QK_DIM), spec(OFF_VC, C_V_DIM),
                  spec(OFF_ZC, C_V_DIM), pl.BlockSpec((1, C_V_DIM), lambda b, h: (0, h))],
        out_specs=[pl.BlockSpec((seq, C_V_DIM), lambda b, h: (b, h)),
                   pl.BlockSpec((None, None, C_QK_DIM, C_V_DIM), lambda b, h: (b, h, 0, 0))],
        out_shape=[jax.ShapeDtypeStruct((batch * seq, C_WIDTH), BF16),
                   jax.ShapeDtypeStruct((batch, C_HEADS, C_QK_DIM, C_V_DIM), F32)],
        scratch_shapes=[pltpu.VMEM((C_CHUNK, C_CHUNK), F32)],
        compiler_params=_params("arbitrary", "arbitrary"),
        name="retention",
    )(log_gamma, main, main, main, main, ret_w_row)


def _lane_sum(x):
    shape = x.shape
    flat = x.reshape(-1, shape[-1])
    hi = flat.astype(BF16)
    lo = (flat - hi.astype(F32)).astype(BF16)
    ones = jnp.ones((shape[-1], shape[-1]), BF16)
    total = jnp.dot(hi, ones, preferred_element_type=F32) + jnp.dot(lo, ones, preferred_element_type=F32)
    return total.reshape(shape)


def _sample_attend(q, k_buf, v_buf, k_new, v_new, slope, sink=None):
    steps_back = float(BAND) - lax.broadcasted_iota(jnp.int32, (1,) + k_buf.shape[1:], 1).astype(F32)
    logit_b = _lane_sum(k_buf * q) * SCALE - slope * steps_back
    logit_n = _lane_sum(k_new * q) * SCALE
    m = jnp.maximum(jnp.max(logit_b, axis=1, keepdims=True), logit_n)
    if sink is not None:
        m = jnp.maximum(m, sink)
    e_b = jnp.exp(logit_b - m)
    e_n = jnp.exp(logit_n - m)
    den = jnp.sum(e_b, axis=1, keepdims=True) + e_n
    if sink is not None:
        den = den + jnp.exp(sink - m)
    acc = jnp.sum(e_b * v_buf, axis=1, keepdims=True) + e_n * v_new
    return acc, m, den


def _column(row_vec):
    n = row_vec.shape[-1]
    eye = lax.broadcasted_iota(jnp.int32, (n, n), 0) == lax.broadcasted_iota(jnp.int32, (n, n), 1)
    return jnp.sum(jnp.where(eye, row_vec, 0.0), axis=-1, keepdims=True)


def _sample_kernel(lg_ref, sa_ref, sb_ref, sink_ref, hs_ref, p_ref, ca_ref, cb1_ref, cb2_ref, cb3_ref,
                   st_ref, w_ref, *rest):
    yab_ref, yc_ref, st_out = rest[-3:]

    def heads(off, count):
        u = off // HEAD_DIM
        return hs_ref[:, u:u + count, :][:, None]

    def per_query_head(x):
        return jnp.concatenate([jnp.broadcast_to(x[..., kv:kv + 1, :], x.shape[:-2] + (A_GROUP, x.shape[-1]))
                                for kv in range(A_KV_HEADS)], axis=-2)

    acc, _, den = _sample_attend(
        heads(OFF_QA, A_HEADS), per_query_head(ca_ref[:, :, 0]), per_query_head(ca_ref[:, :, 1]),
        per_query_head(heads(OFF_KA, A_KV_HEADS)), per_query_head(heads(OFF_VA, A_KV_HEADS)),
        sa_ref[...][None, None], sink_ref[...][None, None])
    yab_ref[:, :A_HEADS, :] = ((acc / den) * _silu(heads(OFF_ZA, A_HEADS)))[:, 0]

    parts = []
    for g, (cache, (_, dil)) in enumerate(zip((cb1_ref, cb2_ref, cb3_ref), B_GROUPS)):
        off = g * B_WIDTH
        parts.append(_sample_attend(
            heads(OFF_QB + off, B_HEADS), cache[:, :, 0], cache[:, :, 1],
            heads(OFF_KB + off, B_HEADS), heads(OFF_VB + off, B_HEADS),
            sb_ref[g * B_HEADS:(g + 1) * B_HEADS, :][None, None] * float(dil)))
    m_all = jnp.maximum(jnp.maximum(parts[0][1], parts[1][1]), parts[2][1])
    num = sum(acc * jnp.exp(m - m_all) for acc, m, _ in parts)
    den = sum(d * jnp.exp(m - m_all) for _, m, d in parts)
    yab_ref[:, A_HEADS:A_HEADS + B_HEADS, :] = ((num / den) * _silu(heads(OFF_ZB, B_HEADS)))[:, 0]

    for h in range(C_HEADS):
        gamma = jnp.exp(lg_ref[h])
        vo = h * C_V_DIM
        for i in range(SAMPLE_CHUNK):
            q = p_ref[i, :, OFF_QC + h * C_QK_DIM:OFF_QC + (h + 1) * C_QK_DIM]
            k = p_ref[i, :, OFF_KC + h * C_QK_DIM:OFF_KC + (h + 1) * C_QK_DIM] * (C_QK_DIM ** -0.5)
            v = p_ref[i, :, OFF_VC + vo:OFF_VC + vo + C_V_DIM]
            z = p_ref[i, :, OFF_ZC + vo:OFF_ZC + vo + C_V_DIM]
            s0 = st_ref[i, h]
            intra = jnp.sum(q * k, axis=-1, keepdims=True) * v
            cross = jnp.sum(_column(q * gamma) * s0, axis=0, keepdims=True)
            st_out[i, h] = gamma * s0 + _column(k) * v
            yc_ref[i, :, vo:vo + C_V_DIM] = _head_norm_gate(intra + cross, w_ref[:, vo:vo + C_V_DIM].astype(F32), z)


def _sample_mixers(flat, caches, state, state_stack, layer, log_gamma, slopes_a, slopes_b, sinks, ret_w_row):
    n, width = flat.shape
    nc = SAMPLE_CHUNK
    units = width // HEAD_DIM
    hs = flat.reshape(n, units, HEAD_DIM)
    p4 = flat.reshape(n // nc, nc, 1, width)
    depth = caches[0].shape[0]
    views = [caches[0]]
    for c, (win, dil) in zip(caches[1:], B_GROUPS):
        views.append(c.reshape(depth, n, BAND, dil, 2, B_HEADS, HEAD_DIM))

    def lanes(vec):
        return jnp.broadcast_to(vec.astype(F32)[:, None], (vec.shape[0], HEAD_DIM))

    def cache_b_spec():
        return pl.BlockSpec((None, nc, BAND, None, 2, B_HEADS, HEAD_DIM), lambda i: (layer, i, 0, 0, 0, 0, 0))

    args = [log_gamma, lanes(slopes_a), lanes(slopes_b), lanes(sinks), hs, p4, *views, state, ret_w_row]
    in_specs = [_smem(),
                pl.BlockSpec((A_HEADS, HEAD_DIM), lambda i: (0, 0)),
                pl.BlockSpec((len(B_GROUPS) * B_HEADS, HEAD_DIM), lambda i: (0, 0)),
                pl.BlockSpec((A_HEADS, HEAD_DIM), lambda i: (0, 0)),
                pl.BlockSpec((nc, units, HEAD_DIM), lambda i: (i, 0, 0)),
                pl.BlockSpec((None, nc, 1, width), lambda i: (i, 0, 0, 0)),
                pl.BlockSpec((None, nc, BAND, 2, A_KV_HEADS, HEAD_DIM), lambda i: (layer, i, 0, 0, 0, 0)),
                cache_b_spec(), cache_b_spec(), cache_b_spec(),
                pl.BlockSpec((None, nc, C_HEADS, C_QK_DIM, C_V_DIM), lambda i: (layer, i, 0, 0, 0)),
                pl.BlockSpec((1, C_WIDTH), lambda i: (0, 0))]
    aliases = {}
    if state_stack is not None:
        in_specs.append(pl.BlockSpec(memory_space=pl.ANY))
        aliases = {len(args): 2}
        args.append(state_stack)
    yab, yc, st = pl.pallas_call(
        _sample_kernel,
        grid=(n // nc,),
        in_specs=in_specs,
        out_specs=[pl.BlockSpec((nc, A_HEADS + B_HEADS, HEAD_DIM), lambda i: (i, 0, 0)),
                   pl.BlockSpec((None, nc, 1, C_WIDTH), lambda i: (i, 0, 0, 0)),
                   pl.BlockSpec((None, nc, C_HEADS, C_QK_DIM, C_V_DIM), lambda i: (layer, i, 0, 0, 0))],
        out_shape=[jax.ShapeDtypeStruct((n, A_HEADS + B_HEADS, HEAD_DIM), F32),
                   jax.ShapeDtypeStruct((n // nc, nc, 1, C_WIDTH), F32),
                   jax.ShapeDtypeStruct((depth, n, C_HEADS, C_QK_DIM, C_V_DIM), F32)],
        input_output_aliases=aliases,
        compiler_params=_params("arbitrary"),
        name="sample_mixers",
    )(*args)
    y = jnp.concatenate([yab.reshape(n, A_WIDTH + B_WIDTH), yc.reshape(n, C_WIDTH)], axis=-1)
    return y, st


def _kv_rows(k_arr, k_col, v_arr, v_col, batch, heads, rows):
    w = heads * HEAD_DIM
    t = k_arr.shape[0] // batch
    k = k_arr.reshape(batch, t, -1)[:, t - rows:, k_col:k_col + w]
    v = v_arr.reshape(batch, t, -1)[:, t - rows:, v_col:v_col + w]
    return jnp.stack([k, v], axis=2).reshape(batch, rows, 2, heads, HEAD_DIM).astype(F32)


def _window_rows(main, strided, main_runs, batch):
    t = main.shape[0] // batch
    out = [_kv_rows(main, _compact(main_runs, OFF_KA), main, _compact(main_runs, OFF_VA), batch,
                    A_KV_HEADS, min(BAND, t))]
    for g, (win, _) in enumerate(B_GROUPS):
        runs, arr = (main_runs, main) if g == 0 else (_STRIDED_RUNS, strided)
        out.append(_kv_rows(arr, _compact(runs, OFF_KB + g * B_WIDTH), arr, _compact(runs, OFF_VB + g * B_WIDTH),
                            batch, B_HEADS, min(win, t)))
    return out


def _kv_rows_kernel(*refs, pieces):
    k_refs, v_refs, o_ref = refs[:pieces], refs[pieces:2 * pieces], refs[-1]
    rows, _, heads, dh = o_ref.shape
    for slot, part_refs in enumerate((k_refs, v_refs)):
        flat = jnp.concatenate([r[...].astype(F32) for r in part_refs], axis=-1)
        o_ref[:, slot] = flat.reshape(rows, heads, dh)


def _kv_rows_into(stack, layer, depth, k_arr, k_col, v_arr, v_col, batch, heads, rows):
    w = heads * HEAD_DIM
    pw = min(w, TN)
    pieces = w // pw
    t = k_arr.shape[0] // batch
    tr = min(rows, 512)
    first, per_seq = (t - rows) // tr, t // tr

    def piece(col):
        return pl.BlockSpec((tr, pw), lambda b, i: (b * per_seq + first + i, col // pw))

    in_specs = ([piece(k_col + p * pw) for p in range(pieces)] + [piece(v_col + p * pw) for p in range(pieces)])
    args = [k_arr] * pieces + [v_arr] * pieces
    aliases = {}
    if stack is not None:
        in_specs.append(pl.BlockSpec(memory_space=pl.ANY))
        args.append(stack)
        aliases = {2 * pieces: 0}
    return pl.pallas_call(
        functools.partial(_kv_rows_kernel, pieces=pieces),
        grid=(batch, rows // tr),
        in_specs=in_specs,
        out_specs=pl.BlockSpec((None, None, tr, 2, heads, HEAD_DIM), lambda b, i: (layer, b, i, 0, 0, 0)),
        out_shape=jax.ShapeDtypeStruct((depth, batch, rows, 2, heads, HEAD_DIM), F32),
        input_output_aliases=aliases,
        compiler_params=_params("arbitrary", "arbitrary"),
        name="kv_rows",
    )(*args)


def _window_rows_into(stacks, layer, depth, main, main_runs, batch):
    t = main.shape[0] // batch
    stacks = stacks or [None, None]
    return [_kv_rows_into(stacks[0], layer, depth, main, _compact(main_runs, OFF_KA), main,
                          _compact(main_runs, OFF_VA), batch, A_KV_HEADS, min(BAND, t)),
            _kv_rows_into(stacks[1], layer, depth, main, _compact(main_runs, OFF_KB), main,
                          _compact(main_runs, OFF_VB), batch, B_HEADS, min(B_GROUPS[0][0], t))]


def _reference_order(main, strided, main_runs, upto):
    pieces = []
    for (lo, hi), src in sorted([(r, main) for r in main_runs] + [(r, strided) for r in _STRIDED_RUNS]):
        if lo >= upto:
            break
        runs = main_runs if src is main else _STRIDED_RUNS
        c = _compact(runs, lo)
        pieces.append(src[:, c:c + min(hi, upto) - lo])
    return jnp.concatenate(pieces, axis=-1)


def kernel(x_prompt, x_sample, cache_a, cache_b1, cache_b2, cache_b3, state_c, norm_w, w_in,
           w_branch_a, w_branch_b, w_branch_c, w_out, attn_sinks, ret_norm_w, final_norm_w):
    batch, seq, d_model = x_prompt.shape
    n_s, t_s, _ = x_sample.shape
    depth = w_in.shape[0]
    width = w_in.shape[-1]
    assert t_s == 1 and seq == B_GROUPS[-1][1] * BAND and n_s % SAMPLE_CHUNK == 0
    assert width == OFF_G + 3 * d_model and d_model % TN == 0
    assert cache_a.shape[2] == BAND
    assert all(c.shape[2] == win for c, (win, _) in zip((cache_b1, cache_b2, cache_b3), B_GROUPS))

    slopes_a = jnp.exp2(-8.0 * jnp.arange(1, A_HEADS + 1, dtype=F32) / A_HEADS)
    nbh = len(B_GROUPS) * B_HEADS
    slopes_b = jnp.exp2(-8.0 * jnp.arange(1, nbh + 1, dtype=F32) / nbh)
    log_gamma = jnp.log1p(-jnp.exp2(-5.0 - jnp.arange(C_HEADS, dtype=F32)))
    caches = (cache_a, cache_b1, cache_b2, cache_b3)
    main_runs = _main_runs(width)
    g_col = _compact(main_runs, OFF_G)

    xp = x_prompt.reshape(batch * seq, d_model)
    xs = x_sample.reshape(n_s, d_model)
    window_rows = [min(BAND, seq)] + [min(win, seq) for win, _ in B_GROUPS]
    window_heads = [A_KV_HEADS] + [B_HEADS] * len(B_GROUPS)
    p_window = [jnp.zeros((depth, batch, r, 2, h, HEAD_DIM), F32) for r, h in zip(window_rows, window_heads)]
    s_c = jnp.zeros((depth, n_s, C_HEADS, C_QK_DIM, C_V_DIM), F32)
    p_c, s_states = [], []
    for l in range(depth):
        nw = norm_w[l].reshape(1, d_model)
        rw = ret_norm_w[l].reshape(1, C_WIDTH)
        sinks = attn_sinks[l]

        hp = _rmsnorm(xp, nw, BF16)
        hsm = _rmsnorm(xs, nw, BF16)
        main, main_s = _matmul(hp, hsm, w_in, l, BF16, runs=main_runs, name="proj_main")
        strided_states = tuple(
            (p_window[1 + g], depth,
             _compact(_STRIDED_RUNS, OFF_KB + g * B_WIDTH) // WIDE, _compact(_STRIDED_RUNS, OFF_VB + g * B_WIDTH) // WIDE,
             min(B_GROUPS[g][0], seq)) for g in (1, 2))
        strided, strided_s, *strided_stacks = _matmul(hp, hsm, w_in, l, F32, runs=_STRIDED_RUNS,
                                                      states=strided_states, seq=seq, name="proj_strided")

        ya = _attn_a(main, slopes_a, sinks, main_runs, batch, seq)
        yb = _attn_b(main, strided, slopes_b, main_runs, batch, seq)
        yc, new_c = _retention(main, log_gamma, rw, main_runs, batch, seq)
        y_s, s_c = _sample_mixers(_reference_order(main_s, strided_s, main_runs, OFF_G), caches, state_c, s_c, l,
                                  log_gamma, slopes_a, slopes_b, sinks, rw)

        mixed, mixed_s = _branch_merge(ya, yb, yc, y_s, w_branch_a, w_branch_b, w_branch_c,
                                       main, main_s, g_col, l, d_model)
        xp, xs = _matmul(mixed, mixed_s, w_out, l, F32, residual=(xp, xs), name="proj_out")
        p_window = _window_rows_into(p_window, l, depth, main, main_runs, batch) + strided_stacks
        p_c.append(new_c)
        s_states.append(_window_rows(main_s, strided_s, main_runs, n_s))

    fw = final_norm_w.reshape(1, d_model)
    y_prompt = _rmsnorm(xp, fw, F32).reshape(batch, seq, d_model)
    y_sample = _rmsnorm(xs, fw, F32).reshape(n_s, 1, d_model)
    sample = [jnp.stack([s[i] for s in s_states]) for i in range(4)]
    return (y_prompt, y_sample, *p_window, jnp.stack(p_c), *sample, s_c)
```
